```python
import jax
import jax.numpy as jnp
from jax import lax
import numpy as np

D_MODEL = 1024
BATCH = 2
SEQ = 8192
DEPTH = 2
DEC_BATCH = 128
DEC_SEQ = 1
PAST_LEN = 8192
PAGE_SIZE = 128

N_EVEN = (DEPTH + 1) // 2
N_ODD = DEPTH // 2
A_HEADS = 8
A_KV_HEADS = 2
A_HEAD_DIM = 64
A_GROUP = A_HEADS // A_KV_HEADS
A_Q = A_HEADS * A_HEAD_DIM
A_KV = A_KV_HEADS * A_HEAD_DIM
WINDOW = 128
A_BLOCK = 128
B_HEADS = 4
B_KEY_DIM = 128
B_VAL_DIM = 128
B_FDIM = B_HEADS * B_KEY_DIM
B_WIDTH = B_HEADS * B_VAL_DIM
B_CHUNK = 64
EVEN_IN = A_Q + 2 * A_KV + 2 * B_FDIM + 2 * B_WIDTH
EVEN_OUT = A_Q + B_WIDTH
C_HEADS = 8
C_KEY_DIM = 128
C_VAL_DIM = 128
C_QK = C_HEADS * C_KEY_DIM
C_V = C_HEADS * C_VAL_DIM
C_CONV = 4
C_CONV_DIM = 2 * C_QK + C_V
C_CHUNK = 64
ODD_IN = C_CONV_DIM + C_V + 2 * C_HEADS
D_FF = 2816
N_MOD = 9
DN_ALPHA = (2 * DEPTH) ** 0.25
DN_BETA = (8 * DEPTH) ** -0.25
LN_EPS = 1e-5
NORM_EPS = 1e-6

kernel_name = 'hybrid_swa_hgrn2_gdn_step'


def layer_norm(x, g, b):
    xf = x.astype(jnp.float32)
    mu = jnp.mean(xf, axis=-1, keepdims=True)
    var = jnp.mean(jnp.square(xf - mu), axis=-1, keepdims=True)
    y = (xf - mu) * lax.rsqrt(var + LN_EPS)
    return (y * g.astype(jnp.float32) + b.astype(jnp.float32)).astype(x.dtype)


def rms_norm(x, g):
    xf = x.astype(jnp.float32)
    y = xf * lax.rsqrt(jnp.mean(jnp.square(xf), axis=-1, keepdims=True) + NORM_EPS)
    return y * g.astype(jnp.float32)


def l2_normalize(x):
    xf = x.astype(jnp.float32)
    return xf * lax.rsqrt(jnp.sum(jnp.square(xf), axis=-1, keepdims=True) + NORM_EPS)


def pad_time(t, n):
    return jnp.pad(t, [(0, 0), (0, n)] + [(0, 0)] * (t.ndim - 2))


def alibi_slopes():
    return jnp.asarray(2.0 ** (-8.0 * np.arange(1, A_HEADS + 1) / A_HEADS), dtype=jnp.float32)


def swiglu(h, w_up, w_down):
    gate, up = jnp.split(h @ w_up, 2, axis=-1)
    return (jax.nn.silu(gate) * up) @ w_down


def sink_window_attention(q, k, v, q_pos, k_pos, sinks):
    s = jnp.einsum('...qhgd,...khd->...hgqk', q, k).astype(jnp.float32) * (A_HEAD_DIM ** -0.5)
    dist = q_pos[..., :, None] - k_pos[..., None, :]
    valid = (dist >= 0) & (dist < WINDOW) & (k_pos[..., None, :] >= 0)
    dist = dist[..., None, None, :, :].astype(jnp.float32)
    valid = valid[..., None, None, :, :]
    slopes = alibi_slopes().reshape(A_KV_HEADS, A_GROUP, 1, 1)
    s = jnp.where(valid, s - slopes * dist, -jnp.inf)
    sink = sinks.astype(jnp.float32).reshape(A_KV_HEADS, A_GROUP, 1, 1)
    m = jnp.maximum(jnp.max(s, axis=-1, keepdims=True), sink)
    p = jnp.exp(s - m)
    p = p / (jnp.sum(p, axis=-1, keepdims=True) + jnp.exp(sink - m))
    return jnp.einsum('...hgqk,...khd->...qhgd', p.astype(v.dtype), v)


def swa_prompt(q, k, v, sinks):
    Bn, S_ = q.shape[:2]
    nb = S_ // A_BLOCK
    qb = q.reshape(Bn, nb, A_BLOCK, A_KV_HEADS, A_GROUP, A_HEAD_DIM)
    kp = jnp.pad(k, ((0, 0), (A_BLOCK, 0), (0, 0), (0, 0))).reshape(Bn, nb + 1, A_BLOCK, A_KV_HEADS, A_HEAD_DIM)
    vp = jnp.pad(v, ((0, 0), (A_BLOCK, 0), (0, 0), (0, 0))).reshape(Bn, nb + 1, A_BLOCK, A_KV_HEADS, A_HEAD_DIM)
    kb = jnp.concatenate([kp[:, :-1], kp[:, 1:]], axis=2)
    vb = jnp.concatenate([vp[:, :-1], vp[:, 1:]], axis=2)
    pos = jnp.arange(S_).reshape(nb, A_BLOCK)
    k_pos = jnp.concatenate([pos - A_BLOCK, pos], axis=1)
    o = sink_window_attention(qb, kb, vb, pos, k_pos, sinks)
    return o.reshape(Bn, S_, A_Q)


def hgrn2_scan(q, logf, k, v, S0):
    Bn, T, H, DK = q.shape
    DV = v.shape[-1]
    C = min(B_CHUNK, T)
    nc = -(-T // C)
    pad = nc * C - T
    q, logf, k, v = (pad_time(t, pad) for t in (q, logf, k, v))

    def blocks(t):
        return jnp.moveaxis(t.reshape((Bn, nc, C) + t.shape[2:]), 1, 0)

    causal = jnp.tril(jnp.ones((C, C), dtype=bool))[None, :, :, None, None]

    def step(S, xs):
        q_c, lf_c, k_c, v_c = xs
        A = jnp.cumsum(lf_c, axis=1)
        decay = jnp.exp(jnp.where(causal, A[:, :, None] - A[:, None, :], -jnp.inf))
        scores = jnp.einsum('bthk,bshk,btshk->bhts', q_c, k_c, decay)
        o = jnp.einsum('bhts,bshv->bthv', scores, v_c) + jnp.einsum('bthk,bhkv->bthv', q_c * jnp.exp(A), S)
        A_last = A[:, -1]
        S = jnp.exp(A_last)[..., None] * S + jnp.einsum('bshk,bshv->bhkv', k_c * jnp.exp(A_last[:, None] - A), v_c)
        return S, o

    S, o = lax.scan(step, S0, tuple(blocks(t) for t in (q, logf, k, v)))
    o = jnp.moveaxis(o, 0, 1).reshape(Bn, nc * C, H, DV)[:, :T]
    return o, S


def gated_delta_scan(q, k, v, log_alpha, beta, S0):
    Bn, T, H, DK = q.shape
    DV = v.shape[-1]
    C = min(C_CHUNK, T)
    nc = -(-T // C)
    pad = nc * C - T
    q, k, v, log_alpha, beta = (pad_time(t, pad) for t in (q, k, v, log_alpha, beta))

    def blocks(t):
        return jnp.moveaxis(t.reshape((Bn, nc, C, H) + t.shape[3:]), 3, 2)

    q, k, v, log_alpha, beta = (blocks(t) for t in (q, k, v, log_alpha, beta))
    G = jnp.cumsum(log_alpha, axis=-1)
    diff = G[..., :, None] - G[..., None, :]
    causal = jnp.tril(jnp.ones((C, C), dtype=bool))
    strict = jnp.tril(jnp.ones((C, C), dtype=bool), k=-1)
    decay_incl = jnp.exp(jnp.where(causal, diff, -jnp.inf))
    decay_strict = jnp.where(strict, decay_incl, 0.0)
    L = beta[..., :, None] * jnp.einsum('bnhtk,bnhsk->bnhts', k, k) * decay_strict
    M = L + jnp.eye(C, dtype=L.dtype)
    u = lax.linalg.triangular_solve(M, beta[..., None] * v, left_side=True, lower=True, unit_diagonal=True)
    w = lax.linalg.triangular_solve(M, (beta * jnp.exp(G))[..., None] * k, left_side=True, lower=True, unit_diagonal=True)
    qk = jnp.einsum('bnhtk,bnhsk->bnhts', q, k) * decay_incl
    q_dec = q * jnp.exp(G)[..., None]
    k_dec = k * jnp.exp(G[..., -1:] - G)[..., None]
    g_last = jnp.exp(G[..., -1])

    def step(S, xs):
        q_c, qk_c, u_c, w_c, k_c, gl_c = xs
        delta = u_c - jnp.einsum('bhtk,bhkv->bhtv', w_c, S)
        o = jnp.einsum('bhtk,bhkv->bhtv', q_c, S) + jnp.einsum('bhts,bhsv->bhtv', qk_c, delta)
        S = gl_c[..., None, None] * S + jnp.einsum('bhsk,bhsv->bhkv', k_c, delta)
        return S, o

    xs = tuple(jnp.moveaxis(t, 1, 0) for t in (q_dec, qk, u, w, k_dec, g_last))
    S, o = lax.scan(step, S0, xs)
    o = jnp.moveaxis(jnp.moveaxis(o, 0, 1), 2, 3).reshape(Bn, nc * C, H, DV)[:, :T]
    return o, S


def even_mixer(h, cache_k, cache_v, S0, w_in, w_out, sinks, norm_g, lower_bound):
    Bn, T, _ = h.shape
    splits = np.cumsum([A_Q, A_KV, A_KV, B_FDIM, B_FDIM, B_WIDTH]).tolist()
    q_a, k_a, v_a, q_b, f_b, i_b, g_b = jnp.split(h @ w_in, splits, axis=-1)
    q_a = q_a.reshape(Bn, T, A_KV_HEADS, A_GROUP, A_HEAD_DIM)
    k_a = k_a.reshape(Bn, T, A_KV_HEADS, A_HEAD_DIM)
    v_a = v_a.reshape(Bn, T, A_KV_HEADS, A_HEAD_DIM)
    if cache_k is None:
        o_a = swa_prompt(q_a, k_a, v_a, sinks)
        new_k, new_v = k_a[:, -WINDOW:], v_a[:, -WINDOW:]
    else:
        keys = jnp.concatenate([cache_k, k_a], axis=1)
        vals = jnp.concatenate([cache_v, v_a], axis=1)
        q_pos = PAST_LEN + jnp.arange(T)
        k_pos = PAST_LEN - WINDOW + jnp.arange(WINDOW + T)
        o_a = sink_window_attention(q_a, keys, vals, q_pos, k_pos, sinks).reshape(Bn, T, A_Q)
        new_k, new_v = keys[:, -WINDOW:], vals[:, -WINDOW:]
    qb = jax.nn.silu(q_b.astype(jnp.float32)).reshape(Bn, T, B_HEADS, B_KEY_DIM)
    lb = lower_bound.reshape(B_HEADS, B_KEY_DIM)
    f = lb + (1.0 - lb) * jax.nn.sigmoid(f_b.astype(jnp.float32).reshape(Bn, T, B_HEADS, B_KEY_DIM))
    vb = i_b.astype(jnp.float32).reshape(Bn, T, B_HEADS, B_VAL_DIM)
    o_b, S_new = hgrn2_scan(qb, jnp.log(f), 1.0 - f, vb, S0)
    o_b = rms_norm(o_b, norm_g) * jax.nn.silu(g_b.astype(jnp.float32).reshape(Bn, T, B_HEADS, B_VAL_DIM))
    o = jnp.concatenate([o_a, o_b.astype(h.dtype).reshape(Bn, T, B_WIDTH)], axis=-1)
    return o @ w_out, new_k, new_v, S_new


def odd_mixer(h, conv_hist, S0, w_in, w_out, conv_w, a_log, dt_bias, norm_g):
    Bn, T, _ = h.shape
    qkv, gate, a_in, b_in = jnp.split(h @ w_in, [C_CONV_DIM, C_CONV_DIM + C_V, C_CONV_DIM + C_V + C_HEADS], axis=-1)
    full = jnp.concatenate([conv_hist, qkv], axis=1)
    acc = full[:, 0:T] * conv_w[0]
    for j in range(1, C_CONV):
        acc = acc + full[:, j:j + T] * conv_w[j]
    qkv_c = jax.nn.silu(acc)
    new_hist = full[:, T:]
    q, k, v = jnp.split(qkv_c, [C_QK, 2 * C_QK], axis=-1)
    q = l2_normalize(q.reshape(Bn, T, C_HEADS, C_KEY_DIM)) * (C_KEY_DIM ** -0.5)
    k = l2_normalize(k.reshape(Bn, T, C_HEADS, C_KEY_DIM))
    v = v.astype(jnp.float32).reshape(Bn, T, C_HEADS, C_VAL_DIM)
    beta = jax.nn.sigmoid(b_in.astype(jnp.float32))
    log_alpha = -jnp.exp(a_log.astype(jnp.float32)) * jax.nn.softplus(a_in.astype(jnp.float32) + dt_bias.astype(jnp.float32))
    o, S_new = gated_delta_scan(q, k, v, log_alpha, beta, S0)
    o = rms_norm(o, norm_g) * jax.nn.silu(gate.astype(jnp.float32).reshape(Bn, T, C_HEADS, C_VAL_DIM))
    return o.astype(h.dtype).reshape(Bn, T, C_V) @ w_out, new_hist, S_new


def run_trunk(x, c, caches, p):
    Bn = x.shape[0]
    cs = jax.nn.silu(c)
    probs = jax.nn.softmax(p['hgrn_lb_logits'].astype(jnp.float32), axis=0)
    lower = jnp.cumsum(probs, axis=0)[1:] - probs[0]
    ks, vs, hs, gs, cvs = [], [], [], [], []
    for l in range(DEPTH):
        mods = (cs @ p['ada_w'][l] + p['ada_b'][l]).reshape(Bn, N_MOD, 1, D_MODEL)
        sh1, sc1, g1, sh2, sc2, g2, sh3, sc3, g3 = (mods[:, j] for j in range(N_MOD))
        ffn1 = swiglu(x * (1.0 + sc1) + sh1, p['ffn_w_up'][l, 0], p['ffn_w_down'][l, 0])
        x = layer_norm(DN_ALPHA * x + 0.5 * g1 * ffn1, p['ln_g'][l, 0], p['ln_b'][l, 0])
        hm = x * (1.0 + sc2) + sh2
        if l % 2 == 0:
            e = l // 2
            if caches is None:
                ck, cv = None, None
                S0 = jnp.zeros((Bn, B_HEADS, B_KEY_DIM, B_VAL_DIM), jnp.float32)
            else:
                ck, cv = caches[0][e], caches[1][e]
                S0 = caches[2][e].astype(jnp.float32)
            mix, nk, nv, nS = even_mixer(hm, ck, cv, S0, p['even_w_in'][e], p['even_w_out'][e], p['swa_sinks'][e], p['hgrn_norm_g'][e], lower[e])
            ks.append(nk)
            vs.append(nv)
            hs.append(nS)
        else:
            o_idx = l // 2
            if caches is None:
                hist = jnp.zeros((Bn, C_CONV - 1, C_CONV_DIM), x.dtype)
                S0 = jnp.zeros((Bn, C_HEADS, C_KEY_DIM, C_VAL_DIM), jnp.float32)
            else:
                hist = caches[4][o_idx]
                S0 = caches[3][o_idx].astype(jnp.float32)
            mix, nh, nS = odd_mixer(hm, hist, S0, p['odd_w_in'][o_idx], p['odd_w_out'][o_idx], p['gdn_conv_w'][o_idx], p['gdn_a_log'][o_idx], p['gdn_dt_bias'][o_idx], p['gdn_norm_g'][o_idx])
            gs.append(nS)
            cvs.append(nh)
        x = layer_norm(DN_ALPHA * x + g2 * mix, p['ln_g'][l, 1], p['ln_b'][l, 1])
        ffn2 = swiglu(x * (1.0 + sc3) + sh3, p['ffn_w_up'][l, 1], p['ffn_w_down'][l, 1])
        x = layer_norm(DN_ALPHA * x + 0.5 * g3 * ffn2, p['ln_g'][l, 2], p['ln_b'][l, 2])
    dt = x.dtype
    return x, jnp.stack(ks).astype(dt), jnp.stack(vs).astype(dt), jnp.stack(hs).astype(dt), jnp.stack(gs).astype(dt), jnp.stack(cvs).astype(dt)


def setup_inputs(seed: int = 0) -> dict:
    key = jax.random.key(seed)
    keys = iter(jax.random.split(key, 32))

    def nrm(shape, scale):
        return jax.random.normal(next(keys), shape, jnp.float32) * scale

    dt = jnp.exp(jax.random.uniform(next(keys), (N_ODD, C_HEADS), jnp.float32, minval=np.log(1e-3), maxval=np.log(1e-1)))
    return {
        'x_prompt': nrm((BATCH, SEQ, D_MODEL), 1.0),
        'x_sample': nrm((DEC_BATCH, DEC_SEQ, D_MODEL), 1.0),
        'cache_swa_k': nrm((N_EVEN, DEC_BATCH, WINDOW, A_KV_HEADS, A_HEAD_DIM), 1.0),
        'cache_swa_v': nrm((N_EVEN, DEC_BATCH, WINDOW, A_KV_HEADS, A_HEAD_DIM), 1.0),
        'state_hgrn': nrm((N_EVEN, DEC_BATCH, B_HEADS, B_KEY_DIM, B_VAL_DIM), 0.5),
        'state_gdn': nrm((N_ODD, DEC_BATCH, C_HEADS, C_KEY_DIM, C_VAL_DIM), 0.1),
        'state_gdn_conv': nrm((N_ODD, DEC_BATCH, C_CONV - 1, C_CONV_DIM), 1.0),
        'c_prompt': nrm((BATCH, D_MODEL), 1.0),
        'c_sample': nrm((DEC_BATCH, D_MODEL), 1.0),
        'ada_w': nrm((DEPTH, D_MODEL, N_MOD * D_MODEL), D_MODEL ** -0.5),
        'ada_b': nrm((DEPTH, N_MOD * D_MODEL), 0.01),
        'ln_g': 1.0 + nrm((DEPTH, 3, D_MODEL), 0.02),
        'ln_b': nrm((DEPTH, 3, D_MODEL), 0.02),
        'ffn_w_up': nrm((DEPTH, 2, D_MODEL, 2 * D_FF), D_MODEL ** -0.5),
        'ffn_w_down': nrm((DEPTH, 2, D_FF, D_MODEL), D_FF ** -0.5 * DN_BETA),
        'even_w_in': nrm((N_EVEN, D_MODEL, EVEN_IN), D_MODEL ** -0.5),
        'even_w_out': nrm((N_EVEN, EVEN_OUT, D_MODEL), EVEN_OUT ** -0.5 * DN_BETA),
        'swa_sinks': nrm((N_EVEN, A_HEADS), 1.0),
        'hgrn_norm_g': 1.0 + nrm((N_EVEN, B_VAL_DIM), 0.02),
        'hgrn_lb_logits': nrm((N_EVEN + 1, B_FDIM), 0.5),
        'odd_w_in': nrm((N_ODD, D_MODEL, ODD_IN), D_MODEL ** -0.5),
        'odd_w_out': nrm((N_ODD, C_V, D_MODEL), C_V ** -0.5 * DN_BETA),
        'gdn_conv_w': nrm((N_ODD, C_CONV, C_CONV_DIM), C_CONV ** -0.5),
        'gdn_a_log': jnp.log(jax.random.uniform(next(keys), (N_ODD, C_HEADS), jnp.float32, minval=1.0, maxval=16.0)),
        'gdn_dt_bias': dt + jnp.log(-jnp.expm1(-dt)),
        'gdn_norm_g': 1.0 + nrm((N_ODD, C_VAL_DIM), 0.02),
    }


def reference(x_prompt, x_sample, cache_swa_k, cache_swa_v, state_hgrn, state_gdn, state_gdn_conv, c_prompt, c_sample, ada_w, ada_b, ln_g, ln_b, ffn_w_up, ffn_w_down, even_w_in, even_w_out, swa_sinks, hgrn_norm_g, hgrn_lb_logits, odd_w_in, odd_w_out, gdn_conv_w, gdn_a_log, gdn_dt_bias, gdn_norm_g):
    p = dict(ada_w=ada_w, ada_b=ada_b, ln_g=ln_g, ln_b=ln_b, ffn_w_up=ffn_w_up, ffn_w_down=ffn_w_down, even_w_in=even_w_in, even_w_out=even_w_out, swa_sinks=swa_sinks, hgrn_norm_g=hgrn_norm_g, hgrn_lb_logits=hgrn_lb_logits, odd_w_in=odd_w_in, odd_w_out=odd_w_out, gdn_conv_w=gdn_conv_w, gdn_a_log=gdn_a_log, gdn_dt_bias=gdn_dt_bias, gdn_norm_g=gdn_norm_g)
    y_prompt, p_k, p_v, p_hgrn, p_gdn, p_conv = run_trunk(x_prompt, c_prompt, None, p)
    y_sample, s_k, s_v, s_hgrn, s_gdn, s_conv = run_trunk(x_sample, c_sample, (cache_swa_k, cache_swa_v, state_hgrn, state_gdn, state_gdn_conv), p)
    return (y_prompt, y_sample, p_k, p_v, p_hgrn, p_gdn, p_conv, s_k, s_v, s_hgrn, s_gdn, s_conv)
```

```python
import functools

import jax
import jax.numpy as jnp
from jax import lax
from jax.experimental import pallas as pl
from jax.experimental.pallas import tpu as pltpu

F32 = jnp.float32
BF16 = jnp.bfloat16
HIGHEST = lax.Precision.HIGHEST

D_MODEL = 1024
DEPTH = 2
WINDOW = 128
A_HEADS = 8
A_KV_HEADS = 2
A_GROUP = A_HEADS // A_KV_HEADS
A_HEAD_DIM = 64
A_Q = A_HEADS * A_HEAD_DIM
A_KV = A_KV_HEADS * A_HEAD_DIM
B_HEADS = 4
B_DIM = 128
EVEN_IN = 2816
C_HEADS = 8
C_DIM = 128
C_QK = C_HEADS * C_DIM
C_CONV = 4
C_CONV_DIM = 3 * C_QK
ODD_IN = C_CONV_DIM + C_QK + 2 * C_HEADS
ODD_IN_PAD = 4224
D_FF = 2816
N_MOD = 9
DN_ALPHA = (2 * DEPTH) ** 0.25
LN_EPS = 1e-5
NORM_EPS = 1e-6
LANES = 128

FF_TILE = 1408
PROJ_TILE = 1408
HGRN_BLOCK = 128
HGRN_SUB = 16
GDN_CHUNK_LOG = 7
GDN_CHUNK = 1 << GDN_CHUNK_LOG
GDN_BASE_LOG = 3
DEC_BLOCK = 64
VMEM_LIMIT = 48 << 20


def _params(sem, vmem=VMEM_LIMIT):
    return pltpu.CompilerParams(dimension_semantics=sem, vmem_limit_bytes=vmem)


def _silu(x):
    return x * jax.nn.sigmoid(x)


def _softplus(x):
    return jnp.maximum(x, 0.0) + jnp.log(1.0 + jnp.exp(-jnp.abs(x)))


def _layer_norm(y, g, b):
    mu = jnp.mean(y, axis=-1, keepdims=True)
    d = y - mu
    var = jnp.mean(d * d, axis=-1, keepdims=True)
    return d * lax.rsqrt(var + LN_EPS) * g + b


def _rms_gate(o, norm_g, gate):
    y = o * lax.rsqrt(jnp.mean(o * o, axis=-1, keepdims=True) + NORM_EPS)
    return y * norm_g * _silu(gate)


def _dot(a, b, precision=None):
    return jnp.dot(a, b, preferred_element_type=F32, precision=precision)


def _dot_nt(a, b, precision=None):
    return lax.dot_general(a, b, (((1,), (1,)), ((), ())), preferred_element_type=F32, precision=precision)


def _ada_kernel(c_ref, w_ref, b_ref, o_ref):
    cs = _silu(c_ref[...]).astype(BF16)
    o_ref[...] = _dot(cs, w_ref[...].astype(BF16)) + b_ref[...]


def _ada_mods(c_all, ada_w, ada_b):
    m = c_all.shape[0]
    n = N_MOD * D_MODEL
    tn = 1152
    return pl.pallas_call(
        _ada_kernel,
        grid=(DEPTH, n // tn),
        in_specs=[
            pl.BlockSpec((m, D_MODEL), lambda l, j: (0, 0)),
            pl.BlockSpec((None, D_MODEL, tn), lambda l, j: (l, 0, j)),
            pl.BlockSpec((None, 1, tn), lambda l, j: (l, 0, j)),
        ],
        out_specs=pl.BlockSpec((None, m, tn), lambda l, j: (l, 0, j)),
        out_shape=jax.ShapeDtypeStruct((DEPTH, m, n), F32),
        compiler_params=_params(("parallel", "parallel")),
        name="ada_mods",
    )(c_all, ada_w, ada_b.reshape(DEPTH, 1, n))


def _mod_spec(mods, k, tm, grid_rank):
    per_token = mods.shape[1] != 1
    rows = tm if per_token else 1
    if grid_rank == 3:
        return pl.BlockSpec((None, rows, D_MODEL), lambda b, i, j: (b, i if per_token else 0, k))
    return pl.BlockSpec((None, rows, D_MODEL), lambda b, i: (b, i if per_token else 0, k))


def _ffn_kernel(x_ref, sh_ref, sc_ref, g_ref, wg_ref, wu_ref, wd_ref, lg_ref, lb_ref, o_ref, h_ref, acc_ref):
    j = pl.program_id(2)

    @pl.when(j == 0)
    def _():
        h_ref[...] = (x_ref[...] * (1.0 + sc_ref[...]) + sh_ref[...]).astype(BF16)
        acc_ref[...] = jnp.zeros_like(acc_ref)

    h = h_ref[...]
    gate = _dot(h, wg_ref[...])
    up = _dot(h, wu_ref[...])
    act = (_silu(gate) * up).astype(BF16)
    acc_ref[...] += _dot(act, wd_ref[...])

    @pl.when(j == pl.num_programs(2) - 1)
    def _():
        y = DN_ALPHA * x_ref[...] + (0.5 * g_ref[...]) * acc_ref[...]
        o_ref[...] = _layer_norm(y, lg_ref[...], lb_ref[...])


def _ffn(x, mods, k0, w_up, w_down, ln_g, ln_b, tm):
    bsz, t, _ = x.shape
    nf = D_FF // FF_TILE
    row = pl.BlockSpec((None, tm, D_MODEL), lambda b, i, j: (b, i, 0))
    vec = pl.BlockSpec((1, D_MODEL), lambda b, i, j: (0, 0))
    return pl.pallas_call(
        _ffn_kernel,
        grid=(bsz, t // tm, nf),
        in_specs=[
            row,
            _mod_spec(mods, k0, tm, 3), _mod_spec(mods, k0 + 1, tm, 3), _mod_spec(mods, k0 + 2, tm, 3),
            pl.BlockSpec((D_MODEL, FF_TILE), lambda b, i, j: (0, j)),
            pl.BlockSpec((D_MODEL, FF_TILE), lambda b, i, j: (0, j + nf)),
            pl.BlockSpec((FF_TILE, D_MODEL), lambda b, i, j: (j, 0)),
            vec, vec,
        ],
        out_specs=row,
        out_shape=jax.ShapeDtypeStruct(x.shape, F32),
        scratch_shapes=[pltpu.VMEM((tm, D_MODEL), BF16), pltpu.VMEM((tm, D_MODEL), F32)],
        compiler_params=_params(("parallel", "parallel", "arbitrary")),
        name="ffn",
    )(x, mods, mods, mods, w_up, w_up, w_down, ln_g.reshape(1, D_MODEL), ln_b.reshape(1, D_MODEL))


def _inproj_kernel(x_ref, sh_ref, sc_ref, w_ref, o_ref, h_ref):
    @pl.when(pl.program_id(2) == 0)
    def _():
        h_ref[...] = (x_ref[...] * (1.0 + sc_ref[...]) + sh_ref[...]).astype(BF16)

    o_ref[...] = _dot(h_ref[...], w_ref[...])


def _inproj(x, mods, k0, w, tm):
    bsz, t, _ = x.shape
    n = w.shape[1]
    return pl.pallas_call(
        _inproj_kernel,
        grid=(bsz, t // tm, n // PROJ_TILE),
        in_specs=[
            pl.BlockSpec((None, tm, D_MODEL), lambda b, i, j: (b, i, 0)),
            _mod_spec(mods, k0, tm, 3), _mod_spec(mods, k0 + 1, tm, 3),
            pl.BlockSpec((D_MODEL, PROJ_TILE), lambda b, i, j: (0, j)),
        ],
        out_specs=pl.BlockSpec((None, tm, PROJ_TILE), lambda b, i, j: (b, i, j)),
        out_shape=jax.ShapeDtypeStruct((bsz, t, n), F32),
        scratch_shapes=[pltpu.VMEM((tm, D_MODEL), BF16)],
        compiler_params=_params(("parallel", "parallel", "arbitrary")),
        name="inproj",
    )(x, mods, mods, w)


def _outproj_kernel(x_ref, o1_ref, o2_ref, g_ref, w1_ref, w2_ref, lg_ref, lb_ref, y_ref):
    mix = _dot(o1_ref[...].astype(BF16), w1_ref[...]) + _dot(o2_ref[...].astype(BF16), w2_ref[...])
    y = DN_ALPHA * x_ref[...] + g_ref[...] * mix
    y_ref[...] = _layer_norm(y, lg_ref[...], lb_ref[...])


def _outproj(x, o1, c1, o2, c2, mods, kg, w_out, ln_g, ln_b, tm):
    bsz, t, _ = x.shape
    half = D_MODEL // 2
    row = pl.BlockSpec((None, tm, D_MODEL), lambda b, i: (b, i, 0))
    vec = pl.BlockSpec((1, D_MODEL), lambda b, i: (0, 0))
    return pl.pallas_call(
        _outproj_kernel,
        grid=(bsz, t // tm),
        in_specs=[
            row,
            pl.BlockSpec((None, tm, half), lambda b, i: (b, i, c1)),
            pl.BlockSpec((None, tm, half), lambda b, i: (b, i, c2)),
            _mod_spec(mods, kg, tm, 2),
            pl.BlockSpec((half, D_MODEL), lambda b, i: (0, 0)),
            pl.BlockSpec((half, D_MODEL), lambda b, i: (1, 0)),
            vec, vec,
        ],
        out_specs=row,
        out_shape=jax.ShapeDtypeStruct(x.shape, F32),
        compiler_params=_params(("parallel", "parallel")),
        name="outproj",
    )(x, o1, o2, mods, w_out, w_out, ln_g.reshape(1, D_MODEL), ln_b.reshape(1, D_MODEL))


def _swa_kernel(sink_ref, q_ref, kc_ref, kp_ref, vc_ref, vp_ref, o_ref):
    i = pl.program_id(1)
    q = q_ref[...]
    kc = kc_ref[...].astype(BF16)
    kp = kp_ref[...].astype(BF16)
    vc = vc_ref[...].astype(BF16)
    vp = vp_ref[...].astype(BF16)
    r = lax.broadcasted_iota(jnp.int32, (WINDOW, WINDOW), 0)
    c = lax.broadcasted_iota(jnp.int32, (WINDOW, WINDOW), 1)
    dist_c = (r - c).astype(F32)
    dist_p = dist_c + float(WINDOW)
    valid_c = c <= r
    valid_p = c > r + jnp.where(i > 0, 0, WINDOW)
    scale = A_HEAD_DIM ** -0.5
    for h in range(A_HEADS):
        kv = h // A_GROUP
        hs = slice(h * A_HEAD_DIM, (h + 1) * A_HEAD_DIM)
        ks = slice(kv * A_HEAD_DIM, (kv + 1) * A_HEAD_DIM)
        slope = 2.0 ** (-8.0 * (h + 1) / A_HEADS)
        qh = q[:, hs].astype(BF16)
        s_c = _dot_nt(qh, kc[:, ks]) * scale
        s_p = _dot_nt(qh, kp[:, ks]) * scale
        s_c = jnp.where(valid_c, s_c - slope * dist_c, -jnp.inf)
        s_p = jnp.where(valid_p, s_p - slope * dist_p, -jnp.inf)
        sink = sink_ref[h]
        m = jnp.maximum(jnp.max(s_c, axis=-1, keepdims=True), jnp.max(s_p, axis=-1, keepdims=True))
        m = jnp.maximum(m, sink)
        p_c = jnp.exp(s_c - m)
        p_p = jnp.exp(s_p - m)
        den = jnp.sum(p_c, axis=-1, keepdims=True) + jnp.sum(p_p, axis=-1, keepdims=True) + jnp.exp(sink - m)
        o = _dot(p_c.astype(BF16), vc[:, ks]) + _dot(p_p.astype(BF16), vp[:, ks])
        o_ref[:, hs] = o / den


def _swa_prompt(proj, sinks):
    bsz, t, _ = proj.shape
    nb = t // WINDOW
    kcol = A_Q // LANES
    vcol = kcol + 1

    def cur(col):
        return pl.BlockSpec((None, WINDOW, LANES), lambda b, i: (b, i, col))

    def prev(col):
        return pl.BlockSpec((None, WINDOW, LANES), lambda b, i: (b, jnp.maximum(i - 1, 0), col))

    return pl.pallas_call(
        _swa_kernel,
        grid=(bsz, nb),
        in_specs=[
            pl.BlockSpec(memory_space=pltpu.SMEM),
            pl.BlockSpec((None, WINDOW, A_Q), lambda b, i: (b, i, 0)),
            cur(kcol), prev(kcol), cur(vcol), prev(vcol),
        ],
        out_specs=pl.BlockSpec((None, WINDOW, A_Q), lambda b, i: (b, i, 0)),
        out_shape=jax.ShapeDtypeStruct((bsz, t, A_Q), F32),
        compiler_params=_params(("parallel", "parallel")),
        name="swa_prompt",
    )(sinks, proj, proj, proj, proj, proj)


def _hgrn_kernel(q_ref, f_ref, v_ref, g_ref, lb_ref, ng_ref, o_ref, s_ref, st_ref):
    i = pl.program_id(2)

    @pl.when(i == 0)
    def _():
        st_ref[...] = jnp.zeros_like(st_ref)

    lb = lb_ref[...]
    qb = _silu(q_ref[...])
    f = lb + (1.0 - lb) * jax.nn.sigmoid(f_ref[...])
    lf = jnp.log(f)
    kk = 1.0 - f
    v = v_ref[...]
    n = HGRN_BLOCK
    r = lax.broadcasted_iota(jnp.int32, (n, n), 0)
    c = lax.broadcasted_iota(jnp.int32, (n, n), 1)
    tri = jnp.where((c <= r) & (c >= r - (r & (HGRN_SUB - 1))), 1.0, 0.0).astype(F32)
    a_all = _dot(tri, lf, HIGHEST)
    v_t = v.T
    s_t = st_ref[...]
    trow = lax.broadcasted_iota(jnp.int32, (HGRN_SUB, 1), 0)
    outs = []
    for ch in range(n // HGRN_SUB):
        sl = slice(ch * HGRN_SUB, (ch + 1) * HGRN_SUB)
        a_c = a_all[sl]
        q_c = qb[sl]
        k_c = kk[sl]
        v_c = v[sl]
        a_last = a_c[HGRN_SUB - 1:HGRN_SUB]
        o_c = _dot_nt((q_c * jnp.exp(a_c)).astype(BF16), s_t.astype(BF16))
        for s in range(HGRN_SUB):
            e = jnp.exp(jnp.where(trow >= s, a_c - a_c[s:s + 1], -jnp.inf))
            col = jnp.sum(q_c * e * k_c[s:s + 1], axis=-1, keepdims=True)
            o_c = o_c + col * v_c[s:s + 1]
        k_dec = k_c * jnp.exp(a_last - a_c)
        s_t = s_t * jnp.exp(a_last) + _dot(v_t[:, sl].astype(BF16), k_dec.astype(BF16))
        outs.append(o_c)
    st_ref[...] = s_t
    o = jnp.concatenate(outs, axis=0)
    o_ref[...] = _rms_gate(o, ng_ref[...], g_ref[...])

    @pl.when(i == pl.num_programs(2) - 1)
    def _():
        s_ref[...] = s_t.T


def _hgrn_prompt(proj, lower, norm_g):
    bsz, t, _ = proj.shape
    base = (A_Q + 2 * A_KV) // LANES
    nh = B_HEADS

    def col(off):
        return pl.BlockSpec((None, HGRN_BLOCK, B_DIM), lambda b, h, i: (b, i, base + off * nh + h))

    return pl.pallas_call(
        _hgrn_kernel,
        grid=(bsz, nh, t // HGRN_BLOCK),
        in_specs=[
            col(0), col(1), col(2), col(3),
            pl.BlockSpec((1, B_DIM), lambda b, h, i: (0, h)),
            pl.BlockSpec((1, B_DIM), lambda b, h, i: (0, 0)),
        ],
        out_specs=[
            pl.BlockSpec((None, HGRN_BLOCK, B_DIM), lambda b, h, i: (b, i, h)),
            pl.BlockSpec((None, None, B_DIM, B_DIM), lambda b, h, i: (b, h, 0, 0)),
        ],
        out_shape=[
            jax.ShapeDtypeStruct((bsz, t, nh * B_DIM), F32),
            jax.ShapeDtypeStruct((bsz, nh, B_DIM, B_DIM), F32),
        ],
        scratch_shapes=[pltpu.VMEM((B_DIM, B_DIM), F32)],
        compiler_params=_params(("parallel", "parallel", "arbitrary")),
        name="hgrn_prompt",
    )(proj, proj, proj, proj, lower.reshape(1, nh * B_DIM), norm_g.reshape(1, B_DIM))


def _gdn_kernel(q_ref, k_ref, v_ref, gate_ref, ab_ref, cw_ref, alog_ref, dtb_ref, ng_ref,
                o_ref, s_out_ref, s_ref, xs_ref):
    i = pl.program_id(1)
    n = GDN_CHUNK

    @pl.when(i == 0)
    def _():
        s_ref[...] = jnp.zeros_like(s_ref)
        xs_ref[0:8, :] = jnp.zeros((8, C_CONV_DIM), F32)

    @pl.when(i > 0)
    def _():
        xs_ref[0:8, :] = xs_ref[n:n + 8, :]

    xs_ref[8:n + 8, 0:C_QK] = q_ref[...]
    xs_ref[8:n + 8, C_QK:2 * C_QK] = k_ref[...]
    xs_ref[8:n + 8, 2 * C_QK:3 * C_QK] = v_ref[...]
    off = 8 - (C_CONV - 1)
    acc = xs_ref[off:off + n, :] * cw_ref[0:1, :]
    for j in range(1, C_CONV):
        acc = acc + xs_ref[off + j:off + j + n, :] * cw_ref[j:j + 1, :]
    qkv = _silu(acc)

    ab = ab_ref[...]
    log_alpha = -jnp.exp(alog_ref[...]) * _softplus(ab + dtb_ref[...])
    beta_all = jax.nn.sigmoid(ab)
    r = lax.broadcasted_iota(jnp.int32, (n, n), 0)
    c = lax.broadcasted_iota(jnp.int32, (n, n), 1)
    causal = c <= r
    strict = c < r
    tri = jnp.where(causal, 1.0, 0.0).astype(F32)
    eye = jnp.where(r == c, 1.0, 0.0).astype(F32)
    base_mask = (r >> GDN_BASE_LOG) == (c >> GDN_BASE_LOG)
    level_masks = [((r >> (k + 1)) == (c >> (k + 1))) & ((r >> k) != (c >> k))
                   for k in range(GDN_BASE_LOG, GDN_CHUNK_LOG)]
    g_cum =_dot(tri, log_alpha, HIGHEST)
    g_cum_t = g_cum.T

    for h in range(C_HEADS):
        hs = slice(h * C_DIM, (h + 1) * C_DIM)
        qh = qkv[:, h * C_DIM:(h + 1) * C_DIM]
        kh = qkv[:, C_QK + h * C_DIM:C_QK + (h + 1) * C_DIM]
        vh = qkv[:, 2 * C_QK + h * C_DIM:2 * C_QK + (h + 1) * C_DIM]
        qn = qh * lax.rsqrt(jnp.sum(qh * qh, axis=-1, keepdims=True) + NORM_EPS) * (C_DIM ** -0.5)
        kn = kh * lax.rsqrt(jnp.sum(kh * kh, axis=-1, keepdims=True) + NORM_EPS)
        g_col = g_cum[:, h:h + 1]
        g_row = g_cum_t[h:h + 1, :]
        beta = beta_all[:, C_HEADS + h:C_HEADS + h + 1]
        dec_incl = jnp.exp(jnp.where(causal, g_col - g_row, -jnp.inf))
        dec_strict = jnp.where(strict, dec_incl, 0.0)
        kn_b = kn.astype(BF16)
        kkt = _dot_nt(kn_b, kn_b)
        l_mat = beta * kkt * dec_strict
        a_pow = jnp.where(base_mask, -l_mat, 0.0)
        t_inv = eye + a_pow
        for _ in range(GDN_BASE_LOG - 1):
            a_pow = _dot(a_pow, a_pow, HIGHEST)
            t_inv = t_inv + _dot(t_inv, a_pow, HIGHEST)
        for lm in level_masks:
            t_inv = t_inv - _dot(t_inv, _dot(jnp.where(lm, l_mat, 0.0), t_inv, HIGHEST), HIGHEST)
        e_g = jnp.exp(g_col)
        u = _dot(t_inv, beta * vh, HIGHEST)
        w = _dot(t_inv, (beta * e_g) * kn, HIGHEST)
        qk = _dot_nt(qn.astype(BF16), kn_b) * dec_incl
        s = s_ref[h]
        s_b = s.astype(BF16)
        delta = u - _dot(w.astype(BF16), s_b)
        delta_b = delta.astype(BF16)
        o = _dot((qn * e_g).astype(BF16), s_b) + _dot(qk.astype(BF16), delta_b)
        g_last = g_col[n - 1:n]
        k_dec = kn * jnp.exp(g_last - g_col)
        s_new = jnp.exp(g_last) * s + _dot(k_dec.T.astype(BF16), delta_b)
        s_ref[h] = s_new
        o_ref[:, hs] = _rms_gate(o, ng_ref[...], gate_ref[:, hs])

    @pl.when(i == pl.num_programs(1) - 1)
    def _():
        s_out_ref[...] = s_ref[...]


def _gdn_prompt(proj, conv_w, alog_row, dtb_row, norm_g):
    bsz, t, _ = proj.shape
    n = GDN_CHUNK

    def col(cidx):
        return pl.BlockSpec((None, n, C_QK), lambda b, i: (b, i, cidx))

    one = pl.BlockSpec((1, LANES), lambda b, i: (0, 0))
    return pl.pallas_call(
        _gdn_kernel,
        grid=(bsz, t // n),
        in_specs=[
            col(0), col(1), col(2), col(3),
            pl.BlockSpec((None, n, LANES), lambda b, i: (b, i, (C_CONV_DIM + C_QK) // LANES)),
            pl.BlockSpec((C_CONV, C_CONV_DIM), lambda b, i: (0, 0)),
            one, one, one,
        ],
        out_specs=[
            pl.BlockSpec((None, n, C_QK), lambda b, i: (b, i, 0)),
            pl.BlockSpec((None, C_HEADS, C_DIM, C_DIM), lambda b, i: (b, 0, 0, 0)),
        ],
        out_shape=[
            jax.ShapeDtypeStruct((bsz, t, C_QK), F32),
            jax.ShapeDtypeStruct((bsz, C_HEADS, C_DIM, C_DIM), F32),
        ],
        scratch_shapes=[pltpu.VMEM((C_HEADS, C_DIM, C_DIM), F32), pltpu.VMEM((n + 8, C_CONV_DIM), F32)],
        compiler_params=_params(("parallel", "arbitrary")),
        name="gdn_prompt",
    )(proj, proj, proj, proj, proj, conv_w, alog_row, dtb_row, norm_g.reshape(1, C_DIM))


def _swa_dec_kernel(sink_ref, q_ref, kn_ref, vn_ref, ck_ref, cv_ref, o_ref, ok_ref, ov_ref):
    nb = q_ref.shape[0]
    row = lax.broadcasted_iota(jnp.int32, (WINDOW, LANES), 0)
    grp = lax.broadcasted_iota(jnp.int32, (A_GROUP, WINDOW), 0)
    dist = (WINDOW - 1 - lax.broadcasted_iota(jnp.int32, (A_GROUP, WINDOW), 1)).astype(F32)
    scale = A_HEAD_DIM ** -0.5
    for bb in range(nb):
        keys = jnp.where(row == WINDOW - 1, kn_ref[bb], pltpu.roll(ck_ref[bb], WINDOW - 1, 0))
        vals = jnp.where(row == WINDOW - 1, vn_ref[bb], pltpu.roll(cv_ref[bb], WINDOW - 1, 0))
        ok_ref[bb] = keys
        ov_ref[bb] = vals
        for kv in range(A_KV_HEADS):
            ks = slice(kv * A_HEAD_DIM, (kv + 1) * A_HEAD_DIM)
            h0 = kv * A_GROUP
            slope = jnp.zeros((A_GROUP, WINDOW), F32)
            sink = jnp.zeros((A_GROUP, 1), F32)
            for g in range(A_GROUP):
                slope = jnp.where(grp == g, 2.0 ** (-8.0 * (h0 + g + 1) / A_HEADS), slope)
                sink = jnp.where(grp[:, 0:1] == g, sink_ref[h0 + g], sink)
            q4 = q_ref[bb, h0:h0 + A_GROUP, :].astype(BF16)
            s = _dot_nt(q4, keys[:, ks].astype(BF16)) * scale - slope * dist
            m = jnp.maximum(jnp.max(s, axis=-1, keepdims=True), sink)
            p = jnp.exp(s - m)
            den = jnp.sum(p, axis=-1, keepdims=True) + jnp.exp(sink - m)
            o = _dot(p.astype(BF16), vals[:, ks].astype(BF16))
            o_ref[bb, h0:h0 + A_GROUP, :] = o / den


def _swa_decode(proj2, cache_k, cache_v, sinks):
    nbatch = proj2.shape[0]
    bb = 8
    q3 = proj2[:, :A_Q].reshape(nbatch, A_HEADS, A_HEAD_DIM)
    kn = proj2[:, A_Q:A_Q + A_KV].reshape(nbatch, 1, A_KV)
    vn = proj2[:, A_Q + A_KV:A_Q + 2 * A_KV].reshape(nbatch, 1, A_KV)
    ck = cache_k.reshape(nbatch, WINDOW, A_KV)
    cv = cache_v.reshape(nbatch, WINDOW, A_KV)
    qspec = pl.BlockSpec((bb, A_HEADS, A_HEAD_DIM), lambda j: (j, 0, 0))
    nspec = pl.BlockSpec((bb, 1, A_KV), lambda j: (j, 0, 0))
    cspec = pl.BlockSpec((bb, WINDOW, A_KV), lambda j: (j, 0, 0))
    o, nk, nv = pl.pallas_call(
        _swa_dec_kernel,
        grid=(nbatch // bb,),
        in_specs=[pl.BlockSpec(memory_space=pltpu.SMEM), qspec, nspec, nspec, cspec, cspec],
        out_specs=[qspec, cspec, cspec],
        out_shape=[
            jax.ShapeDtypeStruct((nbatch, A_HEADS, A_HEAD_DIM), F32),
            jax.ShapeDtypeStruct((nbatch, WINDOW, A_KV), F32),
            jax.ShapeDtypeStruct((nbatch, WINDOW, A_KV), F32),
        ],
        compiler_params=_params(("parallel",)),
        name="swa_decode",
    )(sinks, q3, kn, vn, ck, cv)
    shape5 = (nbatch, WINDOW, A_KV_HEADS, A_HEAD_DIM)
    return o.reshape(nbatch, A_Q), nk.reshape(shape5), nv.reshape(shape5)


def _cols(rows, heads):
    nbatch = rows.shape[0]
    x = rows.reshape(nbatch // DEC_BLOCK, DEC_BLOCK, heads, LANES)
    return x.transpose(2, 0, 3, 1)


def _hgrn_dec_kernel(qt_ref, ft_ref, lb_ref, v_ref, g_ref, ng_ref, s_ref, o_ref, so_ref, orow_ref):
    lb = lb_ref[...]
    qc = _silu(qt_ref[...])
    f = lb + (1.0 - lb) * jax.nn.sigmoid(ft_ref[...])
    kk = 1.0 - f
    v = v_ref[...]
    for bb in range(DEC_BLOCK):
        s_new = f[:, bb:bb + 1] * s_ref[bb] + kk[:, bb:bb + 1] * v[bb:bb + 1, :]
        so_ref[bb] = s_new
        orow_ref[bb:bb + 1, :] = jnp.sum(qc[:, bb:bb + 1] * s_new, axis=0, keepdims=True)
    o_ref[...] = _rms_gate(orow_ref[...], ng_ref[...], g_ref[...])


def _hgrn_decode(proj2, state, lower, norm_g):
    nbatch = proj2.shape[0]
    nh = B_HEADS
    base = A_Q + 2 * A_KV
    width = nh * B_DIM
    qt = _cols(proj2[:, base:base + width], nh)
    ft = _cols(proj2[:, base + width:base + 2 * width], nh)
    colspec = pl.BlockSpec((None, None, B_DIM, DEC_BLOCK), lambda h, j: (h, j, 0, 0))
    sspec = pl.BlockSpec((DEC_BLOCK, None, B_DIM, B_DIM), lambda h, j: (j, h, 0, 0))
    cb = base // LANES
    o, s_new = pl.pallas_call(
        _hgrn_dec_kernel,
        grid=(nh, nbatch // DEC_BLOCK),
        in_specs=[
            colspec, colspec,
            pl.BlockSpec((None, B_DIM, 1), lambda h, j: (h, 0, 0)),
            pl.BlockSpec((DEC_BLOCK, B_DIM), lambda h, j: (j, cb + 2 * nh + h)),
            pl.BlockSpec((DEC_BLOCK, B_DIM), lambda h, j: (j, cb + 3 * nh + h)),
            pl.BlockSpec((1, B_DIM), lambda h, j: (0, 0)),
            sspec,
        ],
        out_specs=[pl.BlockSpec((DEC_BLOCK, B_DIM), lambda h, j: (j, h)), sspec],
        out_shape=[
            jax.ShapeDtypeStruct((nbatch, width), F32),
            jax.ShapeDtypeStruct(state.shape, F32),
        ],
        scratch_shapes=[pltpu.VMEM((DEC_BLOCK, B_DIM), F32)],
        compiler_params=_params(("parallel", "parallel")),
        name="hgrn_decode",
    )(qt, ft, lower.reshape(nh, B_DIM, 1), proj2, proj2, norm_g.reshape(1, B_DIM), state)
    return o, s_new


def _gdn_prep_kernel(p_ref, h0_ref, h1_ref, h2_ref, cw_ref, alog_ref, dtb_ref, q_ref, k_ref, v_ref, ab_ref):
    acc = (h0_ref[...] * cw_ref[0:1, :] + h1_ref[...] * cw_ref[1:2, :] + h2_ref[...] * cw_ref[2:3, :]
           + p_ref[:, 0:C_CONV_DIM] * cw_ref[3:4, :])
    qkv = _silu(acc)
    for h in range(C_HEADS):
        hs = slice(h * C_DIM, (h + 1) * C_DIM)
        qh = qkv[:, h * C_DIM:(h + 1) * C_DIM]
        kh = qkv[:, C_QK + h * C_DIM:C_QK + (h + 1) * C_DIM]
        q_ref[:, hs] = qh * lax.rsqrt(jnp.sum(qh * qh, axis=-1, keepdims=True) + NORM_EPS) * (C_DIM ** -0.5)
        k_ref[:, hs] = kh * lax.rsqrt(jnp.sum(kh * kh, axis=-1, keepdims=True) + NORM_EPS)
    v_ref[...] = qkv[:, 2 * C_QK:3 * C_QK]
    ab = p_ref[:, C_CONV_DIM + C_QK:C_CONV_DIM + C_QK + LANES]
    alpha = jnp.exp(-jnp.exp(alog_ref[...]) * _softplus(ab + dtb_ref[...]))
    beta = jax.nn.sigmoid(ab)
    lane = lax.broadcasted_iota(jnp.int32, ab.shape, 1)
    ab_ref[...] = jnp.where(lane < C_HEADS, alpha, beta)


def _gdn_dec_kernel(qt_ref, kt_ref, v_ref, gate_ref, al_ref, be_ref, ng_ref, s_ref, o_ref, so_ref, orow_ref):
    qt = qt_ref[...]
    kt = kt_ref[...]
    v = v_ref[...]
    al = al_ref[...]
    be = be_ref[...]
    for bb in range(DEC_BLOCK):
        s = s_ref[bb]
        kc = kt[:, bb:bb + 1]
        a = al[bb:bb + 1, :]
        ks = jnp.sum(kc * s, axis=0, keepdims=True)
        delta = be[bb:bb + 1, :] * (v[bb:bb + 1, :] - a * ks)
        s_new = a * s + kc * delta
        so_ref[bb] = s_new
        orow_ref[bb:bb + 1, :] = jnp.sum(qt[:, bb:bb + 1] * s_new, axis=0, keepdims=True)
    o_ref[...] = _rms_gate(orow_ref[...], ng_ref[...], gate_ref[...])


def _gdn_decode(proj2, hist, state, conv_w, alog_row, dtb_row, norm_g):
    nbatch = proj2.shape[0]
    nh = C_HEADS
    full = lambda shape: pl.BlockSpec(shape, lambda: tuple(0 for _ in shape))
    hspec = full((nbatch, C_CONV_DIM))
    qn, kn, vc, ab = pl.pallas_call(
        _gdn_prep_kernel,
        grid=(),
        in_specs=[
            full((nbatch, ODD_IN_PAD)), hspec, hspec, hspec,
            full((C_CONV, C_CONV_DIM)), full((1, LANES)), full((1, LANES)),
        ],
        out_specs=[full((nbatch, C_QK)), full((nbatch, C_QK)), full((nbatch, C_QK)), full((nbatch, LANES))],
        out_shape=[
            jax.ShapeDtypeStruct((nbatch, C_QK), F32),
            jax.ShapeDtypeStruct((nbatch, C_QK), F32),
            jax.ShapeDtypeStruct((nbatch, C_QK), F32),
            jax.ShapeDtypeStruct((nbatch, LANES), F32),
        ],
        compiler_params=pltpu.CompilerParams(vmem_limit_bytes=VMEM_LIMIT),
        name="gdn_decode_prep",
    )(proj2, hist[:, 0], hist[:, 1], hist[:, 2], conv_w, alog_row, dtb_row)
    qt = _cols(qn, nh)
    kt = _cols(kn, nh)
    al = jnp.broadcast_to(ab[:, :nh].T[:, :, None], (nh, nbatch, LANES))
    be = jnp.broadcast_to(ab[:, nh:2 * nh].T[:, :, None], (nh, nbatch, LANES))
    colspec = pl.BlockSpec((None, None, C_DIM, DEC_BLOCK), lambda h, j: (h, j, 0, 0))
    sspec = pl.BlockSpec((DEC_BLOCK, None, C_DIM, C_DIM), lambda h, j: (j, h, 0, 0))
    rspec = pl.BlockSpec((None, DEC_BLOCK, LANES), lambda h, j: (h, j, 0))
    gcol = C_CONV_DIM // LANES
    o, s_new = pl.pallas_call(
        _gdn_dec_kernel,
        grid=(nh, nbatch // DEC_BLOCK),
        in_specs=[
            colspec, colspec,
            pl.BlockSpec((DEC_BLOCK, C_DIM), lambda h, j: (j, h)),
            pl.BlockSpec((DEC_BLOCK, C_DIM), lambda h, j: (j, gcol + h)),
            rspec, rspec,
            pl.BlockSpec((1, C_DIM), lambda h, j: (0, 0)),
            sspec,
        ],
        out_specs=[pl.BlockSpec((DEC_BLOCK, C_DIM), lambda h, j: (j, h)), sspec],
        out_shape=[
            jax.ShapeDtypeStruct((nbatch, C_QK), F32),
            jax.ShapeDtypeStruct(state.shape, F32),
        ],
        scratch_shapes=[pltpu.VMEM((DEC_BLOCK, C_DIM), F32)],
        compiler_params=_params(("parallel", "parallel")),
        name="gdn_decode",
    )(qt, kt, vc, proj2, al, be, norm_g.reshape(1, C_DIM), state)
    return o, s_new


def _trunk(x, mods, caches, p, tm):
    decode = caches is not None
    m = mods[0]
    x = _ffn(x, m, 0, p["w_up"][0, 0], p["w_down"][0, 0], p["ln_g"][0, 0], p["ln_b"][0, 0], tm)
    proj = _inproj(x, m, 3, p["even_in"], tm)
    if decode:
        o_a, new_k, new_v = _swa_decode(proj[0], caches[0][0], caches[1][0], p["sinks"])
        o_b, s_hgrn = _hgrn_decode(proj[0], caches[2][0], p["lower"], p["hgrn_norm_g"])
        o_a, o_b = o_a[None], o_b[None]
    else:
        bsz = x.shape[0]
        o_a = _swa_prompt(proj, p["sinks"])
        new_k = proj[:, -WINDOW:, A_Q:A_Q + A_KV].reshape(bsz, WINDOW, A_KV_HEADS, A_HEAD_DIM)
        new_v = proj[:, -WINDOW:, A_Q + A_KV:A_Q + 2 * A_KV].reshape(bsz, WINDOW, A_KV_HEADS, A_HEAD_DIM)
        o_b, s_hgrn = _hgrn_prompt(proj, p["lower"], p["hgrn_norm_g"])
    x = _outproj(x, o_a, 0, o_b, 0, m, 5, p["even_out"], p["ln_g"][0, 1], p["ln_b"][0, 1], tm)
    x = _ffn(x, m, 6, p["w_up"][0, 1], p["w_down"][0, 1], p["ln_g"][0, 2], p["ln_b"][0, 2], tm)
    m = mods[1]
    x = _ffn(x, m, 0, p["w_up"][1, 0], p["w_down"][1, 0], p["ln_g"][1, 0], p["ln_b"][1, 0], tm)
    proj = _inproj(x, m, 3, p["odd_in"], tm)
    if decode:
        hist = caches[4][0]
        o_c, s_gdn = _gdn_decode(proj[0], hist, caches[3][0], p["conv_w"], p["alog_row"], p["dtb_row"],
                                 p["gdn_norm_g"])
        o_c = o_c[None]
        new_hist = jnp.concatenate([hist[:, 1:], proj[0][:, None, :C_CONV_DIM]], axis=1)
    else:
        o_c, s_gdn = _gdn_prompt(proj, p["conv_w"], p["alog_row"], p["dtb_row"], p["gdn_norm_g"])
        new_hist = proj[:, -(C_CONV - 1):, :C_CONV_DIM]
    x = _outproj(x, o_c, 0, o_c, 1, m, 5, p["odd_out"], p["ln_g"][1, 1], p["ln_b"][1, 1], tm)
    x = _ffn(x, m, 6, p["w_up"][1, 1], p["w_down"][1, 1], p["ln_g"][1, 2], p["ln_b"][1, 2], tm)
    return x, new_k[None], new_v[None], s_hgrn[None], s_gdn[None], new_hist[None]


def kernel(x_prompt, x_sample, cache_swa_k, cache_swa_v, state_hgrn, state_gdn, state_gdn_conv, c_prompt, c_sample, ada_w, ada_b, ln_g, ln_b, ffn_w_up, ffn_w_down, even_w_in, even_w_out, swa_sinks, hgrn_norm_g, hgrn_lb_logits, odd_w_in, odd_w_out, gdn_conv_w, gdn_a_log, gdn_dt_bias, gdn_norm_g):
    n_prompt = c_prompt.shape[0]
    n_sample = c_sample.shape[0]
    pad_rows = (-(n_prompt + n_sample)) % 8
    c_all = jnp.concatenate([c_prompt, c_sample, jnp.zeros((pad_rows, D_MODEL), F32)], axis=0)
    mods = _ada_mods(c_all, ada_w, ada_b)
    mods_p = mods[:, :n_prompt].reshape(DEPTH, n_prompt, 1, N_MOD * D_MODEL)
    mods_s = mods[:, n_prompt:n_prompt + n_sample].reshape(DEPTH, 1, n_sample, N_MOD * D_MODEL)

    probs = jax.nn.softmax(hgrn_lb_logits.astype(F32), axis=0)
    lower = (jnp.cumsum(probs, axis=0)[1:] - probs[0])[0]

    def lane_row(v):
        return jnp.pad(v.astype(F32), (0, LANES - v.shape[0])).reshape(1, LANES)

    p = dict(
        w_up=ffn_w_up.astype(BF16), w_down=ffn_w_down.astype(BF16),
        ln_g=ln_g, ln_b=ln_b,
        even_in=even_w_in[0].astype(BF16), even_out=even_w_out[0].astype(BF16),
        odd_in=jnp.pad(odd_w_in[0], ((0, 0), (0, ODD_IN_PAD - ODD_IN))).astype(BF16),
        odd_out=odd_w_out[0].astype(BF16),
        sinks=swa_sinks[0], lower=lower, hgrn_norm_g=hgrn_norm_g[0],
        conv_w=gdn_conv_w[0], alog_row=lane_row(gdn_a_log[0]), dtb_row=lane_row(gdn_dt_bias[0]),
        gdn_norm_g=gdn_norm_g[0],
    )
    y_p, p_k, p_v, p_hgrn, p_gdn, p_conv = _trunk(x_prompt, mods_p, None, p, 512)
    caches = (cache_swa_k, cache_swa_v, state_hgrn, state_gdn, state_gdn_conv)
    x_s = x_sample.reshape(1, n_sample, D_MODEL)
    y_s, s_k, s_v, s_hgrn, s_gdn, s_conv = _trunk(x_s, mods_s, caches, p, n_sample)
    y_s = y_s.reshape(n_sample, 1, D_MODEL)
    return (y_p, y_s, p_k, p_v, p_hgrn, p_gdn, p_conv, s_k, s_v, s_hgrn, s_gdn, s_conv)
```

```python
import functools

import jax
import jax.numpy as jnp
from jax import lax
from jax.experimental import pallas as pl
from jax.experimental.pallas import tpu as pltpu

F32 = jnp.float32
BF16 = jnp.bfloat16
HIGHEST = lax.Precision.HIGHEST

D_MODEL = 1024
DEPTH = 2
WINDOW = 128
A_HEADS = 8
A_KV_HEADS = 2
A_GROUP = A_HEADS // A_KV_HEADS
A_HEAD_DIM = 64
A_Q = A_HEADS * A_HEAD_DIM
A_KV = A_KV_HEADS * A_HEAD_DIM
B_HEADS = 4
B_DIM = 128
EVEN_IN = 2816
C_HEADS = 8
C_DIM = 128
C_QK = C_HEADS * C_DIM
C_CONV = 4
C_CONV_DIM = 3 * C_QK
ODD_IN = C_CONV_DIM + C_QK + 2 * C_HEADS
ODD_IN_PAD = 4224
D_FF = 2816
N_MOD = 9
DN_ALPHA = (2 * DEPTH) ** 0.25
LN_EPS = 1e-5
NORM_EPS = 1e-6
LANES = 128

FF_TILE = 1408
PROJ_TILE = 1408
HGRN_BLOCK = 128
HGRN_SUB = 16
GDN_CHUNK_LOG = 7
GDN_CHUNK = 1 << GDN_CHUNK_LOG
GDN_BASE_LOG = 3
DEC_BLOCK = 64
VMEM_LIMIT = 48 << 20


def _params(sem, vmem=VMEM_LIMIT):
    return pltpu.CompilerParams(dimension_semantics=sem, vmem_limit_bytes=vmem)


def _silu(x):
    return x * jax.nn.sigmoid(x)


def _softplus(x):
    return jnp.maximum(x, 0.0) + jnp.log(1.0 + jnp.exp(-jnp.abs(x)))


def _layer_norm(y, g, b):
    mu = jnp.mean(y, axis=-1, keepdims=True)
    d = y - mu
    var = jnp.mean(d * d, axis=-1, keepdims=True)
    return d * lax.rsqrt(var + LN_EPS) * g + b


def _rms_gate(o, norm_g, gate):
    y = o * lax.rsqrt(jnp.mean(o * o, axis=-1, keepdims=True) + NORM_EPS)
    return y * norm_g * _silu(gate)


def _dot(a, b, precision=None):
    return jnp.dot(a, b, preferred_element_type=F32, precision=precision)


def _dot_inv(a, b):
    return _dot(a.astype(BF16), b.astype(BF16))


def _dot_nt(a, b, precision=None):
    return lax.dot_general(a, b, (((1,), (1,)), ((), ())), preferred_element_type=F32, precision=precision)


def _ada_kernel(c_ref, w_ref, b_ref, o_ref):
    cs = _silu(c_ref[...]).astype(BF16)
    o_ref[...] = _dot(cs, w_ref[...].astype(BF16)) + b_ref[...]


def _ada_mods(c_all, ada_w, ada_b):
    m = c_all.shape[0]
    n = N_MOD * D_MODEL
    tn = 1152
    return pl.pallas_call(
        _ada_kernel,
        grid=(DEPTH, n // tn),
        in_specs=[
            pl.BlockSpec((m, D_MODEL), lambda l, j: (0, 0)),
            pl.BlockSpec((None, D_MODEL, tn), lambda l, j: (l, 0, j)),
            pl.BlockSpec((None, 1, tn), lambda l, j: (l, 0, j)),
        ],
        out_specs=pl.BlockSpec((None, m, tn), lambda l, j: (l, 0, j)),
        out_shape=jax.ShapeDtypeStruct((DEPTH, m, n), F32),
        compiler_params=_params(("parallel", "parallel")),
        name="ada_mods",
    )(c_all, ada_w, ada_b.reshape(DEPTH, 1, n))


def _mod_spec(mods, k, tm, grid_rank):
    per_token = mods.shape[1] != 1
    rows = tm if per_token else 1
    if grid_rank == 3:
        return pl.BlockSpec((None, rows, D_MODEL), lambda b, i, j: (b, i if per_token else 0, k))
    return pl.BlockSpec((None, rows, D_MODEL), lambda b, i: (b, i if per_token else 0, k))


def _ffn_kernel(x_ref, sh_ref, sc_ref, g_ref, wg_ref, wu_ref, wd_ref, lg_ref, lb_ref, o_ref, h_ref, acc_ref):
    j = pl.program_id(2)

    @pl.when(j == 0)
    def _():
        h_ref[...] = (x_ref[...] * (1.0 + sc_ref[...]) + sh_ref[...]).astype(BF16)
        acc_ref[...] = jnp.zeros_like(acc_ref)

    h = h_ref[...]
    gate = _dot(h, wg_ref[...])
    up = _dot(h, wu_ref[...])
    act = (_silu(gate) * up).astype(BF16)
    acc_ref[...] += _dot(act, wd_ref[...])

    @pl.when(j == pl.num_programs(2) - 1)
    def _():
        y = DN_ALPHA * x_ref[...] + (0.5 * g_ref[...]) * acc_ref[...]
        o_ref[...] = _layer_norm(y, lg_ref[...], lb_ref[...])


def _ffn(x, mods, k0, w_up, w_down, ln_g, ln_b, tm):
    bsz, t, _ = x.shape
    nf = D_FF // FF_TILE
    row = pl.BlockSpec((None, tm, D_MODEL), lambda b, i, j: (b, i, 0))
    vec = pl.BlockSpec((1, D_MODEL), lambda b, i, j: (0, 0))
    return pl.pallas_call(
        _ffn_kernel,
        grid=(bsz, t // tm, nf),
        in_specs=[
            row,
            _mod_spec(mods, k0, tm, 3), _mod_spec(mods, k0 + 1, tm, 3), _mod_spec(mods, k0 + 2, tm, 3),
            pl.BlockSpec((D_MODEL, FF_TILE), lambda b, i, j: (0, j)),
            pl.BlockSpec((D_MODEL, FF_TILE), lambda b, i, j: (0, j + nf)),
            pl.BlockSpec((FF_TILE, D_MODEL), lambda b, i, j: (j, 0)),
            vec, vec,
        ],
        out_specs=row,
        out_shape=jax.ShapeDtypeStruct(x.shape, F32),
        scratch_shapes=[pltpu.VMEM((tm, D_MODEL), BF16), pltpu.VMEM((tm, D_MODEL), F32)],
        compiler_params=_params(("parallel", "parallel", "arbitrary")),
        name="ffn",
    )(x, mods, mods, mods, w_up, w_up, w_down, ln_g.reshape(1, D_MODEL), ln_b.reshape(1, D_MODEL))


def _inproj_kernel(x_ref, sh_ref, sc_ref, w_ref, o_ref, h_ref):
    @pl.when(pl.program_id(2) == 0)
    def _():
        h_ref[...] = (x_ref[...] * (1.0 + sc_ref[...]) + sh_ref[...]).astype(BF16)

    o_ref[...] = _dot(h_ref[...], w_ref[...])


def _inproj(x, mods, k0, w, tm):
    bsz, t, _ = x.shape
    n = w.shape[1]
    return pl.pallas_call(
        _inproj_kernel,
        grid=(bsz, t // tm, n // PROJ_TILE),
        in_specs=[
            pl.BlockSpec((None, tm, D_MODEL), lambda b, i, j: (b, i, 0)),
            _mod_spec(mods, k0, tm, 3), _mod_spec(mods, k0 + 1, tm, 3),
            pl.BlockSpec((D_MODEL, PROJ_TILE), lambda b, i, j: (0, j)),
        ],
        out_specs=pl.BlockSpec((None, tm, PROJ_TILE), lambda b, i, j: (b, i, j)),
        out_shape=jax.ShapeDtypeStruct((bsz, t, n), F32),
        scratch_shapes=[pltpu.VMEM((tm, D_MODEL), BF16)],
        compiler_params=_params(("parallel", "parallel", "arbitrary")),
        name="inproj",
    )(x, mods, mods, w)


def _outproj_kernel(x_ref, o1_ref, o2_ref, g_ref, w1_ref, w2_ref, lg_ref, lb_ref, y_ref):
    mix = _dot(o1_ref[...].astype(BF16), w1_ref[...]) + _dot(o2_ref[...].astype(BF16), w2_ref[...])
    y = DN_ALPHA * x_ref[...] + g_ref[...] * mix
    y_ref[...] = _layer_norm(y, lg_ref[...], lb_ref[...])


def _outproj(x, o1, c1, o2, c2, mods, kg, w_out, ln_g, ln_b, tm):
    bsz, t, _ = x.shape
    half = D_MODEL // 2
    row = pl.BlockSpec((None, tm, D_MODEL), lambda b, i: (b, i, 0))
    vec = pl.BlockSpec((1, D_MODEL), lambda b, i: (0, 0))
    return pl.pallas_call(
        _outproj_kernel,
        grid=(bsz, t // tm),
        in_specs=[
            row,
            pl.BlockSpec((None, tm, half), lambda b, i: (b, i, c1)),
            pl.BlockSpec((None, tm, half), lambda b, i: (b, i, c2)),
            _mod_spec(mods, kg, tm, 2),
            pl.BlockSpec((half, D_MODEL), lambda b, i: (0, 0)),
            pl.BlockSpec((half, D_MODEL), lambda b, i: (1, 0)),
            vec, vec,
        ],
        out_specs=row,
        out_shape=jax.ShapeDtypeStruct(x.shape, F32),
        compiler_params=_params(("parallel", "parallel")),
        name="outproj",
    )(x, o1, o2, mods, w_out, w_out, ln_g.reshape(1, D_MODEL), ln_b.reshape(1, D_MODEL))


def _swa_kernel(sink_ref, q_ref, kc_ref, kp_ref, vc_ref, vp_ref, o_ref):
    i = pl.program_id(1)
    q = q_ref[...]
    kc = kc_ref[...].astype(BF16)
    kp = kp_ref[...].astype(BF16)
    vc = vc_ref[...].astype(BF16)
    vp = vp_ref[...].astype(BF16)
    r = lax.broadcasted_iota(jnp.int32, (WINDOW, WINDOW), 0)
    c = lax.broadcasted_iota(jnp.int32, (WINDOW, WINDOW), 1)
    dist_c = (r - c).astype(F32)
    dist_p = dist_c + float(WINDOW)
    valid_c = c <= r
    valid_p = c > r + jnp.where(i > 0, 0, WINDOW)
    scale = A_HEAD_DIM ** -0.5
    for h in range(A_HEADS):
        kv = h // A_GROUP
        hs = slice(h * A_HEAD_DIM, (h + 1) * A_HEAD_DIM)
        ks = slice(kv * A_HEAD_DIM, (kv + 1) * A_HEAD_DIM)
        slope = 2.0 ** (-8.0 * (h + 1) / A_HEADS)
        qh = q[:, hs].astype(BF16)
        s_c = _dot_nt(qh, kc[:, ks]) * scale
        s_p = _dot_nt(qh, kp[:, ks]) * scale
        s_c = jnp.where(valid_c, s_c - slope * dist_c, -jnp.inf)
        s_p = jnp.where(valid_p, s_p - slope * dist_p, -jnp.inf)
        sink = sink_ref[h]
        m = jnp.maximum(jnp.max(s_c, axis=-1, keepdims=True), jnp.max(s_p, axis=-1, keepdims=True))
        m = jnp.maximum(m, sink)
        p_c = jnp.exp(s_c - m)
        p_p = jnp.exp(s_p - m)
        den = jnp.sum(p_c, axis=-1, keepdims=True) + jnp.sum(p_p, axis=-1, keepdims=True) + jnp.exp(sink - m)
        o = _dot(p_c.astype(BF16), vc[:, ks]) + _dot(p_p.astype(BF16), vp[:, ks])
        o_ref[:, hs] = o / den


def _swa_prompt(proj, sinks):
    bsz, t, _ = proj.shape
    nb = t // WINDOW
    kcol = A_Q // LANES
    vcol = kcol + 1

    def cur(col):
        return pl.BlockSpec((None, WINDOW, LANES), lambda b, i: (b, i, col))

    def prev(col):
        return pl.BlockSpec((None, WINDOW, LANES), lambda b, i: (b, jnp.maximum(i - 1, 0), col))

    return pl.pallas_call(
        _swa_kernel,
        grid=(bsz, nb),
        in_specs=[
            pl.BlockSpec(memory_space=pltpu.SMEM),
            pl.BlockSpec((None, WINDOW, A_Q), lambda b, i: (b, i, 0)),
            cur(kcol), prev(kcol), cur(vcol), prev(vcol),
        ],
        out_specs=pl.BlockSpec((None, WINDOW, A_Q), lambda b, i: (b, i, 0)),
        out_shape=jax.ShapeDtypeStruct((bsz, t, A_Q), F32),
        compiler_params=_params(("parallel", "parallel")),
        name="swa_prompt",
    )(sinks, proj, proj, proj, proj, proj)


def _hgrn_kernel(q_ref, f_ref, v_ref, g_ref, lb_ref, ng_ref, o_ref, s_ref, st_ref):
    i = pl.program_id(2)

    @pl.when(i == 0)
    def _():
        st_ref[...] = jnp.zeros_like(st_ref)

    lb = lb_ref[...]
    qb = _silu(q_ref[...])
    f = lb + (1.0 - lb) * jax.nn.sigmoid(f_ref[...])
    lf = jnp.log(f)
    kk = 1.0 - f
    v = v_ref[...]
    n = HGRN_BLOCK
    r = lax.broadcasted_iota(jnp.int32, (n, n), 0)
    c = lax.broadcasted_iota(jnp.int32, (n, n), 1)
    tri = jnp.where((c <= r) & (c >= r - (r & (HGRN_SUB - 1))), 1.0, 0.0).astype(F32)
    a_all = _dot(tri, lf, HIGHEST)
    v_t = v.T
    s_t = st_ref[...]
    trow = lax.broadcasted_iota(jnp.int32, (HGRN_SUB, 1), 0)
    outs = []
    for ch in range(n // HGRN_SUB):
        sl = slice(ch * HGRN_SUB, (ch + 1) * HGRN_SUB)
        a_c = a_all[sl]
        q_c = qb[sl]
        k_c = kk[sl]
        v_c = v[sl]
        a_last = a_c[HGRN_SUB - 1:HGRN_SUB]
        o_c = _dot_nt((q_c * jnp.exp(a_c)).astype(BF16), s_t.astype(BF16))
        for s in range(HGRN_SUB):
            e = jnp.exp(jnp.where(trow >= s, a_c - a_c[s:s + 1], -jnp.inf))
            col = jnp.sum(q_c * e * k_c[s:s + 1], axis=-1, keepdims=True)
            o_c = o_c + col * v_c[s:s + 1]
        k_dec = k_c * jnp.exp(a_last - a_c)
        s_t = s_t * jnp.exp(a_last) + _dot(v_t[:, sl].astype(BF16), k_dec.astype(BF16))
        outs.append(o_c)
    st_ref[...] = s_t
    o = jnp.concatenate(outs, axis=0)
    o_ref[...] = _rms_gate(o, ng_ref[...], g_ref[...])

    @pl.when(i == pl.num_programs(2) - 1)
    def _():
        s_ref[...] = s_t.T


def _hgrn_prompt(proj, lower, norm_g):
    bsz, t, _ = proj.shape
    base = (A_Q + 2 * A_KV) // LANES
    nh = B_HEADS

    def col(off):
        return pl.BlockSpec((None, HGRN_BLOCK, B_DIM), lambda b, h, i: (b, i, base + off * nh + h))

    return pl.pallas_call(
        _hgrn_kernel,
        grid=(bsz, nh, t // HGRN_BLOCK),
        in_specs=[
            col(0), col(1), col(2), col(3),
            pl.BlockSpec((1, B_DIM), lambda b, h, i: (0, h)),
            pl.BlockSpec((1, B_DIM), lambda b, h, i: (0, 0)),
        ],
        out_specs=[
            pl.BlockSpec((None, HGRN_BLOCK, B_DIM), lambda b, h, i: (b, i, h)),
            pl.BlockSpec((None, None, B_DIM, B_DIM), lambda b, h, i: (b, h, 0, 0)),
        ],
        out_shape=[
            jax.ShapeDtypeStruct((bsz, t, nh * B_DIM), F32),
            jax.ShapeDtypeStruct((bsz, nh, B_DIM, B_DIM), F32),
        ],
        scratch_shapes=[pltpu.VMEM((B_DIM, B_DIM), F32)],
        compiler_params=_params(("parallel", "parallel", "arbitrary")),
        name="hgrn_prompt",
    )(proj, proj, proj, proj, lower.reshape(1, nh * B_DIM), norm_g.reshape(1, B_DIM))


def _gdn_kernel(q_ref, k_ref, v_ref, gate_ref, ab_ref, cw_ref, alog_ref, dtb_ref, ng_ref,
                o_ref, s_out_ref, s_ref, xs_ref):
    i = pl.program_id(1)
    n = GDN_CHUNK

    @pl.when(i == 0)
    def _():
        s_ref[...] = jnp.zeros_like(s_ref)
        xs_ref[0:8, :] = jnp.zeros((8, C_CONV_DIM), F32)

    @pl.when(i > 0)
    def _():
        xs_ref[0:8, :] = xs_ref[n:n + 8, :]

    xs_ref[8:n + 8, 0:C_QK] = q_ref[...]
    xs_ref[8:n + 8, C_QK:2 * C_QK] = k_ref[...]
    xs_ref[8:n + 8, 2 * C_QK:3 * C_QK] = v_ref[...]
    off = 8 - (C_CONV - 1)
    acc = xs_ref[off:off + n, :] * cw_ref[0:1, :]
    for j in range(1, C_CONV):
        acc = acc + xs_ref[off + j:off + j + n, :] * cw_ref[j:j + 1, :]
    qkv = _silu(acc)

    ab = ab_ref[...]
    log_alpha = -jnp.exp(alog_ref[...]) * _softplus(ab + dtb_ref[...])
    beta_all = jax.nn.sigmoid(ab)
    r = lax.broadcasted_iota(jnp.int32, (n, n), 0)
    c = lax.broadcasted_iota(jnp.int32, (n, n), 1)
    causal = c <= r
    strict = c < r
    tri = jnp.where(causal, 1.0, 0.0).astype(F32)
    eye = jnp.where(r == c, 1.0, 0.0).astype(F32)
    base_mask = (r >> GDN_BASE_LOG) == (c >> GDN_BASE_LOG)
    level_masks = [((r >> (k + 1)) == (c >> (k + 1))) & ((r >> k) != (c >> k))
                   for k in range(GDN_BASE_LOG, GDN_CHUNK_LOG)]
    g_cum =_dot(tri, log_alpha, HIGHEST)
    g_cum_t = g_cum.T

    heads = range(C_HEADS)
    qn, kn, kn_b, vh, g_col, beta, dec_incl, e_g = [], [], [], [], [], [], [], []
    for h in heads:
        q_h = qkv[:, h * C_DIM:(h + 1) * C_DIM]
        k_h = qkv[:, C_QK + h * C_DIM:C_QK + (h + 1) * C_DIM]
        vh.append(qkv[:, 2 * C_QK + h * C_DIM:2 * C_QK + (h + 1) * C_DIM])
        qn.append(q_h * lax.rsqrt(jnp.sum(q_h * q_h, axis=-1, keepdims=True) + NORM_EPS) * (C_DIM ** -0.5))
        kn.append(k_h * lax.rsqrt(jnp.sum(k_h * k_h, axis=-1, keepdims=True) + NORM_EPS))
        kn_b.append(kn[h].astype(BF16))
        g_col.append(g_cum[:, h:h + 1])
        beta.append(beta_all[:, C_HEADS + h:C_HEADS + h + 1])
        dec_incl.append(jnp.exp(jnp.where(causal, g_col[h] - g_cum_t[h:h + 1, :], -jnp.inf)))
        e_g.append(jnp.exp(g_col[h]))
    kkt = [_dot_nt(kn_b[h], kn_b[h]) for h in heads]
    qk = [_dot_nt(qn[h].astype(BF16), kn_b[h]) for h in heads]
    l_mat = [beta[h] * kkt[h] * jnp.where(strict, dec_incl[h], 0.0) for h in heads]
    a_pow = [jnp.where(base_mask, -l_mat[h], 0.0) for h in heads]
    t_inv = [eye + a_pow[h] for h in heads]
    for _ in range(GDN_BASE_LOG - 1):
        a_pow = [_dot_inv(a_pow[h], a_pow[h]) for h in heads]
        t_inv = [t_inv[h] + _dot_inv(t_inv[h], a_pow[h]) for h in heads]
    for lm in level_masks:
        x = [_dot_inv(jnp.where(lm, l_mat[h], 0.0), t_inv[h]) for h in heads]
        t_inv = [t_inv[h] - _dot_inv(t_inv[h], x[h]) for h in heads]
    uw = [_dot_inv(t_inv[h], jnp.concatenate([beta[h] * vh[h], (beta[h] * e_g[h]) * kn[h]], axis=1))
          for h in heads]
    s_old = [s_ref[h] for h in heads]
    s_b = [s_old[h].astype(BF16) for h in heads]
    ws = [_dot(uw[h][:, C_DIM:].astype(BF16), s_b[h]) for h in heads]
    qs = [_dot((qn[h] * e_g[h]).astype(BF16), s_b[h]) for h in heads]
    delta_b = [(uw[h][:, :C_DIM] - ws[h]).astype(BF16) for h in heads]
    o = [qs[h] + _dot((qk[h] * dec_incl[h]).astype(BF16), delta_b[h]) for h in heads]
    for h in heads:
        g_last = g_col[h][n - 1:n]
        k_dec = kn[h] * jnp.exp(g_last - g_col[h])
        s_ref[h] = jnp.exp(g_last) * s_old[h] + _dot(k_dec.T.astype(BF16), delta_b[h])
    for h in heads:
        hs = slice(h * C_DIM, (h + 1) * C_DIM)
        o_ref[:, hs] = _rms_gate(o[h], ng_ref[...], gate_ref[:, hs])

    @pl.when(i == pl.num_programs(1) - 1)
    def _():
        s_out_ref[...] = s_ref[...]


def _gdn_prompt(proj, conv_w, alog_row, dtb_row, norm_g):
    bsz, t, _ = proj.shape
    n = GDN_CHUNK

    def col(cidx):
        return pl.BlockSpec((None, n, C_QK), lambda b, i: (b, i, cidx))

    one = pl.BlockSpec((1, LANES), lambda b, i: (0, 0))
    return pl.pallas_call(
        _gdn_kernel,
        grid=(bsz, t // n),
        in_specs=[
            col(0), col(1), col(2), col(3),
            pl.BlockSpec((None, n, LANES), lambda b, i: (b, i, (C_CONV_DIM + C_QK) // LANES)),
            pl.BlockSpec((C_CONV, C_CONV_DIM), lambda b, i: (0, 0)),
            one, one, one,
        ],
        out_specs=[
            pl.BlockSpec((None, n, C_QK), lambda b, i: (b, i, 0)),
            pl.BlockSpec((None, C_HEADS, C_DIM, C_DIM), lambda b, i: (b, 0, 0, 0)),
        ],
        out_shape=[
            jax.ShapeDtypeStruct((bsz, t, C_QK), F32),
            jax.ShapeDtypeStruct((bsz, C_HEADS, C_DIM, C_DIM), F32),
        ],
        scratch_shapes=[pltpu.VMEM((C_HEADS, C_DIM, C_DIM), F32), pltpu.VMEM((n + 8, C_CONV_DIM), F32)],
        compiler_params=_params(("parallel", "arbitrary")),
        name="gdn_prompt",
    )(proj, proj, proj, proj, proj, conv_w, alog_row, dtb_row, norm_g.reshape(1, C_DIM))


def _swa_dec_kernel(sink_ref, q_ref, kn_ref, vn_ref, ck_ref, cv_ref, o_ref, ok_ref, ov_ref):
    nb = q_ref.shape[0]
    row = lax.broadcasted_iota(jnp.int32, (WINDOW, LANES), 0)
    grp = lax.broadcasted_iota(jnp.int32, (A_GROUP, WINDOW), 0)
    dist = (WINDOW - 1 - lax.broadcasted_iota(jnp.int32, (A_GROUP, WINDOW), 1)).astype(F32)
    scale = A_HEAD_DIM ** -0.5
    for bb in range(nb):
        keys = jnp.where(row == WINDOW - 1, kn_ref[bb], pltpu.roll(ck_ref[bb], WINDOW - 1, 0))
        vals = jnp.where(row == WINDOW - 1, vn_ref[bb], pltpu.roll(cv_ref[bb], WINDOW - 1, 0))
        ok_ref[bb] = keys
        ov_ref[bb] = vals
        for kv in range(A_KV_HEADS):
            ks = slice(kv * A_HEAD_DIM, (kv + 1) * A_HEAD_DIM)
            h0 = kv * A_GROUP
            slope = jnp.zeros((A_GROUP, WINDOW), F32)
            sink = jnp.zeros((A_GROUP, 1), F32)
            for g in range(A_GROUP):
                slope = jnp.where(grp == g, 2.0 ** (-8.0 * (h0 + g + 1) / A_HEADS), slope)
                sink = jnp.where(grp[:, 0:1] == g, sink_ref[h0 + g], sink)
            q4 = q_ref[bb, h0:h0 + A_GROUP, :].astype(BF16)
            s = _dot_nt(q4, keys[:, ks].astype(BF16)) * scale - slope * dist
            m = jnp.maximum(jnp.max(s, axis=-1, keepdims=True), sink)
            p = jnp.exp(s - m)
            den = jnp.sum(p, axis=-1, keepdims=True) + jnp.exp(sink - m)
            o = _dot(p.astype(BF16), vals[:, ks].astype(BF16))
            o_ref[bb, h0:h0 + A_GROUP, :] = o / den


def _swa_decode(proj2, cache_k, cache_v, sinks):
    nbatch = proj2.shape[0]
    bb = 8
    q3 = proj2[:, :A_Q].reshape(nbatch, A_HEADS, A_HEAD_DIM)
    kn = proj2[:, A_Q:A_Q + A_KV].reshape(nbatch, 1, A_KV)
    vn = proj2[:, A_Q + A_KV:A_Q + 2 * A_KV].reshape(nbatch, 1, A_KV)
    ck = cache_k.reshape(nbatch, WINDOW, A_KV)
    cv = cache_v.reshape(nbatch, WINDOW, A_KV)
    qspec = pl.BlockSpec((bb, A_HEADS, A_HEAD_DIM), lambda j: (j, 0, 0))
    nspec = pl.BlockSpec((bb, 1, A_KV), lambda j: (j, 0, 0))
    cspec = pl.BlockSpec((bb, WINDOW, A_KV), lambda j: (j, 0, 0))
    o, nk, nv = pl.pallas_call(
        _swa_dec_kernel,
        grid=(nbatch // bb,),
        in_specs=[pl.BlockSpec(memory_space=pltpu.SMEM), qspec, nspec, nspec, cspec, cspec],
        out_specs=[qspec, cspec, cspec],
        out_shape=[
            jax.ShapeDtypeStruct((nbatch, A_HEADS, A_HEAD_DIM), F32),
            jax.ShapeDtypeStruct((nbatch, WINDOW, A_KV), F32),
            jax.ShapeDtypeStruct((nbatch, WINDOW, A_KV), F32),
        ],
        compiler_params=_params(("parallel",)),
        name="swa_decode",
    )(sinks, q3, kn, vn, ck, cv)
    shape5 = (nbatch, WINDOW, A_KV_HEADS, A_HEAD_DIM)
    return o.reshape(nbatch, A_Q), nk.reshape(shape5), nv.reshape(shape5)


def _cols(rows, heads):
    nbatch = rows.shape[0]
    x = rows.reshape(nbatch // DEC_BLOCK, DEC_BLOCK, heads, LANES)
    return x.transpose(2, 0, 3, 1)


def _hgrn_dec_kernel(qt_ref, ft_ref, lb_ref, v_ref, g_ref, ng_ref, s_ref, o_ref, so_ref, orow_ref):
    lb = lb_ref[...]
    qc = _silu(qt_ref[...])
    f = lb + (1.0 - lb) * jax.nn.sigmoid(ft_ref[...])
    kk = 1.0 - f
    v = v_ref[...]
    for bb in range(DEC_BLOCK):
        s_new = f[:, bb:bb + 1] * s_ref[bb] + kk[:, bb:bb + 1] * v[bb:bb + 1, :]
        so_ref[bb] = s_new
        orow_ref[bb:bb + 1, :] = jnp.sum(qc[:, bb:bb + 1] * s_new, axis=0, keepdims=True)
    o_ref[...] = _rms_gate(orow_ref[...], ng_ref[...], g_ref[...])


def _hgrn_decode(proj2, state, lower, norm_g):
    nbatch = proj2.shape[0]
    nh = B_HEADS
    base = A_Q + 2 * A_KV
    width = nh * B_DIM
    qt = _cols(proj2[:, base:base + width], nh)
    ft = _cols(proj2[:, base + width:base + 2 * width], nh)
    colspec = pl.BlockSpec((None, None, B_DIM, DEC_BLOCK), lambda h, j: (h, j, 0, 0))
    sspec = pl.BlockSpec((DEC_BLOCK, None, B_DIM, B_DIM), lambda h, j: (j, h, 0, 0))
    cb = base // LANES
    o, s_new = pl.pallas_call(
        _hgrn_dec_kernel,
        grid=(nh, nbatch // DEC_BLOCK),
        in_specs=[
            colspec, colspec,
            pl.BlockSpec((None, B_DIM, 1), lambda h, j: (h, 0, 0)),
            pl.BlockSpec((DEC_BLOCK, B_DIM), lambda h, j: (j, cb + 2 * nh + h)),
            pl.BlockSpec((DEC_BLOCK, B_DIM), lambda h, j: (j, cb + 3 * nh + h)),
            pl.BlockSpec((1, B_DIM), lambda h, j: (0, 0)),
            sspec,
        ],
        out_specs=[pl.BlockSpec((DEC_BLOCK, B_DIM), lambda h, j: (j, h)), sspec],
        out_shape=[
            jax.ShapeDtypeStruct((nbatch, width), F32),
            jax.ShapeDtypeStruct(state.shape, F32),
        ],
        scratch_shapes=[pltpu.VMEM((DEC_BLOCK, B_DIM), F32)],
        compiler_params=_params(("parallel", "parallel")),
        name="hgrn_decode",
    )(qt, ft, lower.reshape(nh, B_DIM, 1), proj2, proj2, norm_g.reshape(1, B_DIM), state)
    return o, s_new


def _gdn_prep_kernel(p_ref, h0_ref, h1_ref, h2_ref, cw_ref, alog_ref, dtb_ref, q_ref, k_ref, v_ref, ab_ref):
    acc = (h0_ref[...] * cw_ref[0:1, :] + h1_ref[...] * cw_ref[1:2, :] + h2_ref[...] * cw_ref[2:3, :]
           + p_ref[:, 0:C_CONV_DIM] * cw_ref[3:4, :])
    qkv = _silu(acc)
    for h in range(C_HEADS):
        hs = slice(h * C_DIM, (h + 1) * C_DIM)
        qh = qkv[:, h * C_DIM:(h + 1) * C_DIM]
        kh = qkv[:, C_QK + h * C_DIM:C_QK + (h + 1) * C_DIM]
        q_ref[:, hs] = qh * lax.rsqrt(jnp.sum(qh * qh, axis=-1, keepdims=True) + NORM_EPS) * (C_DIM ** -0.5)
        k_ref[:, hs] = kh * lax.rsqrt(jnp.sum(kh * kh, axis=-1, keepdims=True) + NORM_EPS)
    v_ref[...] = qkv[:, 2 * C_QK:3 * C_QK]
    ab = p_ref[:, C_CONV_DIM + C_QK:C_CONV_DIM + C_QK + LANES]
    alpha = jnp.exp(-jnp.exp(alog_ref[...]) * _softplus(ab + dtb_ref[...]))
    beta = jax.nn.sigmoid(ab)
    lane = lax.broadcasted_iota(jnp.int32, ab.shape, 1)
    ab_ref[...] = jnp.where(lane < C_HEADS, alpha, beta)


def _gdn_dec_kernel(qt_ref, kt_ref, v_ref, gate_ref, al_ref, be_ref, ng_ref, s_ref, o_ref, so_ref, orow_ref):
    qt = qt_ref[...]
    kt = kt_ref[...]
    v = v_ref[...]
    al = al_ref[...]
    be = be_ref[...]
    for bb in range(DEC_BLOCK):
        s = s_ref[bb]
        kc = kt[:, bb:bb + 1]
        a = al[bb:bb + 1, :]
        ks = jnp.sum(kc * s, axis=0, keepdims=True)
        delta = be[bb:bb + 1, :] * (v[bb:bb + 1, :] - a * ks)
        s_new = a * s + kc * delta
        so_ref[bb] = s_new
        orow_ref[bb:bb + 1, :] = jnp.sum(qt[:, bb:bb + 1] * s_new, axis=0, keepdims=True)
    o_ref[...] = _rms_gate(orow_ref[...], ng_ref[...], gate_ref[...])


def _gdn_decode(proj2, hist, state, conv_w, alog_row, dtb_row, norm_g):
    nbatch = proj2.shape[0]
    nh = C_HEADS
    full = lambda shape: pl.BlockSpec(shape, lambda: tuple(0 for _ in shape))
    hspec = full((nbatch, C_CONV_DIM))
    qn, kn, vc, ab = pl.pallas_call(
        _gdn_prep_kernel,
        grid=(),
        in_specs=[
            full((nbatch, ODD_IN_PAD)), hspec, hspec, hspec,
            full((C_CONV, C_CONV_DIM)), full((1, LANES)), full((1, LANES)),
        ],
        out_specs=[full((nbatch, C_QK)), full((nbatch, C_QK)), full((nbatch, C_QK)), full((nbatch, LANES))],
        out_shape=[
            jax.ShapeDtypeStruct((nbatch, C_QK), F32),
            jax.ShapeDtypeStruct((nbatch, C_QK), F32),
            jax.ShapeDtypeStruct((nbatch, C_QK), F32),
            jax.ShapeDtypeStruct((nbatch, LANES), F32),
        ],
        compiler_params=pltpu.CompilerParams(vmem_limit_bytes=VMEM_LIMIT),
        name="gdn_decode_prep",
    )(proj2, hist[:, 0], hist[:, 1], hist[:, 2], conv_w, alog_row, dtb_row)
    qt = _cols(qn, nh)
    kt = _cols(kn, nh)
    al = jnp.broadcast_to(ab[:, :nh].T[:, :, None], (nh, nbatch, LANES))
    be = jnp.broadcast_to(ab[:, nh:2 * nh].T[:, :, None], (nh, nbatch, LANES))
    colspec = pl.BlockSpec((None, None, C_DIM, DEC_BLOCK), lambda h, j: (h, j, 0, 0))
    sspec = pl.BlockSpec((DEC_BLOCK, None, C_DIM, C_DIM), lambda h, j: (j, h, 0, 0))
    rspec = pl.BlockSpec((None, DEC_BLOCK, LANES), lambda h, j: (h, j, 0))
    gcol = C_CONV_DIM // LANES
    o, s_new = pl.pallas_call(
        _gdn_dec_kernel,
        grid=(nh, nbatch // DEC_BLOCK),
        in_specs=[
            colspec, colspec,
            pl.BlockSpec((DEC_BLOCK, C_DIM), lambda h, j: (j, h)),
            pl.BlockSpec((DEC_BLOCK, C_DIM), lambda h, j: (j, gcol + h)),
            rspec, rspec,
            pl.BlockSpec((1, C_DIM), lambda h, j: (0, 0)),
            sspec,
        ],
        out_specs=[pl.BlockSpec((DEC_BLOCK, C_DIM), lambda h, j: (j, h)), sspec],
        out_shape=[
            jax.ShapeDtypeStruct((nbatch, C_QK), F32),
            jax.ShapeDtypeStruct(state.shape, F32),
        ],
        scratch_shapes=[pltpu.VMEM((DEC_BLOCK, C_DIM), F32)],
        compiler_params=_params(("parallel", "parallel")),
        name="gdn_decode",
    )(qt, kt, vc, proj2, al, be, norm_g.reshape(1, C_DIM), state)
    return o, s_new


def _trunk(x, mods, caches, p, tm):
    decode = caches is not None
    m = mods[0]
    x = _ffn(x, m, 0, p["w_up"][0, 0], p["w_down"][0, 0], p["ln_g"][0, 0], p["ln_b"][0, 0], tm)
    proj = _inproj(x, m, 3, p["even_in"], tm)
    if decode:
        o_a, new_k, new_v = _swa_decode(proj[0], caches[0][0], caches[1][0], p["sinks"])
        o_b, s_hgrn = _hgrn_decode(proj[0], caches[2][0], p["lower"], p["hgrn_norm_g"])
        o_a, o_b = o_a[None], o_b[None]
    else:
        bsz = x.shape[0]
        o_a = _swa_prompt(proj, p["sinks"])
        new_k = proj[:, -WINDOW:, A_Q:A_Q + A_KV].reshape(bsz, WINDOW, A_KV_HEADS, A_HEAD_DIM)
        new_v = proj[:, -WINDOW:, A_Q + A_KV:A_Q + 2 * A_KV].reshape(bsz, WINDOW, A_KV_HEADS, A_HEAD_DIM)
        o_b, s_hgrn = _hgrn_prompt(proj, p["lower"], p["hgrn_norm_g"])
    x = _outproj(x, o_a, 0, o_b, 0, m, 5, p["even_out"], p["ln_g"][0, 1], p["ln_b"][0, 1], tm)
    x = _ffn(x, m, 6, p["w_up"][0, 1], p["w_down"][0, 1], p["ln_g"][0, 2], p["ln_b"][0, 2], tm)
    m = mods[1]
    x = _ffn(x, m, 0, p["w_up"][1, 0], p["w_down"][1, 0], p["ln_g"][1, 0], p["ln_b"][1, 0], tm)
    proj = _inproj(x, m, 3, p["odd_in"], tm)
    if decode:
        hist = caches[4][0]
        o_c, s_gdn = _gdn_decode(proj[0], hist, caches[3][0], p["conv_w"], p["alog_row"], p["dtb_row"],
                                 p["gdn_norm_g"])
        o_c = o_c[None]
        new_hist = jnp.concatenate([hist[:, 1:], proj[0][:, None, :C_CONV_DIM]], axis=1)
    else:
        o_c, s_gdn = _gdn_prompt(proj, p["conv_w"], p["alog_row"], p["dtb_row"], p["gdn_norm_g"])
        new_hist = proj[:, -(C_CONV - 1):, :C_CONV_DIM]
    x = _outproj(x, o_c, 0, o_c, 1, m, 5, p["odd_out"], p["ln_g"][1, 1], p["ln_b"][1, 1], tm)
    x = _ffn(x, m, 6, p["w_up"][1, 1], p["w_down"][1, 1], p["ln_g"][1, 2], p["ln_b"][1, 2], tm)
    return x, new_k[None], new_v[None], s_hgrn[None], s_gdn[None], new_hist[None]


def kernel(x_prompt, x_sample, cache_swa_k, cache_swa_v, state_hgrn, state_gdn, state_gdn_conv, c_prompt, c_sample, ada_w, ada_b, ln_g, ln_b, ffn_w_up, ffn_w_down, even_w_in, even_w_out, swa_sinks, hgrn_norm_g, hgrn_lb_logits, odd_w_in, odd_w_out, gdn_conv_w, gdn_a_log, gdn_dt_bias, gdn_norm_g):
    n_prompt = c_prompt.shape[0]
    n_sample = c_sample.shape[0]
    pad_rows = (-(n_prompt + n_sample)) % 8
    c_all = jnp.concatenate([c_prompt, c_sample, jnp.zeros((pad_rows, D_MODEL), F32)], axis=0)
    mods = _ada_mods(c_all, ada_w, ada_b)
    mods_p = mods[:, :n_prompt].reshape(DEPTH, n_prompt, 1, N_MOD * D_MODEL)
    mods_s = mods[:, n_prompt:n_prompt + n_sample].reshape(DEPTH, 1, n_sample, N_MOD * D_MODEL)

    probs = jax.nn.softmax(hgrn_lb_logits.astype(F32), axis=0)
    lower = (jnp.cumsum(probs, axis=0)[1:] - probs[0])[0]

    def lane_row(v):
        return jnp.pad(v.astype(F32), (0, LANES - v.shape[0])).reshape(1, LANES)

    p = dict(
        w_up=ffn_w_up.astype(BF16), w_down=ffn_w_down.astype(BF16),
        ln_g=ln_g, ln_b=ln_b,
        even_in=even_w_in[0].astype(BF16), even_out=even_w_out[0].astype(BF16),
        odd_in=jnp.pad(odd_w_in[0], ((0, 0), (0, ODD_IN_PAD - ODD_IN))).astype(BF16),
        odd_out=odd_w_out[0].astype(BF16),
        sinks=swa_sinks[0], lower=lower, hgrn_norm_g=hgrn_norm_g[0],
        conv_w=gdn_conv_w[0], alog_row=lane_row(gdn_a_log[0]), dtb_row=lane_row(gdn_dt_bias[0]),
        gdn_norm_g=gdn_norm_g[0],
    )
    y_p, p_k, p_v, p_hgrn, p_gdn, p_conv = _trunk(x_prompt, mods_p, None, p, 512)
    caches = (cache_swa_k, cache_swa_v, state_hgrn, state_gdn, state_gdn_conv)
    x_s = x_sample.reshape(1, n_sample, D_MODEL)
    y_s, s_k, s_v, s_hgrn, s_gdn, s_conv = _trunk(x_s, mods_s, caches, p, n_sample)
    y_s = y_s.reshape(n_sample, 1, D_MODEL)
    return (y_p, y_s, p_k, p_v, p_hgrn, p_gdn, p_conv, s_k, s_v, s_hgrn, s_gdn, s_conv)
```

```python
import functools

import jax
import jax.numpy as jnp
from jax import lax
from jax.experimental import pallas as pl
from jax.experimental.pallas import tpu as pltpu

F32 = jnp.float32
BF16 = jnp.bfloat16
HIGHEST = lax.Precision.HIGHEST

D_MODEL = 1024
DEPTH = 2
WINDOW = 128
A_HEADS = 8
A_KV_HEADS = 2
A_GROUP = A_HEADS // A_KV_HEADS
A_HEAD_DIM = 64
A_Q = A_HEADS * A_HEAD_DIM
A_KV = A_KV_HEADS * A_HEAD_DIM
B_HEADS = 4
B_DIM = 128
EVEN_IN = 2816
C_HEADS = 8
C_DIM = 128
C_QK = C_HEADS * C_DIM
C_CONV = 4
C_CONV_DIM = 3 * C_QK
ODD_IN = C_CONV_DIM + C_QK + 2 * C_HEADS
ODD_IN_PAD = 4224
D_FF = 2816
N_MOD = 9
DN_ALPHA = (2 * DEPTH) ** 0.25
LN_EPS = 1e-5
NORM_EPS = 1e-6
LANES = 128

FF_TILE = 1408
PROJ_TILE = 1408
HGRN_BLOCK = 128
HGRN_SUB = 8
GDN_CHUNK_LOG = 7
GDN_CHUNK = 1 << GDN_CHUNK_LOG
GDN_BASE_LOG = 3
DEC_BLOCK = 64
VMEM_LIMIT = 48 << 20


def _params(sem, vmem=VMEM_LIMIT):
    return pltpu.CompilerParams(dimension_semantics=sem, vmem_limit_bytes=vmem)


def _silu(x):
    return x * jax.nn.sigmoid(x)


def _softplus(x):
    return jnp.maximum(x, 0.0) + jnp.log(1.0 + jnp.exp(-jnp.abs(x)))


def _layer_norm(y, g, b):
    mu = jnp.mean(y, axis=-1, keepdims=True)
    d = y - mu
    var = jnp.mean(d * d, axis=-1, keepdims=True)
    return d * lax.rsqrt(var + LN_EPS) * g + b


def _rms_gate(o, norm_g, gate):
    y = o * lax.rsqrt(jnp.mean(o * o, axis=-1, keepdims=True) + NORM_EPS)
    return y * norm_g * _silu(gate)


def _dot(a, b, precision=None):
    return jnp.dot(a, b, preferred_element_type=F32, precision=precision)


def _dot_inv(a, b):
    return _dot(a.astype(BF16), b.astype(BF16))


def _dot_nt(a, b, precision=None):
    return lax.dot_general(a, b, (((1,), (1,)), ((), ())), preferred_element_type=F32, precision=precision)


def _ada_kernel(c_ref, w_ref, b_ref, o_ref):
    cs = _silu(c_ref[...]).astype(BF16)
    o_ref[...] = _dot(cs, w_ref[...].astype(BF16)) + b_ref[...]


def _ada_mods(c_all, ada_w, ada_b):
    m = c_all.shape[0]
    n = N_MOD * D_MODEL
    tn = 1152
    return pl.pallas_call(
        _ada_kernel,
        grid=(DEPTH, n // tn),
        in_specs=[
            pl.BlockSpec((m, D_MODEL), lambda l, j: (0, 0)),
            pl.BlockSpec((None, D_MODEL, tn), lambda l, j: (l, 0, j)),
            pl.BlockSpec((None, 1, tn), lambda l, j: (l, 0, j)),
        ],
        out_specs=pl.BlockSpec((None, m, tn), lambda l, j: (l, 0, j)),
        out_shape=jax.ShapeDtypeStruct((DEPTH, m, n), F32),
        compiler_params=_params(("parallel", "parallel")),
        name="ada_mods",
    )(c_all, ada_w, ada_b.reshape(DEPTH, 1, n))


def _mod_spec(mods, k, tm, grid_rank):
    per_token = mods.shape[1] != 1
    rows = tm if per_token else 1
    if grid_rank == 3:
        return pl.BlockSpec((None, rows, D_MODEL), lambda b, i, j: (b, i if per_token else 0, k))
    return pl.BlockSpec((None, rows, D_MODEL), lambda b, i: (b, i if per_token else 0, k))


def _ffn_kernel(x_ref, sh_ref, sc_ref, g_ref, wg_ref, wu_ref, wd_ref, lg_ref, lb_ref, o_ref, h_ref, acc_ref):
    j = pl.program_id(2)

    @pl.when(j == 0)
    def _():
        h_ref[...] = (x_ref[...] * (1.0 + sc_ref[...]) + sh_ref[...]).astype(BF16)
        acc_ref[...] = jnp.zeros_like(acc_ref)

    h = h_ref[...]
    gate = _dot(h, wg_ref[...])
    up = _dot(h, wu_ref[...])
    act = (_silu(gate) * up).astype(BF16)
    acc_ref[...] += _dot(act, wd_ref[...])

    @pl.when(j == pl.num_programs(2) - 1)
    def _():
        y = DN_ALPHA * x_ref[...] + (0.5 * g_ref[...]) * acc_ref[...]
        o_ref[...] = _layer_norm(y, lg_ref[...], lb_ref[...])


def _ffn(x, mods, k0, w_up, w_down, ln_g, ln_b, tm):
    bsz, t, _ = x.shape
    nf = D_FF // FF_TILE
    row = pl.BlockSpec((None, tm, D_MODEL), lambda b, i, j: (b, i, 0))
    vec = pl.BlockSpec((1, D_MODEL), lambda b, i, j: (0, 0))
    return pl.pallas_call(
        _ffn_kernel,
        grid=(bsz, t // tm, nf),
        in_specs=[
            row,
            _mod_spec(mods, k0, tm, 3), _mod_spec(mods, k0 + 1, tm, 3), _mod_spec(mods, k0 + 2, tm, 3),
            pl.BlockSpec((D_MODEL, FF_TILE), lambda b, i, j: (0, j)),
            pl.BlockSpec((D_MODEL, FF_TILE), lambda b, i, j: (0, j + nf)),
            pl.BlockSpec((FF_TILE, D_MODEL), lambda b, i, j: (j, 0)),
            vec, vec,
        ],
        out_specs=row,
        out_shape=jax.ShapeDtypeStruct(x.shape, F32),
        scratch_shapes=[pltpu.VMEM((tm, D_MODEL), BF16), pltpu.VMEM((tm, D_MODEL), F32)],
        compiler_params=_params(("parallel", "parallel", "arbitrary")),
        name="ffn",
    )(x, mods, mods, mods, w_up, w_up, w_down, ln_g.reshape(1, D_MODEL), ln_b.reshape(1, D_MODEL))


def _inproj_kernel(x_ref, sh_ref, sc_ref, w_ref, o_ref, h_ref):
    @pl.when(pl.program_id(2) == 0)
    def _():
        h_ref[...] = (x_ref[...] * (1.0 + sc_ref[...]) + sh_ref[...]).astype(BF16)

    o_ref[...] = _dot(h_ref[...], w_ref[...])


def _inproj(x, mods, k0, w, tm):
    bsz, t, _ = x.shape
    n = w.shape[1]
    return pl.pallas_call(
        _inproj_kernel,
        grid=(bsz, t // tm, n // PROJ_TILE),
        in_specs=[
            pl.BlockSpec((None, tm, D_MODEL), lambda b, i, j: (b, i, 0)),
            _mod_spec(mods, k0, tm, 3), _mod_spec(mods, k0 + 1, tm, 3),
            pl.BlockSpec((D_MODEL, PROJ_TILE), lambda b, i, j: (0, j)),
        ],
        out_specs=pl.BlockSpec((None, tm, PROJ_TILE), lambda b, i, j: (b, i, j)),
        out_shape=jax.ShapeDtypeStruct((bsz, t, n), F32),
        scratch_shapes=[pltpu.VMEM((tm, D_MODEL), BF16)],
        compiler_params=_params(("parallel", "parallel", "arbitrary")),
        name="inproj",
    )(x, mods, mods, w)


def _outproj_kernel(x_ref, o1_ref, o2_ref, g_ref, w1_ref, w2_ref, lg_ref, lb_ref, y_ref):
    mix = _dot(o1_ref[...].astype(BF16), w1_ref[...]) + _dot(o2_ref[...].astype(BF16), w2_ref[...])
    y = DN_ALPHA * x_ref[...] + g_ref[...] * mix
    y_ref[...] = _layer_norm(y, lg_ref[...], lb_ref[...])


def _outproj(x, o1, c1, o2, c2, mods, kg, w_out, ln_g, ln_b, tm):
    bsz, t, _ = x.shape
    half = D_MODEL // 2
    row = pl.BlockSpec((None, tm, D_MODEL), lambda b, i: (b, i, 0))
    vec = pl.BlockSpec((1, D_MODEL), lambda b, i: (0, 0))
    return pl.pallas_call(
        _outproj_kernel,
        grid=(bsz, t // tm),
        in_specs=[
            row,
            pl.BlockSpec((None, tm, half), lambda b, i: (b, i, c1)),
            pl.BlockSpec((None, tm, half), lambda b, i: (b, i, c2)),
            _mod_spec(mods, kg, tm, 2),
            pl.BlockSpec((half, D_MODEL), lambda b, i: (0, 0)),
            pl.BlockSpec((half, D_MODEL), lambda b, i: (1, 0)),
            vec, vec,
        ],
        out_specs=row,
        out_shape=jax.ShapeDtypeStruct(x.shape, F32),
        compiler_params=_params(("parallel", "parallel")),
        name="outproj",
    )(x, o1, o2, mods, w_out, w_out, ln_g.reshape(1, D_MODEL), ln_b.reshape(1, D_MODEL))


def _swa_kernel(sink_ref, q_ref, kc_ref, kp_ref, vc_ref, vp_ref, o_ref):
    i = pl.program_id(1)
    q = q_ref[...]
    kc = kc_ref[...].astype(BF16)
    kp = kp_ref[...].astype(BF16)
    vc = vc_ref[...].astype(BF16)
    vp = vp_ref[...].astype(BF16)
    r = lax.broadcasted_iota(jnp.int32, (WINDOW, WINDOW), 0)
    c = lax.broadcasted_iota(jnp.int32, (WINDOW, WINDOW), 1)
    dist_c = (r - c).astype(F32)
    dist_p = dist_c + float(WINDOW)
    valid_c = c <= r
    valid_p = c > r + jnp.where(i > 0, 0, WINDOW)
    scale = A_HEAD_DIM ** -0.5
    heads = range(A_HEADS)
    hsl = [slice(h * A_HEAD_DIM, (h + 1) * A_HEAD_DIM) for h in heads]
    ksl = [slice((h // A_GROUP) * A_HEAD_DIM, (h // A_GROUP + 1) * A_HEAD_DIM) for h in heads]
    slope = [2.0 ** (-8.0 * (h + 1) / A_HEADS) for h in heads]
    qh = [q[:, hsl[h]].astype(BF16) for h in heads]
    s_c = [_dot_nt(qh[h], kc[:, ksl[h]]) for h in heads]
    s_p = [_dot_nt(qh[h], kp[:, ksl[h]]) for h in heads]
    s_c = [jnp.where(valid_c, s_c[h] * scale - slope[h] * dist_c, -jnp.inf) for h in heads]
    s_p = [jnp.where(valid_p, s_p[h] * scale - slope[h] * dist_p, -jnp.inf) for h in heads]
    m = [jnp.maximum(jnp.maximum(jnp.max(s_c[h], axis=-1, keepdims=True), jnp.max(s_p[h], axis=-1, keepdims=True)),
                     sink_ref[h]) for h in heads]
    p_c = [jnp.exp(s_c[h] - m[h]) for h in heads]
    p_p = [jnp.exp(s_p[h] - m[h]) for h in heads]
    den = [jnp.sum(p_c[h], axis=-1, keepdims=True) + jnp.sum(p_p[h], axis=-1, keepdims=True)
           + jnp.exp(sink_ref[h] - m[h]) for h in heads]
    o = [_dot(p_c[h].astype(BF16), vc[:, ksl[h]]) + _dot(p_p[h].astype(BF16), vp[:, ksl[h]]) for h in heads]
    for h in heads:
        o_ref[:, hsl[h]] = o[h] / den[h]


def _swa_prompt(proj, sinks):
    bsz, t, _ = proj.shape
    nb = t // WINDOW
    kcol = A_Q // LANES
    vcol = kcol + 1

    def cur(col):
        return pl.BlockSpec((None, WINDOW, LANES), lambda b, i: (b, i, col))

    def prev(col):
        return pl.BlockSpec((None, WINDOW, LANES), lambda b, i: (b, jnp.maximum(i - 1, 0), col))

    return pl.pallas_call(
        _swa_kernel,
        grid=(bsz, nb),
        in_specs=[
            pl.BlockSpec(memory_space=pltpu.SMEM),
            pl.BlockSpec((None, WINDOW, A_Q), lambda b, i: (b, i, 0)),
            cur(kcol), prev(kcol), cur(vcol), prev(vcol),
        ],
        out_specs=pl.BlockSpec((None, WINDOW, A_Q), lambda b, i: (b, i, 0)),
        out_shape=jax.ShapeDtypeStruct((bsz, t, A_Q), F32),
        compiler_params=_params(("parallel", "parallel")),
        name="swa_prompt",
    )(sinks, proj, proj, proj, proj, proj)


def _split3(x):
    hi = x.astype(BF16)
    r1 = x - hi.astype(F32)
    mid = r1.astype(BF16)
    lo = (r1 - mid.astype(F32)).astype(BF16)
    return hi, mid, lo


def _hgrn_kernel(q0_ref, q1_ref, f0_ref, f1_ref, v0_ref, v1_ref, g0_ref, g1_ref, lb_ref, ng_ref,
                 o_ref, s_ref, st_ref):
    i = pl.program_id(1)

    @pl.when(i == 0)
    def _():
        st_ref[...] = jnp.zeros_like(st_ref)

    n = HGRN_BLOCK
    nsub = n // HGRN_SUB
    heads = range(B_HEADS)
    lb = lb_ref[...]
    qb = _silu(jnp.concatenate([q0_ref[...], q1_ref[...]], axis=1))
    f = lb + (1.0 - lb) * jax.nn.sigmoid(jnp.concatenate([f0_ref[...], f1_ref[...]], axis=1))
    lf = jnp.log(f)
    kk = 1.0 - f
    v = jnp.concatenate([v0_ref[...], v1_ref[...]], axis=1)
    r = lax.broadcasted_iota(jnp.int32, (n, n), 0)
    c = lax.broadcasted_iota(jnp.int32, (n, n), 1)
    tri = jnp.where((c <= r) & (c >= r - (r & (HGRN_SUB - 1))), 1.0, 0.0).astype(BF16)
    lf_hi, lf_mid, lf_lo = _split3(lf)
    a_all = _dot(tri, lf_hi) + (_dot(tri, lf_mid) + _dot(tri, lf_lo))
    a3 = a_all.reshape(nsub, HGRN_SUB, B_HEADS * B_DIM)
    a_last3 = a3[:, HGRN_SUB - 1:HGRN_SUB, :]
    qb3 = qb.reshape(a3.shape)
    kk3 = kk.reshape(a3.shape)
    v3 = v.reshape(a3.shape)
    qe3 = qb3 * jnp.exp(a3)
    kd3 = kk3 * jnp.exp(a_last3 - a3)
    dec3 = jnp.exp(a_last3)
    trow = lax.broadcasted_iota(jnp.int32, (1, HGRN_SUB, 1), 1)
    o_diag = []
    v_t = []
    for h in heads:
        hs = slice(h * B_DIM, (h + 1) * B_DIM)
        a_h, q_h, k_h, v_h = a3[:, :, hs], qb3[:, :, hs], kk3[:, :, hs], v3[:, :, hs]
        acc = jnp.zeros((nsub, HGRN_SUB, B_DIM), F32)
        for s in range(HGRN_SUB):
            e = jnp.exp(jnp.where(trow >= s, a_h - a_h[:, s:s + 1, :], -jnp.inf))
            col = jnp.sum(q_h * e * k_h[:, s:s + 1, :], axis=-1, keepdims=True)
            acc = acc + col * v_h[:, s:s + 1, :]
        o_diag.append(acc.reshape(n, B_DIM))
        v_t.append(v[:, hs].T.astype(BF16))
    state = [st_ref[h] for h in heads]
    o_state = [[] for _ in heads]
    for ch in range(nsub):
        sl = slice(ch * HGRN_SUB, (ch + 1) * HGRN_SUB)
        for h in heads:
            hs = slice(h * B_DIM, (h + 1) * B_DIM)
            o_state[h].append(_dot_nt(qe3[ch, :, hs].astype(BF16), state[h].astype(BF16)))
            state[h] = state[h] * dec3[ch, :, hs] + _dot(v_t[h][:, sl], kd3[ch, :, hs].astype(BF16))
    g = jnp.concatenate([g0_ref[...], g1_ref[...]], axis=1)
    for h in heads:
        hs = slice(h * B_DIM, (h + 1) * B_DIM)
        st_ref[h] = state[h]
        o = jnp.concatenate(o_state[h], axis=0) + o_diag[h]
        o_ref[:, hs] = _rms_gate(o, ng_ref[...], g[:, hs])

    @pl.when(i == pl.num_programs(1) - 1)
    def _():
        for h in heads:
            s_ref[h] = state[h].T


def _hgrn_prompt(proj, lower, norm_g):
    bsz, t, _ = proj.shape
    nh = B_HEADS
    width = nh * B_DIM
    half = width // 2
    base = (A_Q + 2 * A_KV) // half

    def col(off):
        return pl.BlockSpec((None, HGRN_BLOCK, half), lambda b, i: (b, i, base + off))

    return pl.pallas_call(
        _hgrn_kernel,
        grid=(bsz, t // HGRN_BLOCK),
        in_specs=[
            col(0), col(1), col(2), col(3), col(4), col(5), col(6), col(7),
            pl.BlockSpec((1, width), lambda b, i: (0, 0)),
            pl.BlockSpec((1, B_DIM), lambda b, i: (0, 0)),
        ],
        out_specs=[
            pl.BlockSpec((None, HGRN_BLOCK, width), lambda b, i: (b, i, 0)),
            pl.BlockSpec((None, nh, B_DIM, B_DIM), lambda b, i: (b, 0, 0, 0)),
        ],
        out_shape=[
            jax.ShapeDtypeStruct((bsz, t, width), F32),
            jax.ShapeDtypeStruct((bsz, nh, B_DIM, B_DIM), F32),
        ],
        scratch_shapes=[pltpu.VMEM((nh, B_DIM, B_DIM), F32)],
        compiler_params=_params(("parallel", "arbitrary")),
        name="hgrn_prompt",
    )(proj, proj, proj, proj, proj, proj, proj, proj, lower.reshape(1, width), norm_g.reshape(1, B_DIM))


def _gdn_kernel(q_ref, k_ref, v_ref, gate_ref, ab_ref, cw_ref, alog_ref, dtb_ref, ng_ref,
                o_ref, s_out_ref, s_ref, xs_ref):
    i = pl.program_id(1)
    n = GDN_CHUNK

    @pl.when(i == 0)
    def _():
        s_ref[...] = jnp.zeros_like(s_ref)
        xs_ref[0:8, :] = jnp.zeros((8, C_CONV_DIM), F32)

    @pl.when(i > 0)
    def _():
        xs_ref[0:8, :] = xs_ref[n:n + 8, :]

    xs_ref[8:n + 8, 0:C_QK] = q_ref[...]
    xs_ref[8:n + 8, C_QK:2 * C_QK] = k_ref[...]
    xs_ref[8:n + 8, 2 * C_QK:3 * C_QK] = v_ref[...]
    off = 8 - (C_CONV - 1)
    acc = xs_ref[off:off + n, :] * cw_ref[0:1, :]
    for j in range(1, C_CONV):
        acc = acc + xs_ref[off + j:off + j + n, :] * cw_ref[j:j + 1, :]
    qkv = _silu(acc)

    ab = ab_ref[...]
    log_alpha = -jnp.exp(alog_ref[...]) * _softplus(ab + dtb_ref[...])
    beta_all = jax.nn.sigmoid(ab)
    r = lax.broadcasted_iota(jnp.int32, (n, n), 0)
    c = lax.broadcasted_iota(jnp.int32, (n, n), 1)
    causal = c <= r
    strict = c < r
    tri = jnp.where(causal, 1.0, 0.0).astype(F32)
    eye = jnp.where(r == c, 1.0, 0.0).astype(F32)
    base_mask = (r >> GDN_BASE_LOG) == (c >> GDN_BASE_LOG)
    level_masks = [((r >> (k + 1)) == (c >> (k + 1))) & ((r >> k) != (c >> k))
                   for k in range(GDN_BASE_LOG, GDN_CHUNK_LOG)]
    g_cum =_dot(tri, log_alpha, HIGHEST)
    g_cum_t = g_cum.T

    heads = range(C_HEADS)
    qn, kn, kn_b, vh, g_col, beta, dec_incl, e_g = [], [], [], [], [], [], [], []
    for h in heads:
        q_h = qkv[:, h * C_DIM:(h + 1) * C_DIM]
        k_h = qkv[:, C_QK + h * C_DIM:C_QK + (h + 1) * C_DIM]
        vh.append(qkv[:, 2 * C_QK + h * C_DIM:2 * C_QK + (h + 1) * C_DIM])
        qn.append(q_h * lax.rsqrt(jnp.sum(q_h * q_h, axis=-1, keepdims=True) + NORM_EPS) * (C_DIM ** -0.5))
        kn.append(k_h * lax.rsqrt(jnp.sum(k_h * k_h, axis=-1, keepdims=True) + NORM_EPS))
        kn_b.append(kn[h].astype(BF16))
        g_col.append(g_cum[:, h:h + 1])
        beta.append(beta_all[:, C_HEADS + h:C_HEADS + h + 1])
        dec_incl.append(jnp.exp(jnp.where(causal, g_col[h] - g_cum_t[h:h + 1, :], -jnp.inf)))
        e_g.append(jnp.exp(g_col[h]))
    kkt = [_dot_nt(kn_b[h], kn_b[h]) for h in heads]
    qk = [_dot_nt(qn[h].astype(BF16), kn_b[h]) for h in heads]
    l_mat = [beta[h] * kkt[h] * jnp.where(strict, dec_incl[h], 0.0) for h in heads]
    a_pow = [jnp.where(base_mask, -l_mat[h], 0.0) for h in heads]
    t_inv = [eye + a_pow[h] for h in heads]
    for _ in range(GDN_BASE_LOG - 1):
        a_pow = [_dot_inv(a_pow[h], a_pow[h]) for h in heads]
        t_inv = [t_inv[h] + _dot_inv(t_inv[h], a_pow[h]) for h in heads]
    for lm in level_masks:
        x = [_dot_inv(jnp.where(lm, l_mat[h], 0.0), t_inv[h]) for h in heads]
        t_inv = [t_inv[h] - _dot_inv(t_inv[h], x[h]) for h in heads]
    uw = [_dot_inv(t_inv[h], jnp.concatenate([beta[h] * vh[h], (beta[h] * e_g[h]) * kn[h]], axis=1))
          for h in heads]
    s_old = [s_ref[h] for h in heads]
    s_b = [s_old[h].astype(BF16) for h in heads]
    ws = [_dot(uw[h][:, C_DIM:].astype(BF16), s_b[h]) for h in heads]
    qs = [_dot((qn[h] * e_g[h]).astype(BF16), s_b[h]) for h in heads]
    delta_b = [(uw[h][:, :C_DIM] - ws[h]).astype(BF16) for h in heads]
    o = [qs[h] + _dot((qk[h] * dec_incl[h]).astype(BF16), delta_b[h]) for h in heads]
    for h in heads:
        g_last = g_col[h][n - 1:n]
        k_dec = kn[h] * jnp.exp(g_last - g_col[h])
        s_ref[h] = jnp.exp(g_last) * s_old[h] + _dot(k_dec.T.astype(BF16), delta_b[h])
    for h in heads:
        hs = slice(h * C_DIM, (h + 1) * C_DIM)
        o_ref[:, hs] = _rms_gate(o[h], ng_ref[...], gate_ref[:, hs])

    @pl.when(i == pl.num_programs(1) - 1)
    def _():
        s_out_ref[...] = s_ref[...]


def _gdn_prompt(proj, conv_w, alog_row, dtb_row, norm_g):
    bsz, t, _ = proj.shape
    n = GDN_CHUNK

    def col(cidx):
        return pl.BlockSpec((None, n, C_QK), lambda b, i: (b, i, cidx))

    one = pl.BlockSpec((1, LANES), lambda b, i: (0, 0))
    return pl.pallas_call(
        _gdn_kernel,
        grid=(bsz, t // n),
        in_specs=[
            col(0), col(1), col(2), col(3),
            pl.BlockSpec((None, n, LANES), lambda b, i: (b, i, (C_CONV_DIM + C_QK) // LANES)),
            pl.BlockSpec((C_CONV, C_CONV_DIM), lambda b, i: (0, 0)),
            one, one, one,
        ],
        out_specs=[
            pl.BlockSpec((None, n, C_QK), lambda b, i: (b, i, 0)),
            pl.BlockSpec((None, C_HEADS, C_DIM, C_DIM), lambda b, i: (b, 0, 0, 0)),
        ],
        out_shape=[
            jax.ShapeDtypeStruct((bsz, t, C_QK), F32),
            jax.ShapeDtypeStruct((bsz, C_HEADS, C_DIM, C_DIM), F32),
        ],
        scratch_shapes=[pltpu.VMEM((C_HEADS, C_DIM, C_DIM), F32), pltpu.VMEM((n + 8, C_CONV_DIM), F32)],
        compiler_params=_params(("parallel", "arbitrary")),
        name="gdn_prompt",
    )(proj, proj, proj, proj, proj, conv_w, alog_row, dtb_row, norm_g.reshape(1, C_DIM))


def _swa_dec_kernel(sink_ref, q_ref, kn_ref, vn_ref, ck_ref, cv_ref, o_ref, ok_ref, ov_ref):
    nb = q_ref.shape[0]
    row = lax.broadcasted_iota(jnp.int32, (WINDOW, LANES), 0)
    grp = lax.broadcasted_iota(jnp.int32, (A_GROUP, WINDOW), 0)
    dist = (WINDOW - 1 - lax.broadcasted_iota(jnp.int32, (A_GROUP, WINDOW), 1)).astype(F32)
    scale = A_HEAD_DIM ** -0.5
    for bb in range(nb):
        keys = jnp.where(row == WINDOW - 1, kn_ref[bb], pltpu.roll(ck_ref[bb], WINDOW - 1, 0))
        vals = jnp.where(row == WINDOW - 1, vn_ref[bb], pltpu.roll(cv_ref[bb], WINDOW - 1, 0))
        ok_ref[bb] = keys
        ov_ref[bb] = vals
        for kv in range(A_KV_HEADS):
            ks = slice(kv * A_HEAD_DIM, (kv + 1) * A_HEAD_DIM)
            h0 = kv * A_GROUP
            slope = jnp.zeros((A_GROUP, WINDOW), F32)
            sink = jnp.zeros((A_GROUP, 1), F32)
            for g in range(A_GROUP):
                slope = jnp.where(grp == g, 2.0 ** (-8.0 * (h0 + g + 1) / A_HEADS), slope)
                sink = jnp.where(grp[:, 0:1] == g, sink_ref[h0 + g], sink)
            q4 = q_ref[bb, h0:h0 + A_GROUP, :].astype(BF16)
            s = _dot_nt(q4, keys[:, ks].astype(BF16)) * scale - slope * dist
            m = jnp.maximum(jnp.max(s, axis=-1, keepdims=True), sink)
            p = jnp.exp(s - m)
            den = jnp.sum(p, axis=-1, keepdims=True) + jnp.exp(sink - m)
            o = _dot(p.astype(BF16), vals[:, ks].astype(BF16))
            o_ref[bb, h0:h0 + A_GROUP, :] = o / den


def _swa_decode(proj2, cache_k, cache_v, sinks):
    nbatch = proj2.shape[0]
    bb = 8
    q3 = proj2[:, :A_Q].reshape(nbatch, A_HEADS, A_HEAD_DIM)
    kn = proj2[:, A_Q:A_Q + A_KV].reshape(nbatch, 1, A_KV)
    vn = proj2[:, A_Q + A_KV:A_Q + 2 * A_KV].reshape(nbatch, 1, A_KV)
    ck = cache_k.reshape(nbatch, WINDOW, A_KV)
    cv = cache_v.reshape(nbatch, WINDOW, A_KV)
    qspec = pl.BlockSpec((bb, A_HEADS, A_HEAD_DIM), lambda j: (j, 0, 0))
    nspec = pl.BlockSpec((bb, 1, A_KV), lambda j: (j, 0, 0))
    cspec = pl.BlockSpec((bb, WINDOW, A_KV), lambda j: (j, 0, 0))
    o, nk, nv = pl.pallas_call(
        _swa_dec_kernel,
        grid=(nbatch // bb,),
        in_specs=[pl.BlockSpec(memory_space=pltpu.SMEM), qspec, nspec, nspec, cspec, cspec],
        out_specs=[qspec, cspec, cspec],
        out_shape=[
            jax.ShapeDtypeStruct((nbatch, A_HEADS, A_HEAD_DIM), F32),
            jax.ShapeDtypeStruct((nbatch, WINDOW, A_KV), F32),
            jax.ShapeDtypeStruct((nbatch, WINDOW, A_KV), F32),
        ],
        compiler_params=_params(("parallel",)),
        name="swa_decode",
    )(sinks, q3, kn, vn, ck, cv)
    shape5 = (nbatch, WINDOW, A_KV_HEADS, A_HEAD_DIM)
    return o.reshape(nbatch, A_Q), nk.reshape(shape5), nv.reshape(shape5)


def _cols(rows, heads):
    nbatch = rows.shape[0]
    x = rows.reshape(nbatch // DEC_BLOCK, DEC_BLOCK, heads, LANES)
    return x.transpose(2, 0, 3, 1)


def _hgrn_dec_kernel(qt_ref, ft_ref, lb_ref, v_ref, g_ref, ng_ref, s_ref, o_ref, so_ref, orow_ref):
    lb = lb_ref[...]
    qc = _silu(qt_ref[...])
    f = lb + (1.0 - lb) * jax.nn.sigmoid(ft_ref[...])
    kk = 1.0 - f
    v = v_ref[...]
    for bb in range(DEC_BLOCK):
        s_new = f[:, bb:bb + 1] * s_ref[bb] + kk[:, bb:bb + 1] * v[bb:bb + 1, :]
        so_ref[bb] = s_new
        orow_ref[bb:bb + 1, :] = jnp.sum(qc[:, bb:bb + 1] * s_new, axis=0, keepdims=True)
    o_ref[...] = _rms_gate(orow_ref[...], ng_ref[...], g_ref[...])


def _hgrn_decode(proj2, state, lower, norm_g):
    nbatch = proj2.shape[0]
    nh = B_HEADS
    base = A_Q + 2 * A_KV
    width = nh * B_DIM
    qt = _cols(proj2[:, base:base + width], nh)
    ft = _cols(proj2[:, base + width:base + 2 * width], nh)
    colspec = pl.BlockSpec((None, None, B_DIM, DEC_BLOCK), lambda h, j: (h, j, 0, 0))
    sspec = pl.BlockSpec((DEC_BLOCK, None, B_DIM, B_DIM), lambda h, j: (j, h, 0, 0))
    cb = base // LANES
    o, s_new = pl.pallas_call(
        _hgrn_dec_kernel,
        grid=(nh, nbatch // DEC_BLOCK),
        in_specs=[
            colspec, colspec,
            pl.BlockSpec((None, B_DIM, 1), lambda h, j: (h, 0, 0)),
            pl.BlockSpec((DEC_BLOCK, B_DIM), lambda h, j: (j, cb + 2 * nh + h)),
            pl.BlockSpec((DEC_BLOCK, B_DIM), lambda h, j: (j, cb + 3 * nh + h)),
            pl.BlockSpec((1, B_DIM), lambda h, j: (0, 0)),
            sspec,
        ],
        out_specs=[pl.BlockSpec((DEC_BLOCK, B_DIM), lambda h, j: (j, h)), sspec],
        out_shape=[
            jax.ShapeDtypeStruct((nbatch, width), F32),
            jax.ShapeDtypeStruct(state.shape, F32),
        ],
        scratch_shapes=[pltpu.VMEM((DEC_BLOCK, B_DIM), F32)],
        compiler_params=_params(("parallel", "parallel")),
        name="hgrn_decode",
    )(qt, ft, lower.reshape(nh, B_DIM, 1), proj2, proj2, norm_g.reshape(1, B_DIM), state)
    return o, s_new


def _gdn_prep_kernel(p_ref, h0_ref, h1_ref, h2_ref, cw_ref, alog_ref, dtb_ref, q_ref, k_ref, v_ref, ab_ref):
    acc = (h0_ref[...] * cw_ref[0:1, :] + h1_ref[...] * cw_ref[1:2, :] + h2_ref[...] * cw_ref[2:3, :]
           + p_ref[:, 0:C_CONV_DIM] * cw_ref[3:4, :])
    qkv = _silu(acc)
    for h in range(C_HEADS):
        hs = slice(h * C_DIM, (h + 1) * C_DIM)
        qh = qkv[:, h * C_DIM:(h + 1) * C_DIM]
        kh = qkv[:, C_QK + h * C_DIM:C_QK + (h + 1) * C_DIM]
        q_ref[:, hs] = qh * lax.rsqrt(jnp.sum(qh * qh, axis=-1, keepdims=True) + NORM_EPS) * (C_DIM ** -0.5)
        k_ref[:, hs] = kh * lax.rsqrt(jnp.sum(kh * kh, axis=-1, keepdims=True) + NORM_EPS)
    v_ref[...] = qkv[:, 2 * C_QK:3 * C_QK]
    ab = p_ref[:, C_CONV_DIM + C_QK:C_CONV_DIM + C_QK + LANES]
    alpha = jnp.exp(-jnp.exp(alog_ref[...]) * _softplus(ab + dtb_ref[...]))
    beta = jax.nn.sigmoid(ab)
    lane = lax.broadcasted_iota(jnp.int32, ab.shape, 1)
    ab_ref[...] = jnp.where(lane < C_HEADS, alpha, beta)


def _gdn_dec_kernel(qt_ref, kt_ref, v_ref, gate_ref, al_ref, be_ref, ng_ref, s_ref, o_ref, so_ref, orow_ref):
    qt = qt_ref[...]
    kt = kt_ref[...]
    v = v_ref[...]
    al = al_ref[...]
    be = be_ref[...]
    for bb in range(DEC_BLOCK):
        s = s_ref[bb]
        kc = kt[:, bb:bb + 1]
        a = al[bb:bb + 1, :]
        ks = jnp.sum(kc * s, axis=0, keepdims=True)
        delta = be[bb:bb + 1, :] * (v[bb:bb + 1, :] - a * ks)
        s_new = a * s + kc * delta
        so_ref[bb] = s_new
        orow_ref[bb:bb + 1, :] = jnp.sum(qt[:, bb:bb + 1] * s_new, axis=0, keepdims=True)
    o_ref[...] = _rms_gate(orow_ref[...], ng_ref[...], gate_ref[...])


def _gdn_decode(proj2, hist, state, conv_w, alog_row, dtb_row, norm_g):
    nbatch = proj2.shape[0]
    nh = C_HEADS
    full = lambda shape: pl.BlockSpec(shape, lambda: tuple(0 for _ in shape))
    hspec = full((nbatch, C_CONV_DIM))
    qn, kn, vc, ab = pl.pallas_call(
        _gdn_prep_kernel,
        grid=(),
        in_specs=[
            full((nbatch, ODD_IN_PAD)), hspec, hspec, hspec,
            full((C_CONV, C_CONV_DIM)), full((1, LANES)), full((1, LANES)),
        ],
        out_specs=[full((nbatch, C_QK)), full((nbatch, C_QK)), full((nbatch, C_QK)), full((nbatch, LANES))],
        out_shape=[
            jax.ShapeDtypeStruct((nbatch, C_QK), F32),
            jax.ShapeDtypeStruct((nbatch, C_QK), F32),
            jax.ShapeDtypeStruct((nbatch, C_QK), F32),
            jax.ShapeDtypeStruct((nbatch, LANES), F32),
        ],
        compiler_params=pltpu.CompilerParams(vmem_limit_bytes=VMEM_LIMIT),
        name="gdn_decode_prep",
    )(proj2, hist[:, 0], hist[:, 1], hist[:, 2], conv_w, alog_row, dtb_row)
    qt = _cols(qn, nh)
    kt = _cols(kn, nh)
    al = jnp.broadcast_to(ab[:, :nh].T[:, :, None], (nh, nbatch, LANES))
    be = jnp.broadcast_to(ab[:, nh:2 * nh].T[:, :, None], (nh, nbatch, LANES))
    colspec = pl.BlockSpec((None, None, C_DIM, DEC_BLOCK), lambda h, j: (h, j, 0, 0))
    sspec = pl.BlockSpec((DEC_BLOCK, None, C_DIM, C_DIM), lambda h, j: (j, h, 0, 0))
    rspec = pl.BlockSpec((None, DEC_BLOCK, LANES), lambda h, j: (h, j, 0))
    gcol = C_CONV_DIM // LANES
    o, s_new = pl.pallas_call(
        _gdn_dec_kernel,
        grid=(nh, nbatch // DEC_BLOCK),
        in_specs=[
            colspec, colspec,
            pl.BlockSpec((DEC_BLOCK, C_DIM), lambda h, j: (j, h)),
            pl.BlockSpec((DEC_BLOCK, C_DIM), lambda h, j: (j, gcol + h)),
            rspec, rspec,
            pl.BlockSpec((1, C_DIM), lambda h, j: (0, 0)),
            sspec,
        ],
        out_specs=[pl.BlockSpec((DEC_BLOCK, C_DIM), lambda h, j: (j, h)), sspec],
        out_shape=[
            jax.ShapeDtypeStruct((nbatch, C_QK), F32),
            jax.ShapeDtypeStruct(state.shape, F32),
        ],
        scratch_shapes=[pltpu.VMEM((DEC_BLOCK, C_DIM), F32)],
        compiler_params=_params(("parallel", "parallel")),
        name="gdn_decode",
    )(qt, kt, vc, proj2, al, be, norm_g.reshape(1, C_DIM), state)
    return o, s_new


def _trunk(x, mods, caches, p, tm):
    decode = caches is not None
    m = mods[0]
    x = _ffn(x, m, 0, p["w_up"][0, 0], p["w_down"][0, 0], p["ln_g"][0, 0], p["ln_b"][0, 0], tm)
    proj = _inproj(x, m, 3, p["even_in"], tm)
    if decode:
        o_a, new_k, new_v = _swa_decode(proj[0], caches[0][0], caches[1][0], p["sinks"])
        o_b, s_hgrn = _hgrn_decode(proj[0], caches[2][0], p["lower"], p["hgrn_norm_g"])
        o_a, o_b = o_a[None], o_b[None]
    else:
        bsz = x.shape[0]
        o_a = _swa_prompt(proj, p["sinks"])
        new_k = proj[:, -WINDOW:, A_Q:A_Q + A_KV].reshape(bsz, WINDOW, A_KV_HEADS, A_HEAD_DIM)
        new_v = proj[:, -WINDOW:, A_Q + A_KV:A_Q + 2 * A_KV].reshape(bsz, WINDOW, A_KV_HEADS, A_HEAD_DIM)
        o_b, s_hgrn = _hgrn_prompt(proj, p["lower"], p["hgrn_norm_g"])
    x = _outproj(x, o_a, 0, o_b, 0, m, 5, p["even_out"], p["ln_g"][0, 1], p["ln_b"][0, 1], tm)
    x = _ffn(x, m, 6, p["w_up"][0, 1], p["w_down"][0, 1], p["ln_g"][0, 2], p["ln_b"][0, 2], tm)
    m = mods[1]
    x = _ffn(x, m, 0, p["w_up"][1, 0], p["w_down"][1, 0], p["ln_g"][1, 0], p["ln_b"][1, 0], tm)
    proj = _inproj(x, m, 3, p["odd_in"], tm)
    if decode:
        hist = caches[4][0]
        o_c, s_gdn = _gdn_decode(proj[0], hist, caches[3][0], p["conv_w"], p["alog_row"], p["dtb_row"],
                                 p["gdn_norm_g"])
        o_c = o_c[None]
        new_hist = jnp.concatenate([hist[:, 1:], proj[0][:, None, :C_CONV_DIM]], axis=1)
    else:
        o_c, s_gdn = _gdn_prompt(proj, p["conv_w"], p["alog_row"], p["dtb_row"], p["gdn_norm_g"])
        new_hist = proj[:, -(C_CONV - 1):, :C_CONV_DIM]
    x = _outproj(x, o_c, 0, o_c, 1, m, 5, p["odd_out"], p["ln_g"][1, 1], p["ln_b"][1, 1], tm)
    x = _ffn(x, m, 6, p["w_up"][1, 1], p["w_down"][1, 1], p["ln_g"][1, 2], p["ln_b"][1, 2], tm)
    return x, new_k[None], new_v[None], s_hgrn[None], s_gdn[None], new_hist[None]


def kernel(x_prompt, x_sample, cache_swa_k, cache_swa_v, state_hgrn, state_gdn, state_gdn_conv, c_prompt, c_sample, ada_w, ada_b, ln_g, ln_b, ffn_w_up, ffn_w_down, even_w_in, even_w_out, swa_sinks, hgrn_norm_g, hgrn_lb_logits, odd_w_in, odd_w_out, gdn_conv_w, gdn_a_log, gdn_dt_bias, gdn_norm_g):
    n_prompt = c_prompt.shape[0]
    n_sample = c_sample.shape[0]
    pad_rows = (-(n_prompt + n_sample)) % 8
    c_all = jnp.concatenate([c_prompt, c_sample, jnp.zeros((pad_rows, D_MODEL), F32)], axis=0)
    mods = _ada_mods(c_all, ada_w, ada_b)
    mods_p = mods[:, :n_prompt].reshape(DEPTH, n_prompt, 1, N_MOD * D_MODEL)
    mods_s = mods[:, n_prompt:n_prompt + n_sample].reshape(DEPTH, 1, n_sample, N_MOD * D_MODEL)

    probs = jax.nn.softmax(hgrn_lb_logits.astype(F32), axis=0)
    lower = (jnp.cumsum(probs, axis=0)[1:] - probs[0])[0]

    def lane_row(v):
        return jnp.pad(v.astype(F32), (0, LANES - v.shape[0])).reshape(1, LANES)

    p = dict(
        w_up=ffn_w_up.astype(BF16), w_down=ffn_w_down.astype(BF16),
        ln_g=ln_g, ln_b=ln_b,
        even_in=even_w_in[0].astype(BF16), even_out=even_w_out[0].astype(BF16),
        odd_in=jnp.pad(odd_w_in[0], ((0, 0), (0, ODD_IN_PAD - ODD_IN))).astype(BF16),
        odd_out=odd_w_out[0].astype(BF16),
        sinks=swa_sinks[0], lower=lower, hgrn_norm_g=hgrn_norm_g[0],
        conv_w=gdn_conv_w[0], alog_row=lane_row(gdn_a_log[0]), dtb_row=lane_row(gdn_dt_bias[0]),
        gdn_norm_g=gdn_norm_g[0],
    )
    y_p, p_k, p_v, p_hgrn, p_gdn, p_conv = _trunk(x_prompt, mods_p, None, p, 512)
    caches = (cache_swa_k, cache_swa_v, state_hgrn, state_gdn, state_gdn_conv)
    x_s = x_sample.reshape(1, n_sample, D_MODEL)
    y_s, s_k, s_v, s_hgrn, s_gdn, s_conv = _trunk(x_s, mods_s, caches, p, n_sample)
    y_s = y_s.reshape(n_sample, 1, D_MODEL)
    return (y_p, y_s, p_k, p_v, p_hgrn, p_gdn, p_conv, s_k, s_v, s_hgrn, s_gdn, s_conv)
```

```python
import functools

import jax
import jax.numpy as jnp
from jax import lax
from jax.experimental import pallas as pl
from jax.experimental.pallas import tpu as pltpu

F32 = jnp.float32
BF16 = jnp.bfloat16
HIGHEST = lax.Precision.HIGHEST

D_MODEL = 1024
DEPTH = 2
WINDOW = 128
A_HEADS = 8
A_KV_HEADS = 2
A_GROUP = A_HEADS // A_KV_HEADS
A_HEAD_DIM = 64
A_Q = A_HEADS * A_HEAD_DIM
A_KV = A_KV_HEADS * A_HEAD_DIM
B_HEADS = 4
B_DIM = 128
EVEN_IN = 2816
C_HEADS = 8
C_DIM = 128
C_QK = C_HEADS * C_DIM
C_CONV = 4
C_CONV_DIM = 3 * C_QK
ODD_IN = C_CONV_DIM + C_QK + 2 * C_HEADS
ODD_IN_PAD = 4224
D_FF = 2816
N_MOD = 9
DN_ALPHA = (2 * DEPTH) ** 0.25
LN_EPS = 1e-5
NORM_EPS = 1e-6
LANES = 128

FF_CHUNK = 256
FFN_TM = 1024
FFN_VMEM_LIMIT = 56 << 20
PROJ_TILE = 1408
HGRN_BLOCK = 128
HGRN_SUB = 8
GDN_CHUNK_LOG = 7
GDN_CHUNK = 1 << GDN_CHUNK_LOG
GDN_BASE_LOG = 3
DEC_BLOCK = 64
VMEM_LIMIT = 48 << 20


def _params(sem, vmem=VMEM_LIMIT):
    return pltpu.CompilerParams(dimension_semantics=sem, vmem_limit_bytes=vmem)


def _silu(x):
    return x * jax.nn.sigmoid(x)


def _softplus(x):
    return jnp.maximum(x, 0.0) + jnp.log(1.0 + jnp.exp(-jnp.abs(x)))


def _layer_norm(y, g, b):
    mu = jnp.mean(y, axis=-1, keepdims=True)
    d = y - mu
    var = jnp.mean(d * d, axis=-1, keepdims=True)
    return d * lax.rsqrt(var + LN_EPS) * g + b


def _rms_gate(o, norm_g, gate):
    y = o * lax.rsqrt(jnp.mean(o * o, axis=-1, keepdims=True) + NORM_EPS)
    return y * norm_g * _silu(gate)


def _dot(a, b, precision=None):
    return jnp.dot(a, b, preferred_element_type=F32, precision=precision)


def _dot_inv(a, b):
    return _dot(a.astype(BF16), b.astype(BF16))


def _dot_nt(a, b, precision=None):
    return lax.dot_general(a, b, (((1,), (1,)), ((), ())), preferred_element_type=F32, precision=precision)


def _ada_kernel(c_ref, w_ref, b_ref, o_ref):
    cs = _silu(c_ref[...]).astype(BF16)
    o_ref[...] = _dot(cs, w_ref[...].astype(BF16)) + b_ref[...]


def _ada_mods(c_all, ada_w, ada_b):
    m = c_all.shape[0]
    n = N_MOD * D_MODEL
    tn = 1152
    return pl.pallas_call(
        _ada_kernel,
        grid=(DEPTH, n // tn),
        in_specs=[
            pl.BlockSpec((m, D_MODEL), lambda l, j: (0, 0)),
            pl.BlockSpec((None, D_MODEL, tn), lambda l, j: (l, 0, j)),
            pl.BlockSpec((None, 1, tn), lambda l, j: (l, 0, j)),
        ],
        out_specs=pl.BlockSpec((None, m, tn), lambda l, j: (l, 0, j)),
        out_shape=jax.ShapeDtypeStruct((DEPTH, m, n), F32),
        compiler_params=_params(("parallel", "parallel")),
        name="ada_mods",
    )(c_all, ada_w, ada_b.reshape(DEPTH, 1, n))


def _mod_spec(mods, k, tm, grid_rank):
    per_token = mods.shape[1] != 1
    rows = tm if per_token else 1
    if grid_rank == 3:
        return pl.BlockSpec((None, rows, D_MODEL), lambda b, i, j: (b, i if per_token else 0, k))
    return pl.BlockSpec((None, rows, D_MODEL), lambda b, i: (b, i if per_token else 0, k))


def _ffn_kernel(x_ref, sh_ref, sc_ref, g_ref, wu_ref, wd_ref, lg_ref, lb_ref, o_ref, acc_ref):
    x = x_ref[...]
    h = (x * (1.0 + sc_ref[...]) + sh_ref[...]).astype(BF16)
    for c in range(D_FF // FF_CHUNK):
        lo = c * FF_CHUNK
        gate = _dot(h, wu_ref[:, lo:lo + FF_CHUNK])
        up = _dot(h, wu_ref[:, D_FF + lo:D_FF + lo + FF_CHUNK])
        act = (_silu(gate) * up).astype(BF16)
        part = _dot(act, wd_ref[lo:lo + FF_CHUNK, :])
        if c == 0:
            acc_ref[...] = part
        else:
            acc_ref[...] += part
    y = DN_ALPHA * x + (0.5 * g_ref[...]) * acc_ref[...]
    o_ref[...] = _layer_norm(y, lg_ref[...], lb_ref[...])


def _ffn(x, mods, k0, w_up, w_down, ln_g, ln_b, tm):
    bsz, t, _ = x.shape
    tm = min(t, max(tm, FFN_TM))
    row = pl.BlockSpec((None, tm, D_MODEL), lambda b, i: (b, i, 0))
    vec = pl.BlockSpec((1, D_MODEL), lambda b, i: (0, 0))
    resident = pl.Buffered(1)
    return pl.pallas_call(
        _ffn_kernel,
        grid=(bsz, t // tm),
        in_specs=[
            row,
            _mod_spec(mods, k0, tm, 2), _mod_spec(mods, k0 + 1, tm, 2), _mod_spec(mods, k0 + 2, tm, 2),
            pl.BlockSpec((D_MODEL, 2 * D_FF), lambda b, i: (0, 0), pipeline_mode=resident),
            pl.BlockSpec((D_FF, D_MODEL), lambda b, i: (0, 0), pipeline_mode=resident),
            vec, vec,
        ],
        out_specs=row,
        out_shape=jax.ShapeDtypeStruct(x.shape, F32),
        scratch_shapes=[pltpu.VMEM((tm, D_MODEL), F32)],
        compiler_params=_params(("parallel", "parallel"), FFN_VMEM_LIMIT),
        name="ffn",
    )(x, mods, mods, mods, w_up, w_down, ln_g.reshape(1, D_MODEL), ln_b.reshape(1, D_MODEL))


def _inproj_kernel(x_ref, sh_ref, sc_ref, w_ref, o_ref, h_ref):
    @pl.when(pl.program_id(2) == 0)
    def _():
        h_ref[...] = (x_ref[...] * (1.0 + sc_ref[...]) + sh_ref[...]).astype(BF16)

    o_ref[...] = _dot(h_ref[...], w_ref[...])


def _inproj(x, mods, k0, w, tm):
    bsz, t, _ = x.shape
    n = w.shape[1]
    return pl.pallas_call(
        _inproj_kernel,
        grid=(bsz, t // tm, n // PROJ_TILE),
        in_specs=[
            pl.BlockSpec((None, tm, D_MODEL), lambda b, i, j: (b, i, 0)),
            _mod_spec(mods, k0, tm, 3), _mod_spec(mods, k0 + 1, tm, 3),
            pl.BlockSpec((D_MODEL, PROJ_TILE), lambda b, i, j: (0, j)),
        ],
        out_specs=pl.BlockSpec((None, tm, PROJ_TILE), lambda b, i, j: (b, i, j)),
        out_shape=jax.ShapeDtypeStruct((bsz, t, n), F32),
        scratch_shapes=[pltpu.VMEM((tm, D_MODEL), BF16)],
        compiler_params=_params(("parallel", "parallel", "arbitrary")),
        name="inproj",
    )(x, mods, mods, w)


def _outproj_kernel(x_ref, o1_ref, o2_ref, g_ref, w1_ref, w2_ref, lg_ref, lb_ref, y_ref):
    mix = _dot(o1_ref[...].astype(BF16), w1_ref[...]) + _dot(o2_ref[...].astype(BF16), w2_ref[...])
    y = DN_ALPHA * x_ref[...] + g_ref[...] * mix
    y_ref[...] = _layer_norm(y, lg_ref[...], lb_ref[...])


def _outproj(x, o1, c1, o2, c2, mods, kg, w_out, ln_g, ln_b, tm):
    bsz, t, _ = x.shape
    half = D_MODEL // 2
    row = pl.BlockSpec((None, tm, D_MODEL), lambda b, i: (b, i, 0))
    vec = pl.BlockSpec((1, D_MODEL), lambda b, i: (0, 0))
    return pl.pallas_call(
        _outproj_kernel,
        grid=(bsz, t // tm),
        in_specs=[
            row,
            pl.BlockSpec((None, tm, half), lambda b, i: (b, i, c1)),
            pl.BlockSpec((None, tm, half), lambda b, i: (b, i, c2)),
            _mod_spec(mods, kg, tm, 2),
            pl.BlockSpec((half, D_MODEL), lambda b, i: (0, 0)),
            pl.BlockSpec((half, D_MODEL), lambda b, i: (1, 0)),
            vec, vec,
        ],
        out_specs=row,
        out_shape=jax.ShapeDtypeStruct(x.shape, F32),
        compiler_params=_params(("parallel", "parallel")),
        name="outproj",
    )(x, o1, o2, mods, w_out, w_out, ln_g.reshape(1, D_MODEL), ln_b.reshape(1, D_MODEL))


def _swa_kernel(sink_ref, q_ref, kc_ref, kp_ref, vc_ref, vp_ref, o_ref):
    i = pl.program_id(1)
    q = q_ref[...]
    kc = kc_ref[...].astype(BF16)
    kp = kp_ref[...].astype(BF16)
    vc = vc_ref[...].astype(BF16)
    vp = vp_ref[...].astype(BF16)
    r = lax.broadcasted_iota(jnp.int32, (WINDOW, WINDOW), 0)
    c = lax.broadcasted_iota(jnp.int32, (WINDOW, WINDOW), 1)
    dist_c = (r - c).astype(F32)
    dist_p = dist_c + float(WINDOW)
    valid_c = c <= r
    valid_p = c > r + jnp.where(i > 0, 0, WINDOW)
    scale = A_HEAD_DIM ** -0.5
    heads = range(A_HEADS)
    hsl = [slice(h * A_HEAD_DIM, (h + 1) * A_HEAD_DIM) for h in heads]
    ksl = [slice((h // A_GROUP) * A_HEAD_DIM, (h // A_GROUP + 1) * A_HEAD_DIM) for h in heads]
    slope = [2.0 ** (-8.0 * (h + 1) / A_HEADS) for h in heads]
    qh = [q[:, hsl[h]].astype(BF16) for h in heads]
    s_c = [_dot_nt(qh[h], kc[:, ksl[h]]) for h in heads]
    s_p = [_dot_nt(qh[h], kp[:, ksl[h]]) for h in heads]
    s_c = [jnp.where(valid_c, s_c[h] * scale - slope[h] * dist_c, -jnp.inf) for h in heads]
    s_p = [jnp.where(valid_p, s_p[h] * scale - slope[h] * dist_p, -jnp.inf) for h in heads]
    m = [jnp.maximum(jnp.maximum(jnp.max(s_c[h], axis=-1, keepdims=True), jnp.max(s_p[h], axis=-1, keepdims=True)),
                     sink_ref[h]) for h in heads]
    p_c = [jnp.exp(s_c[h] - m[h]) for h in heads]
    p_p = [jnp.exp(s_p[h] - m[h]) for h in heads]
    den = [jnp.sum(p_c[h], axis=-1, keepdims=True) + jnp.sum(p_p[h], axis=-1, keepdims=True)
           + jnp.exp(sink_ref[h] - m[h]) for h in heads]
    o = [_dot(p_c[h].astype(BF16), vc[:, ksl[h]]) + _dot(p_p[h].astype(BF16), vp[:, ksl[h]]) for h in heads]
    for h in heads:
        o_ref[:, hsl[h]] = o[h] / den[h]


def _swa_prompt(proj, sinks):
    bsz, t, _ = proj.shape
    nb = t // WINDOW
    kcol = A_Q // LANES
    vcol = kcol + 1

    def cur(col):
        return pl.BlockSpec((None, WINDOW, LANES), lambda b, i: (b, i, col))

    def prev(col):
        return pl.BlockSpec((None, WINDOW, LANES), lambda b, i: (b, jnp.maximum(i - 1, 0), col))

    return pl.pallas_call(
        _swa_kernel,
        grid=(bsz, nb),
        in_specs=[
            pl.BlockSpec(memory_space=pltpu.SMEM),
            pl.BlockSpec((None, WINDOW, A_Q), lambda b, i: (b, i, 0)),
            cur(kcol), prev(kcol), cur(vcol), prev(vcol),
        ],
        out_specs=pl.BlockSpec((None, WINDOW, A_Q), lambda b, i: (b, i, 0)),
        out_shape=jax.ShapeDtypeStruct((bsz, t, A_Q), F32),
        compiler_params=_params(("parallel", "parallel")),
        name="swa_prompt",
    )(sinks, proj, proj, proj, proj, proj)


def _split3(x):
    hi = x.astype(BF16)
    r1 = x - hi.astype(F32)
    mid = r1.astype(BF16)
    lo = (r1 - mid.astype(F32)).astype(BF16)
    return hi, mid, lo


def _hgrn_kernel(q0_ref, q1_ref, f0_ref, f1_ref, v0_ref, v1_ref, g0_ref, g1_ref, lb_ref, ng_ref,
                 o_ref, s_ref, st_ref):
    i = pl.program_id(1)

    @pl.when(i == 0)
    def _():
        st_ref[...] = jnp.zeros_like(st_ref)

    n = HGRN_BLOCK
    nsub = n // HGRN_SUB
    heads = range(B_HEADS)
    lb = lb_ref[...]
    qb = _silu(jnp.concatenate([q0_ref[...], q1_ref[...]], axis=1))
    f = lb + (1.0 - lb) * jax.nn.sigmoid(jnp.concatenate([f0_ref[...], f1_ref[...]], axis=1))
    lf = jnp.log(f)
    kk = 1.0 - f
    v = jnp.concatenate([v0_ref[...], v1_ref[...]], axis=1)
    r = lax.broadcasted_iota(jnp.int32, (n, n), 0)
    c = lax.broadcasted_iota(jnp.int32, (n, n), 1)
    tri = jnp.where((c <= r) & (c >= r - (r & (HGRN_SUB - 1))), 1.0, 0.0).astype(BF16)
    lf_hi, lf_mid, lf_lo = _split3(lf)
    a_all = _dot(tri, lf_hi) + (_dot(tri, lf_mid) + _dot(tri, lf_lo))
    a3 = a_all.reshape(nsub, HGRN_SUB, B_HEADS * B_DIM)
    a_last3 = a3[:, HGRN_SUB - 1:HGRN_SUB, :]
    qb3 = qb.reshape(a3.shape)
    kk3 = kk.reshape(a3.shape)
    v3 = v.reshape(a3.shape)
    qe3 = qb3 * jnp.exp(a3)
    kd3 = kk3 * jnp.exp(a_last3 - a3)
    dec3 = jnp.exp(a_last3)
    trow = lax.broadcasted_iota(jnp.int32, (1, HGRN_SUB, 1), 1)
    o_diag = []
    v_t = []
    for h in heads:
        hs = slice(h * B_DIM, (h + 1) * B_DIM)
        a_h, q_h, k_h, v_h = a3[:, :, hs], qb3[:, :, hs], kk3[:, :, hs], v3[:, :, hs]
        acc = jnp.zeros((nsub, HGRN_SUB, B_DIM), F32)
        for s in range(HGRN_SUB):
            e = jnp.exp(jnp.where(trow >= s, a_h - a_h[:, s:s + 1, :], -jnp.inf))
            col = jnp.sum(q_h * e * k_h[:, s:s + 1, :], axis=-1, keepdims=True)
            acc = acc + col * v_h[:, s:s + 1, :]
        o_diag.append(acc.reshape(n, B_DIM))
        v_t.append(v[:, hs].T.astype(BF16))
    state = [st_ref[h] for h in heads]
    o_state = [[] for _ in heads]
    for ch in range(nsub):
        sl = slice(ch * HGRN_SUB, (ch + 1) * HGRN_SUB)
        for h in heads:
            hs = slice(h * B_DIM, (h + 1) * B_DIM)
            o_state[h].append(_dot_nt(qe3[ch, :, hs].astype(BF16), state[h].astype(BF16)))
            state[h] = state[h] * dec3[ch, :, hs] + _dot(v_t[h][:, sl], kd3[ch, :, hs].astype(BF16))
    g = jnp.concatenate([g0_ref[...], g1_ref[...]], axis=1)
    for h in heads:
        hs = slice(h * B_DIM, (h + 1) * B_DIM)
        st_ref[h] = state[h]
        o = jnp.concatenate(o_state[h], axis=0) + o_diag[h]
        o_ref[:, hs] = _rms_gate(o, ng_ref[...], g[:, hs])

    @pl.when(i == pl.num_programs(1) - 1)
    def _():
        for h in heads:
            s_ref[h] = state[h].T


def _hgrn_prompt(proj, lower, norm_g):
    bsz, t, _ = proj.shape
    nh = B_HEADS
    width = nh * B_DIM
    half = width // 2
    base = (A_Q + 2 * A_KV) // half

    def col(off):
        return pl.BlockSpec((None, HGRN_BLOCK, half), lambda b, i: (b, i, base + off))

    return pl.pallas_call(
        _hgrn_kernel,
        grid=(bsz, t // HGRN_BLOCK),
        in_specs=[
            col(0), col(1), col(2), col(3), col(4), col(5), col(6), col(7),
            pl.BlockSpec((1, width), lambda b, i: (0, 0)),
            pl.BlockSpec((1, B_DIM), lambda b, i: (0, 0)),
        ],
        out_specs=[
            pl.BlockSpec((None, HGRN_BLOCK, width), lambda b, i: (b, i, 0)),
            pl.BlockSpec((None, nh, B_DIM, B_DIM), lambda b, i: (b, 0, 0, 0)),
        ],
        out_shape=[
            jax.ShapeDtypeStruct((bsz, t, width), F32),
            jax.ShapeDtypeStruct((bsz, nh, B_DIM, B_DIM), F32),
        ],
        scratch_shapes=[pltpu.VMEM((nh, B_DIM, B_DIM), F32)],
        compiler_params=_params(("parallel", "arbitrary")),
        name="hgrn_prompt",
    )(proj, proj, proj, proj, proj, proj, proj, proj, lower.reshape(1, width), norm_g.reshape(1, B_DIM))


def _gdn_kernel(q_ref, k_ref, v_ref, gate_ref, ab_ref, cw_ref, alog_ref, dtb_ref, ng_ref,
                o_ref, s_out_ref, s_ref, xs_ref):
    i = pl.program_id(1)
    n = GDN_CHUNK

    @pl.when(i == 0)
    def _():
        s_ref[...] = jnp.zeros_like(s_ref)
        xs_ref[0:8, :] = jnp.zeros((8, C_CONV_DIM), F32)

    @pl.when(i > 0)
    def _():
        xs_ref[0:8, :] = xs_ref[n:n + 8, :]

    xs_ref[8:n + 8, 0:C_QK] = q_ref[...]
    xs_ref[8:n + 8, C_QK:2 * C_QK] = k_ref[...]
    xs_ref[8:n + 8, 2 * C_QK:3 * C_QK] = v_ref[...]
    off = 8 - (C_CONV - 1)
    acc = xs_ref[off:off + n, :] * cw_ref[0:1, :]
    for j in range(1, C_CONV):
        acc = acc + xs_ref[off + j:off + j + n, :] * cw_ref[j:j + 1, :]
    qkv = _silu(acc)

    ab = ab_ref[...]
    log_alpha = -jnp.exp(alog_ref[...]) * _softplus(ab + dtb_ref[...])
    beta_all = jax.nn.sigmoid(ab)
    r = lax.broadcasted_iota(jnp.int32, (n, n), 0)
    c = lax.broadcasted_iota(jnp.int32, (n, n), 1)
    causal = c <= r
    strict = c < r
    tri = jnp.where(causal, 1.0, 0.0).astype(F32)
    eye = jnp.where(r == c, 1.0, 0.0).astype(F32)
    base_mask = (r >> GDN_BASE_LOG) == (c >> GDN_BASE_LOG)
    level_masks = [((r >> (k + 1)) == (c >> (k + 1))) & ((r >> k) != (c >> k))
                   for k in range(GDN_BASE_LOG, GDN_CHUNK_LOG)]
    g_cum =_dot(tri, log_alpha, HIGHEST)
    g_cum_t = g_cum.T

    heads = range(C_HEADS)
    qn, kn, kn_b, vh, g_col, beta, dec_incl, e_g = [], [], [], [], [], [], [], []
    for h in heads:
        q_h = qkv[:, h * C_DIM:(h + 1) * C_DIM]
        k_h = qkv[:, C_QK + h * C_DIM:C_QK + (h + 1) * C_DIM]
        vh.append(qkv[:, 2 * C_QK + h * C_DIM:2 * C_QK + (h + 1) * C_DIM])
        qn.append(q_h * lax.rsqrt(jnp.sum(q_h * q_h, axis=-1, keepdims=True) + NORM_EPS) * (C_DIM ** -0.5))
        kn.append(k_h * lax.rsqrt(jnp.sum(k_h * k_h, axis=-1, keepdims=True) + NORM_EPS))
        kn_b.append(kn[h].astype(BF16))
        g_col.append(g_cum[:, h:h + 1])
        beta.append(beta_all[:, C_HEADS + h:C_HEADS + h + 1])
        dec_incl.append(jnp.exp(jnp.where(causal, g_col[h] - g_cum_t[h:h + 1, :], -jnp.inf)))
        e_g.append(jnp.exp(g_col[h]))
    kkt = [_dot_nt(kn_b[h], kn_b[h]) for h in heads]
    qk = [_dot_nt(qn[h].astype(BF16), kn_b[h]) for h in heads]
    l_mat = [beta[h] * kkt[h] * jnp.where(strict, dec_incl[h], 0.0) for h in heads]
    a_pow = [jnp.where(base_mask, -l_mat[h], 0.0) for h in heads]
    t_inv = [eye + a_pow[h] for h in heads]
    for _ in range(GDN_BASE_LOG - 1):
        a_pow = [_dot_inv(a_pow[h], a_pow[h]) for h in heads]
        t_inv = [t_inv[h] + _dot_inv(t_inv[h], a_pow[h]) for h in heads]
    for lm in level_masks:
        x = [_dot_inv(jnp.where(lm, l_mat[h], 0.0), t_inv[h]) for h in heads]
        t_inv = [t_inv[h] - _dot_inv(t_inv[h], x[h]) for h in heads]
    uw = [_dot_inv(t_inv[h], jnp.concatenate([beta[h] * vh[h], (beta[h] * e_g[h]) * kn[h]], axis=1))
          for h in heads]
    s_old = [s_ref[h] for h in heads]
    s_b = [s_old[h].astype(BF16) for h in heads]
    ws = [_dot(uw[h][:, C_DIM:].astype(BF16), s_b[h]) for h in heads]
    qs = [_dot((qn[h] * e_g[h]).astype(BF16), s_b[h]) for h in heads]
    delta_b = [(uw[h][:, :C_DIM] - ws[h]).astype(BF16) for h in heads]
    o = [qs[h] + _dot((qk[h] * dec_incl[h]).astype(BF16), delta_b[h]) for h in heads]
    for h in heads:
        g_last = g_col[h][n - 1:n]
        k_dec = kn[h] * jnp.exp(g_last - g_col[h])
        s_ref[h] = jnp.exp(g_last) * s_old[h] + _dot(k_dec.T.astype(BF16), delta_b[h])
    for h in heads:
        hs = slice(h * C_DIM, (h + 1) * C_DIM)
        o_ref[:, hs] = _rms_gate(o[h], ng_ref[...], gate_ref[:, hs])

    @pl.when(i == pl.num_programs(1) - 1)
    def _():
        s_out_ref[...] = s_ref[...]


def _gdn_prompt(proj, conv_w, alog_row, dtb_row, norm_g):
    bsz, t, _ = proj.shape
    n = GDN_CHUNK

    def col(cidx):
        return pl.BlockSpec((None, n, C_QK), lambda b, i: (b, i, cidx))

    one = pl.BlockSpec((1, LANES), lambda b, i: (0, 0))
    return pl.pallas_call(
        _gdn_kernel,
        grid=(bsz, t // n),
        in_specs=[
            col(0), col(1), col(2), col(3),
            pl.BlockSpec((None, n, LANES), lambda b, i: (b, i, (C_CONV_DIM + C_QK) // LANES)),
            pl.BlockSpec((C_CONV, C_CONV_DIM), lambda b, i: (0, 0)),
            one, one, one,
        ],
        out_specs=[
            pl.BlockSpec((None, n, C_QK), lambda b, i: (b, i, 0)),
            pl.BlockSpec((None, C_HEADS, C_DIM, C_DIM), lambda b, i: (b, 0, 0, 0)),
        ],
        out_shape=[
            jax.ShapeDtypeStruct((bsz, t, C_QK), F32),
            jax.ShapeDtypeStruct((bsz, C_HEADS, C_DIM, C_DIM), F32),
        ],
        scratch_shapes=[pltpu.VMEM((C_HEADS, C_DIM, C_DIM), F32), pltpu.VMEM((n + 8, C_CONV_DIM), F32)],
        compiler_params=_params(("parallel", "arbitrary")),
        name="gdn_prompt",
    )(proj, proj, proj, proj, proj, conv_w, alog_row, dtb_row, norm_g.reshape(1, C_DIM))


def _swa_dec_kernel(sink_ref, q_ref, kn_ref, vn_ref, ck_ref, cv_ref, o_ref, ok_ref, ov_ref):
    nb = q_ref.shape[0]
    row = lax.broadcasted_iota(jnp.int32, (WINDOW, LANES), 0)
    grp = lax.broadcasted_iota(jnp.int32, (A_GROUP, WINDOW), 0)
    dist = (WINDOW - 1 - lax.broadcasted_iota(jnp.int32, (A_GROUP, WINDOW), 1)).astype(F32)
    scale = A_HEAD_DIM ** -0.5
    for bb in range(nb):
        keys = jnp.where(row == WINDOW - 1, kn_ref[bb], pltpu.roll(ck_ref[bb], WINDOW - 1, 0))
        vals = jnp.where(row == WINDOW - 1, vn_ref[bb], pltpu.roll(cv_ref[bb], WINDOW - 1, 0))
        ok_ref[bb] = keys
        ov_ref[bb] = vals
        for kv in range(A_KV_HEADS):
            ks = slice(kv * A_HEAD_DIM, (kv + 1) * A_HEAD_DIM)
            h0 = kv * A_GROUP
            slope = jnp.zeros((A_GROUP, WINDOW), F32)
            sink = jnp.zeros((A_GROUP, 1), F32)
            for g in range(A_GROUP):
                slope = jnp.where(grp == g, 2.0 ** (-8.0 * (h0 + g + 1) / A_HEADS), slope)
                sink = jnp.where(grp[:, 0:1] == g, sink_ref[h0 + g], sink)
            q4 = q_ref[bb, h0:h0 + A_GROUP, :].astype(BF16)
            s = _dot_nt(q4, keys[:, ks].astype(BF16)) * scale - slope * dist
            m = jnp.maximum(jnp.max(s, axis=-1, keepdims=True), sink)
            p = jnp.exp(s - m)
            den = jnp.sum(p, axis=-1, keepdims=True) + jnp.exp(sink - m)
            o = _dot(p.astype(BF16), vals[:, ks].astype(BF16))
            o_ref[bb, h0:h0 + A_GROUP, :] = o / den


def _swa_decode(proj2, cache_k, cache_v, sinks):
    nbatch = proj2.shape[0]
    bb = 8
    q3 = proj2[:, :A_Q].reshape(nbatch, A_HEADS, A_HEAD_DIM)
    kn = proj2[:, A_Q:A_Q + A_KV].reshape(nbatch, 1, A_KV)
    vn = proj2[:, A_Q + A_KV:A_Q + 2 * A_KV].reshape(nbatch, 1, A_KV)
    ck = cache_k.reshape(nbatch, WINDOW, A_KV)
    cv = cache_v.reshape(nbatch, WINDOW, A_KV)
    qspec = pl.BlockSpec((bb, A_HEADS, A_HEAD_DIM), lambda j: (j, 0, 0))
    nspec = pl.BlockSpec((bb, 1, A_KV), lambda j: (j, 0, 0))
    cspec = pl.BlockSpec((bb, WINDOW, A_KV), lambda j: (j, 0, 0))
    o, nk, nv = pl.pallas_call(
        _swa_dec_kernel,
        grid=(nbatch // bb,),
        in_specs=[pl.BlockSpec(memory_space=pltpu.SMEM), qspec, nspec, nspec, cspec, cspec],
        out_specs=[qspec, cspec, cspec],
        out_shape=[
            jax.ShapeDtypeStruct((nbatch, A_HEADS, A_HEAD_DIM), F32),
            jax.ShapeDtypeStruct((nbatch, WINDOW, A_KV), F32),
            jax.ShapeDtypeStruct((nbatch, WINDOW, A_KV), F32),
        ],
        compiler_params=_params(("parallel",)),
        name="swa_decode",
    )(sinks, q3, kn, vn, ck, cv)
    shape5 = (nbatch, WINDOW, A_KV_HEADS, A_HEAD_DIM)
    return o.reshape(nbatch, A_Q), nk.reshape(shape5), nv.reshape(shape5)


def _cols(rows, heads):
    nbatch = rows.shape[0]
    x = rows.reshape(nbatch // DEC_BLOCK, DEC_BLOCK, heads, LANES)
    return x.transpose(2, 0, 3, 1)


def _hgrn_dec_kernel(qt_ref, ft_ref, lb_ref, v_ref, g_ref, ng_ref, s_ref, o_ref, so_ref, orow_ref):
    lb = lb_ref[...]
    qc = _silu(qt_ref[...])
    f = lb + (1.0 - lb) * jax.nn.sigmoid(ft_ref[...])
    kk = 1.0 - f
    v = v_ref[...]
    for bb in range(DEC_BLOCK):
        s_new = f[:, bb:bb + 1] * s_ref[bb] + kk[:, bb:bb + 1] * v[bb:bb + 1, :]
        so_ref[bb] = s_new
        orow_ref[bb:bb + 1, :] = jnp.sum(qc[:, bb:bb + 1] * s_new, axis=0, keepdims=True)
    o_ref[...] = _rms_gate(orow_ref[...], ng_ref[...], g_ref[...])


def _hgrn_decode(proj2, state, lower, norm_g):
    nbatch = proj2.shape[0]
    nh = B_HEADS
    base = A_Q + 2 * A_KV
    width = nh * B_DIM
    qt = _cols(proj2[:, base:base + width], nh)
    ft = _cols(proj2[:, base + width:base + 2 * width], nh)
    colspec = pl.BlockSpec((None, None, B_DIM, DEC_BLOCK), lambda h, j: (h, j, 0, 0))
    sspec = pl.BlockSpec((DEC_BLOCK, None, B_DIM, B_DIM), lambda h, j: (j, h, 0, 0))
    cb = base // LANES
    o, s_new = pl.pallas_call(
        _hgrn_dec_kernel,
        grid=(nh, nbatch // DEC_BLOCK),
        in_specs=[
            colspec, colspec,
            pl.BlockSpec((None, B_DIM, 1), lambda h, j: (h, 0, 0)),
            pl.BlockSpec((DEC_BLOCK, B_DIM), lambda h, j: (j, cb + 2 * nh + h)),
            pl.BlockSpec((DEC_BLOCK, B_DIM), lambda h, j: (j, cb + 3 * nh + h)),
            pl.BlockSpec((1, B_DIM), lambda h, j: (0, 0)),
            sspec,
        ],
        out_specs=[pl.BlockSpec((DEC_BLOCK, B_DIM), lambda h, j: (j, h)), sspec],
        out_shape=[
            jax.ShapeDtypeStruct((nbatch, width), F32),
            jax.ShapeDtypeStruct(state.shape, F32),
        ],
        scratch_shapes=[pltpu.VMEM((DEC_BLOCK, B_DIM), F32)],
        compiler_params=_params(("parallel", "parallel")),
        name="hgrn_decode",
    )(qt, ft, lower.reshape(nh, B_DIM, 1), proj2, proj2, norm_g.reshape(1, B_DIM), state)
    return o, s_new


def _gdn_prep_kernel(p_ref, h0_ref, h1_ref, h2_ref, cw_ref, alog_ref, dtb_ref, q_ref, k_ref, v_ref, ab_ref):
    acc = (h0_ref[...] * cw_ref[0:1, :] + h1_ref[...] * cw_ref[1:2, :] + h2_ref[...] * cw_ref[2:3, :]
           + p_ref[:, 0:C_CONV_DIM] * cw_ref[3:4, :])
    qkv = _silu(acc)
    for h in range(C_HEADS):
        hs = slice(h * C_DIM, (h + 1) * C_DIM)
        qh = qkv[:, h * C_DIM:(h + 1) * C_DIM]
        kh = qkv[:, C_QK + h * C_DIM:C_QK + (h + 1) * C_DIM]
        q_ref[:, hs] = qh * lax.rsqrt(jnp.sum(qh * qh, axis=-1, keepdims=True) + NORM_EPS) * (C_DIM ** -0.5)
        k_ref[:, hs] = kh * lax.rsqrt(jnp.sum(kh * kh, axis=-1, keepdims=True) + NORM_EPS)
    v_ref[...] = qkv[:, 2 * C_QK:3 * C_QK]
    ab = p_ref[:, C_CONV_DIM + C_QK:C_CONV_DIM + C_QK + LANES]
    alpha = jnp.exp(-jnp.exp(alog_ref[...]) * _softplus(ab + dtb_ref[...]))
    beta = jax.nn.sigmoid(ab)
    lane = lax.broadcasted_iota(jnp.int32, ab.shape, 1)
    ab_ref[...] = jnp.where(lane < C_HEADS, alpha, beta)


def _gdn_dec_kernel(qt_ref, kt_ref, v_ref, gate_ref, al_ref, be_ref, ng_ref, s_ref, o_ref, so_ref, orow_ref):
    qt = qt_ref[...]
    kt = kt_ref[...]
    v = v_ref[...]
    al = al_ref[...]
    be = be_ref[...]
    for bb in range(DEC_BLOCK):
        s = s_ref[bb]
        kc = kt[:, bb:bb + 1]
        a = al[bb:bb + 1, :]
        ks = jnp.sum(kc * s, axis=0, keepdims=True)
        delta = be[bb:bb + 1, :] * (v[bb:bb + 1, :] - a * ks)
        s_new = a * s + kc * delta
        so_ref[bb] = s_new
        orow_ref[bb:bb + 1, :] = jnp.sum(qt[:, bb:bb + 1] * s_new, axis=0, keepdims=True)
    o_ref[...] = _rms_gate(orow_ref[...], ng_ref[...], gate_ref[...])


def _gdn_decode(proj2, hist, state, conv_w, alog_row, dtb_row, norm_g):
    nbatch = proj2.shape[0]
    nh = C_HEADS
    full = lambda shape: pl.BlockSpec(shape, lambda: tuple(0 for _ in shape))
    hspec = full((nbatch, C_CONV_DIM))
    qn, kn, vc, ab = pl.pallas_call(
        _gdn_prep_kernel,
        grid=(),
        in_specs=[
            full((nbatch, ODD_IN_PAD)), hspec, hspec, hspec,
            full((C_CONV, C_CONV_DIM)), full((1, LANES)), full((1, LANES)),
        ],
        out_specs=[full((nbatch, C_QK)), full((nbatch, C_QK)), full((nbatch, C_QK)), full((nbatch, LANES))],
        out_shape=[
            jax.ShapeDtypeStruct((nbatch, C_QK), F32),
            jax.ShapeDtypeStruct((nbatch, C_QK), F32),
            jax.ShapeDtypeStruct((nbatch, C_QK), F32),
            jax.ShapeDtypeStruct((nbatch, LANES), F32),
        ],
        compiler_params=pltpu.CompilerParams(vmem_limit_bytes=VMEM_LIMIT),
        name="gdn_decode_prep",
    )(proj2, hist[:, 0], hist[:, 1], hist[:, 2], conv_w, alog_row, dtb_row)
    qt = _cols(qn, nh)
    kt = _cols(kn, nh)
    al = jnp.broadcast_to(ab[:, :nh].T[:, :, None], (nh, nbatch, LANES))
    be = jnp.broadcast_to(ab[:, nh:2 * nh].T[:, :, None], (nh, nbatch, LANES))
    colspec = pl.BlockSpec((None, None, C_DIM, DEC_BLOCK), lambda h, j: (h, j, 0, 0))
    sspec = pl.BlockSpec((DEC_BLOCK, None, C_DIM, C_DIM), lambda h, j: (j, h, 0, 0))
    rspec = pl.BlockSpec((None, DEC_BLOCK, LANES), lambda h, j: (h, j, 0))
    gcol = C_CONV_DIM // LANES
    o, s_new = pl.pallas_call(
        _gdn_dec_kernel,
        grid=(nh, nbatch // DEC_BLOCK),
        in_specs=[
            colspec, colspec,
            pl.BlockSpec((DEC_BLOCK, C_DIM), lambda h, j: (j, h)),
            pl.BlockSpec((DEC_BLOCK, C_DIM), lambda h, j: (j, gcol + h)),
            rspec, rspec,
            pl.BlockSpec((1, C_DIM), lambda h, j: (0, 0)),
            sspec,
        ],
        out_specs=[pl.BlockSpec((DEC_BLOCK, C_DIM), lambda h, j: (j, h)), sspec],
        out_shape=[
            jax.ShapeDtypeStruct((nbatch, C_QK), F32),
            jax.ShapeDtypeStruct(state.shape, F32),
        ],
        scratch_shapes=[pltpu.VMEM((DEC_BLOCK, C_DIM), F32)],
        compiler_params=_params(("parallel", "parallel")),
        name="gdn_decode",
    )(qt, kt, vc, proj2, al, be, norm_g.reshape(1, C_DIM), state)
    return o, s_new


def _trunk(x, mods, caches, p, tm):
    decode = caches is not None
    m = mods[0]
    x = _ffn(x, m, 0, p["w_up"][0, 0], p["w_down"][0, 0], p["ln_g"][0, 0], p["ln_b"][0, 0], tm)
    proj = _inproj(x, m, 3, p["even_in"], tm)
    if decode:
        o_a, new_k, new_v = _swa_decode(proj[0], caches[0][0], caches[1][0], p["sinks"])
        o_b, s_hgrn = _hgrn_decode(proj[0], caches[2][0], p["lower"], p["hgrn_norm_g"])
        o_a, o_b = o_a[None], o_b[None]
    else:
        bsz = x.shape[0]
        o_a = _swa_prompt(proj, p["sinks"])
        new_k = proj[:, -WINDOW:, A_Q:A_Q + A_KV].reshape(bsz, WINDOW, A_KV_HEADS, A_HEAD_DIM)
        new_v = proj[:, -WINDOW:, A_Q + A_KV:A_Q + 2 * A_KV].reshape(bsz, WINDOW, A_KV_HEADS, A_HEAD_DIM)
        o_b, s_hgrn = _hgrn_prompt(proj, p["lower"], p["hgrn_norm_g"])
    x = _outproj(x, o_a, 0, o_b, 0, m, 5, p["even_out"], p["ln_g"][0, 1], p["ln_b"][0, 1], tm)
    x = _ffn(x, m, 6, p["w_up"][0, 1], p["w_down"][0, 1], p["ln_g"][0, 2], p["ln_b"][0, 2], tm)
    m = mods[1]
    x = _ffn(x, m, 0, p["w_up"][1, 0], p["w_down"][1, 0], p["ln_g"][1, 0], p["ln_b"][1, 0], tm)
    proj = _inproj(x, m, 3, p["odd_in"], tm)
    if decode:
        hist = caches[4][0]
        o_c, s_gdn = _gdn_decode(proj[0], hist, caches[3][0], p["conv_w"], p["alog_row"], p["dtb_row"],
                                 p["gdn_norm_g"])
        o_c = o_c[None]
        new_hist = jnp.concatenate([hist[:, 1:], proj[0][:, None, :C_CONV_DIM]], axis=1)
    else:
        o_c, s_gdn = _gdn_prompt(proj, p["conv_w"], p["alog_row"], p["dtb_row"], p["gdn_norm_g"])
        new_hist = proj[:, -(C_CONV - 1):, :C_CONV_DIM]
    x = _outproj(x, o_c, 0, o_c, 1, m, 5, p["odd_out"], p["ln_g"][1, 1], p["ln_b"][1, 1], tm)
    x = _ffn(x, m, 6, p["w_up"][1, 1], p["w_down"][1, 1], p["ln_g"][1, 2], p["ln_b"][1, 2], tm)
    return x, new_k[None], new_v[None], s_hgrn[None], s_gdn[None], new_hist[None]


def kernel(x_prompt, x_sample, cache_swa_k, cache_swa_v, state_hgrn, state_gdn, state_gdn_conv, c_prompt, c_sample, ada_w, ada_b, ln_g, ln_b, ffn_w_up, ffn_w_down, even_w_in, even_w_out, swa_sinks, hgrn_norm_g, hgrn_lb_logits, odd_w_in, odd_w_out, gdn_conv_w, gdn_a_log, gdn_dt_bias, gdn_norm_g):
    n_prompt = c_prompt.shape[0]
    n_sample = c_sample.shape[0]
    pad_rows = (-(n_prompt + n_sample)) % 8
    c_all = jnp.concatenate([c_prompt, c_sample, jnp.zeros((pad_rows, D_MODEL), F32)], axis=0)
    mods = _ada_mods(c_all, ada_w, ada_b)
    mods_p = mods[:, :n_prompt].reshape(DEPTH, n_prompt, 1, N_MOD * D_MODEL)
    mods_s = mods[:, n_prompt:n_prompt + n_sample].reshape(DEPTH, 1, n_sample, N_MOD * D_MODEL)

    probs = jax.nn.softmax(hgrn_lb_logits.astype(F32), axis=0)
    lower = (jnp.cumsum(probs, axis=0)[1:] - probs[0])[0]

    def lane_row(v):
        return jnp.pad(v.astype(F32), (0, LANES - v.shape[0])).reshape(1, LANES)

    p = dict(
        w_up=ffn_w_up.astype(BF16), w_down=ffn_w_down.astype(BF16),
        ln_g=ln_g, ln_b=ln_b,
        even_in=even_w_in[0].astype(BF16), even_out=even_w_out[0].astype(BF16),
        odd_in=jnp.pad(odd_w_in[0], ((0, 0), (0, ODD_IN_PAD - ODD_IN))).astype(BF16),
        odd_out=odd_w_out[0].astype(BF16),
        sinks=swa_sinks[0], lower=lower, hgrn_norm_g=hgrn_norm_g[0],
        conv_w=gdn_conv_w[0], alog_row=lane_row(gdn_a_log[0]), dtb_row=lane_row(gdn_dt_bias[0]),
        gdn_norm_g=gdn_norm_g[0],
    )
    y_p, p_k, p_v, p_hgrn, p_gdn, p_conv = _trunk(x_prompt, mods_p, None, p, 512)
    caches = (cache_swa_k, cache_swa_v, state_hgrn, state_gdn, state_gdn_conv)
    x_s = x_sample.reshape(1, n_sample, D_MODEL)
    y_s, s_k, s_v, s_hgrn, s_gdn, s_conv = _trunk(x_s, mods_s, caches, p, n_sample)
    y_s = y_s.reshape(n_sample, 1, D_MODEL)
    return (y_p, y_s, p_k, p_v, p_hgrn, p_gdn, p_conv, s_k, s_v, s_hgrn, s_gdn, s_conv)
```

```python
import functools

import jax
import jax.numpy as jnp
from jax import lax
from jax.experimental import pallas as pl
from jax.experimental.pallas import tpu as pltpu

F32 = jnp.float32
BF16 = jnp.bfloat16
HIGHEST = lax.Precision.HIGHEST

D_MODEL = 1024
DEPTH = 2
WINDOW = 128
A_HEADS = 8
A_KV_HEADS = 2
A_GROUP = A_HEADS // A_KV_HEADS
A_HEAD_DIM = 64
A_Q = A_HEADS * A_HEAD_DIM
A_KV = A_KV_HEADS * A_HEAD_DIM
B_HEADS = 4
B_DIM = 128
EVEN_IN = 2816
C_HEADS = 8
C_DIM = 128
C_QK = C_HEADS * C_DIM
C_CONV = 4
C_CONV_DIM = 3 * C_QK
ODD_IN = C_CONV_DIM + C_QK + 2 * C_HEADS
ODD_IN_PAD = 4224
D_FF = 2816
N_MOD = 9
DN_ALPHA = (2 * DEPTH) ** 0.25
LN_EPS = 1e-5
NORM_EPS = 1e-6
LANES = 128

FF_CHUNK = 256
FFN_TM = 1024
FFN_VMEM_LIMIT = 56 << 20
PROJ_TILE = 1408
HGRN_BLOCK = 128
HGRN_SUB = 8
GDN_CHUNK_LOG = 7
GDN_CHUNK = 1 << GDN_CHUNK_LOG
GDN_BASE_LOG = 3
DEC_BLOCK = 64
VMEM_LIMIT = 48 << 20


def _params(sem, vmem=VMEM_LIMIT):
    return pltpu.CompilerParams(dimension_semantics=sem, vmem_limit_bytes=vmem)


def _silu(x):
    return x * jax.nn.sigmoid(x)


def _softplus(x):
    return jnp.maximum(x, 0.0) + jnp.log(1.0 + jnp.exp(-jnp.abs(x)))


def _layer_norm(y, g, b):
    mu = jnp.mean(y, axis=-1, keepdims=True)
    d = y - mu
    var = jnp.mean(d * d, axis=-1, keepdims=True)
    return d * lax.rsqrt(var + LN_EPS) * g + b


def _rms_gate(o, norm_g, gate):
    y = o * lax.rsqrt(jnp.mean(o * o, axis=-1, keepdims=True) + NORM_EPS)
    return y * norm_g * _silu(gate)


def _dot(a, b, precision=None):
    return jnp.dot(a, b, preferred_element_type=F32, precision=precision)


def _dot_inv(a, b):
    return _dot(a.astype(BF16), b.astype(BF16))


def _dot_nt(a, b, precision=None):
    return lax.dot_general(a, b, (((1,), (1,)), ((), ())), preferred_element_type=F32, precision=precision)


def _ada_kernel(c_ref, w_ref, b_ref, o_ref):
    cs = _silu(c_ref[...]).astype(BF16)
    o_ref[...] = _dot(cs, w_ref[...].astype(BF16)) + b_ref[...]


def _ada_mods(c_all, ada_w, ada_b):
    m = c_all.shape[0]
    n = N_MOD * D_MODEL
    tn = 1152
    return pl.pallas_call(
        _ada_kernel,
        grid=(DEPTH, n // tn),
        in_specs=[
            pl.BlockSpec((m, D_MODEL), lambda l, j: (0, 0)),
            pl.BlockSpec((None, D_MODEL, tn), lambda l, j: (l, 0, j)),
            pl.BlockSpec((None, 1, tn), lambda l, j: (l, 0, j)),
        ],
        out_specs=pl.BlockSpec((None, m, tn), lambda l, j: (l, 0, j)),
        out_shape=jax.ShapeDtypeStruct((DEPTH, m, n), F32),
        compiler_params=_params(("parallel", "parallel")),
        name="ada_mods",
    )(c_all, ada_w, ada_b.reshape(DEPTH, 1, n))


def _mod_spec(mods, k, tm, grid_rank):
    per_token = mods.shape[1] != 1
    rows = tm if per_token else 1
    if grid_rank == 3:
        return pl.BlockSpec((None, rows, D_MODEL), lambda b, i, j: (b, i if per_token else 0, k))
    return pl.BlockSpec((None, rows, D_MODEL), lambda b, i: (b, i if per_token else 0, k))


def _ffn_kernel(x_ref, sh_ref, sc_ref, g_ref, wu_ref, wd_ref, lg_ref, lb_ref, o_ref, acc_ref):
    x = x_ref[...]
    h = (x * (1.0 + sc_ref[...]) + sh_ref[...]).astype(BF16)
    for c in range(D_FF // FF_CHUNK):
        lo = c * FF_CHUNK
        gate = _dot(h, wu_ref[:, lo:lo + FF_CHUNK])
        up = _dot(h, wu_ref[:, D_FF + lo:D_FF + lo + FF_CHUNK])
        act = (_silu(gate) * up).astype(BF16)
        part = _dot(act, wd_ref[lo:lo + FF_CHUNK, :])
        if c == 0:
            acc_ref[...] = part
        else:
            acc_ref[...] += part
    y = DN_ALPHA * x + (0.5 * g_ref[...]) * acc_ref[...]
    o_ref[...] = _layer_norm(y, lg_ref[...], lb_ref[...])


def _ffn(x, mods, k0, w_up, w_down, ln_g, ln_b, layer, sub, tm):
    bsz, t, _ = x.shape
    tm = min(t, max(tm, FFN_TM))
    row = pl.BlockSpec((None, tm, D_MODEL), lambda b, i: (b, i, 0))
    vec = pl.BlockSpec((1, D_MODEL), lambda b, i: (0, 0))
    resident = pl.Buffered(1)
    ln_g = ln_g[layer, 2 * sub]
    ln_b = ln_b[layer, 2 * sub]
    return pl.pallas_call(
        _ffn_kernel,
        grid=(bsz, t // tm),
        in_specs=[
            row,
            _mod_spec(mods, k0, tm, 2), _mod_spec(mods, k0 + 1, tm, 2), _mod_spec(mods, k0 + 2, tm, 2),
            pl.BlockSpec((None, None, D_MODEL, 2 * D_FF), lambda b, i: (layer, sub, 0, 0), pipeline_mode=resident),
            pl.BlockSpec((None, None, D_FF, D_MODEL), lambda b, i: (layer, sub, 0, 0), pipeline_mode=resident),
            vec, vec,
        ],
        out_specs=row,
        out_shape=jax.ShapeDtypeStruct(x.shape, F32),
        scratch_shapes=[pltpu.VMEM((tm, D_MODEL), F32)],
        compiler_params=_params(("parallel", "parallel"), FFN_VMEM_LIMIT),
        name="ffn",
    )(x, mods, mods, mods, w_up, w_down, ln_g.reshape(1, D_MODEL), ln_b.reshape(1, D_MODEL))


def _inproj_kernel(x_ref, sh_ref, sc_ref, w_ref, o_ref):
    h = (x_ref[...] * (1.0 + sc_ref[...]) + sh_ref[...]).astype(BF16)
    for lo in range(0, w_ref.shape[1], PROJ_TILE):
        o_ref[:, lo:lo + PROJ_TILE] = _dot(h, w_ref[:, lo:lo + PROJ_TILE])


def _inproj(x, mods, k0, w, tm):
    bsz, t, _ = x.shape
    n = w.shape[1]
    return pl.pallas_call(
        _inproj_kernel,
        grid=(bsz, t // tm),
        in_specs=[
            pl.BlockSpec((None, tm, D_MODEL), lambda b, i: (b, i, 0)),
            _mod_spec(mods, k0, tm, 2), _mod_spec(mods, k0 + 1, tm, 2),
            pl.BlockSpec((D_MODEL, n), lambda b, i: (0, 0), pipeline_mode=pl.Buffered(1)),
        ],
        out_specs=pl.BlockSpec((None, tm, n), lambda b, i: (b, i, 0)),
        out_shape=jax.ShapeDtypeStruct((bsz, t, n), F32),
        compiler_params=_params(("parallel", "parallel")),
        name="inproj",
    )(x, mods, mods, w)


def _outproj_kernel(x_ref, o1_ref, o2_ref, g_ref, w1_ref, w2_ref, lg_ref, lb_ref, y_ref):
    mix = _dot(o1_ref[...].astype(BF16), w1_ref[...]) + _dot(o2_ref[...].astype(BF16), w2_ref[...])
    y = DN_ALPHA * x_ref[...] + g_ref[...] * mix
    y_ref[...] = _layer_norm(y, lg_ref[...], lb_ref[...])


def _outproj(x, o1, c1, o2, c2, mods, kg, w_out, ln_g, ln_b, tm):
    bsz, t, _ = x.shape
    half = D_MODEL // 2
    row = pl.BlockSpec((None, tm, D_MODEL), lambda b, i: (b, i, 0))
    vec = pl.BlockSpec((1, D_MODEL), lambda b, i: (0, 0))
    return pl.pallas_call(
        _outproj_kernel,
        grid=(bsz, t // tm),
        in_specs=[
            row,
            pl.BlockSpec((None, tm, half), lambda b, i: (b, i, c1)),
            pl.BlockSpec((None, tm, half), lambda b, i: (b, i, c2)),
            _mod_spec(mods, kg, tm, 2),
            pl.BlockSpec((half, D_MODEL), lambda b, i: (0, 0)),
            pl.BlockSpec((half, D_MODEL), lambda b, i: (1, 0)),
            vec, vec,
        ],
        out_specs=row,
        out_shape=jax.ShapeDtypeStruct(x.shape, F32),
        compiler_params=_params(("parallel", "parallel")),
        name="outproj",
    )(x, o1, o2, mods, w_out, w_out, ln_g.reshape(1, D_MODEL), ln_b.reshape(1, D_MODEL))


def _swa_kernel(sink_ref, q_ref, kc_ref, kp_ref, vc_ref, vp_ref, o_ref):
    i = pl.program_id(1)
    q = q_ref[...]
    kc = kc_ref[...].astype(BF16)
    kp = kp_ref[...].astype(BF16)
    vc = vc_ref[...].astype(BF16)
    vp = vp_ref[...].astype(BF16)
    r = lax.broadcasted_iota(jnp.int32, (WINDOW, WINDOW), 0)
    c = lax.broadcasted_iota(jnp.int32, (WINDOW, WINDOW), 1)
    dist_c = (r - c).astype(F32)
    dist_p = dist_c + float(WINDOW)
    valid_c = c <= r
    valid_p = c > r + jnp.where(i > 0, 0, WINDOW)
    scale = A_HEAD_DIM ** -0.5
    heads = range(A_HEADS)
    hsl = [slice(h * A_HEAD_DIM, (h + 1) * A_HEAD_DIM) for h in heads]
    ksl = [slice((h // A_GROUP) * A_HEAD_DIM, (h // A_GROUP + 1) * A_HEAD_DIM) for h in heads]
    slope = [2.0 ** (-8.0 * (h + 1) / A_HEADS) for h in heads]
    qh = [q[:, hsl[h]].astype(BF16) for h in heads]
    s_c = [_dot_nt(qh[h], kc[:, ksl[h]]) for h in heads]
    s_p = [_dot_nt(qh[h], kp[:, ksl[h]]) for h in heads]
    s_c = [jnp.where(valid_c, s_c[h] * scale - slope[h] * dist_c, -jnp.inf) for h in heads]
    s_p = [jnp.where(valid_p, s_p[h] * scale - slope[h] * dist_p, -jnp.inf) for h in heads]
    m = [jnp.maximum(jnp.maximum(jnp.max(s_c[h], axis=-1, keepdims=True), jnp.max(s_p[h], axis=-1, keepdims=True)),
                     sink_ref[h]) for h in heads]
    p_c = [jnp.exp(s_c[h] - m[h]) for h in heads]
    p_p = [jnp.exp(s_p[h] - m[h]) for h in heads]
    den = [jnp.sum(p_c[h], axis=-1, keepdims=True) + jnp.sum(p_p[h], axis=-1, keepdims=True)
           + jnp.exp(sink_ref[h] - m[h]) for h in heads]
    o = [_dot(p_c[h].astype(BF16), vc[:, ksl[h]]) + _dot(p_p[h].astype(BF16), vp[:, ksl[h]]) for h in heads]
    o_ref[...] = jnp.concatenate([o[h] / den[h] for h in heads], axis=1).astype(o_ref.dtype)


def _swa_prompt(proj, sinks):
    bsz, t, _ = proj.shape
    nb = t // WINDOW
    kcol = A_Q // LANES
    vcol = kcol + 1

    def cur(col):
        return pl.BlockSpec((None, WINDOW, LANES), lambda b, i: (b, i, col))

    def prev(col):
        return pl.BlockSpec((None, WINDOW, LANES), lambda b, i: (b, jnp.maximum(i - 1, 0), col))

    return pl.pallas_call(
        _swa_kernel,
        grid=(bsz, nb),
        in_specs=[
            pl.BlockSpec(memory_space=pltpu.SMEM),
            pl.BlockSpec((None, WINDOW, A_Q), lambda b, i: (b, i, 0)),
            cur(kcol), prev(kcol), cur(vcol), prev(vcol),
        ],
        out_specs=pl.BlockSpec((None, WINDOW, A_Q), lambda b, i: (b, i, 0)),
        out_shape=jax.ShapeDtypeStruct((bsz, t, A_Q), BF16),
        compiler_params=_params(("parallel", "parallel")),
        name="swa_prompt",
    )(sinks, proj, proj, proj, proj, proj)


def _split3(x):
    hi = x.astype(BF16)
    r1 = x - hi.astype(F32)
    mid = r1.astype(BF16)
    lo = (r1 - mid.astype(F32)).astype(BF16)
    return hi, mid, lo


def _hgrn_kernel(q0_ref, q1_ref, f0_ref, f1_ref, v0_ref, v1_ref, g0_ref, g1_ref, lb_ref, ng_ref,
                 o_ref, s_ref, st_ref):
    i = pl.program_id(1)

    @pl.when(i == 0)
    def _():
        st_ref[...] = jnp.zeros_like(st_ref)

    n = HGRN_BLOCK
    nsub = n // HGRN_SUB
    heads = range(B_HEADS)
    lb = lb_ref[...]
    qb = _silu(jnp.concatenate([q0_ref[...], q1_ref[...]], axis=1))
    f = lb + (1.0 - lb) * jax.nn.sigmoid(jnp.concatenate([f0_ref[...], f1_ref[...]], axis=1))
    lf = jnp.log(f)
    kk = 1.0 - f
    v = jnp.concatenate([v0_ref[...], v1_ref[...]], axis=1)
    r = lax.broadcasted_iota(jnp.int32, (n, n), 0)
    c = lax.broadcasted_iota(jnp.int32, (n, n), 1)
    tri = jnp.where((c <= r) & (c >= r - (r & (HGRN_SUB - 1))), 1.0, 0.0).astype(BF16)
    lf_hi, lf_mid, lf_lo = _split3(lf)
    a_all = _dot(tri, lf_hi) + (_dot(tri, lf_mid) + _dot(tri, lf_lo))
    a3 = a_all.reshape(nsub, HGRN_SUB, B_HEADS * B_DIM)
    a_last3 = a3[:, HGRN_SUB - 1:HGRN_SUB, :]
    qb3 = qb.reshape(a3.shape)
    kk3 = kk.reshape(a3.shape)
    v3 = v.reshape(a3.shape)
    qe3 = qb3 * jnp.exp(a3)
    kd3 = kk3 * jnp.exp(a_last3 - a3)
    dec3 = jnp.exp(a_last3)
    trow = lax.broadcasted_iota(jnp.int32, (1, HGRN_SUB, 1), 1)
    o_diag = []
    v_t = []
    for h in heads:
        hs = slice(h * B_DIM, (h + 1) * B_DIM)
        a_h, q_h, k_h, v_h = a3[:, :, hs], qb3[:, :, hs], kk3[:, :, hs], v3[:, :, hs]
        acc = jnp.zeros((nsub, HGRN_SUB, B_DIM), F32)
        for s in range(HGRN_SUB):
            e = jnp.exp(jnp.where(trow >= s, a_h - a_h[:, s:s + 1, :], -jnp.inf))
            col = jnp.sum(q_h * e * k_h[:, s:s + 1, :], axis=-1, keepdims=True)
            acc = acc + col * v_h[:, s:s + 1, :]
        o_diag.append(acc.reshape(n, B_DIM))
        v_t.append(v[:, hs].T.astype(BF16))
    state = [st_ref[h] for h in heads]
    o_state = [[] for _ in heads]
    for ch in range(nsub):
        sl = slice(ch * HGRN_SUB, (ch + 1) * HGRN_SUB)
        for h in heads:
            hs = slice(h * B_DIM, (h + 1) * B_DIM)
            o_state[h].append(_dot_nt(qe3[ch, :, hs].astype(BF16), state[h].astype(BF16)))
            state[h] = state[h] * dec3[ch, :, hs] + _dot(v_t[h][:, sl], kd3[ch, :, hs].astype(BF16))
    g = jnp.concatenate([g0_ref[...], g1_ref[...]], axis=1)
    for h in heads:
        hs = slice(h * B_DIM, (h + 1) * B_DIM)
        st_ref[h] = state[h]
        o = jnp.concatenate(o_state[h], axis=0) + o_diag[h]
        o_ref[:, hs] = _rms_gate(o, ng_ref[...], g[:, hs]).astype(o_ref.dtype)

    @pl.when(i == pl.num_programs(1) - 1)
    def _():
        for h in heads:
            s_ref[h] = state[h].T


def _hgrn_prompt(proj, lower, norm_g):
    bsz, t, _ = proj.shape
    nh = B_HEADS
    width = nh * B_DIM
    half = width // 2
    base = (A_Q + 2 * A_KV) // half

    def col(off):
        return pl.BlockSpec((None, HGRN_BLOCK, half), lambda b, i: (b, i, base + off))

    return pl.pallas_call(
        _hgrn_kernel,
        grid=(bsz, t // HGRN_BLOCK),
        in_specs=[
            col(0), col(1), col(2), col(3), col(4), col(5), col(6), col(7),
            pl.BlockSpec((1, width), lambda b, i: (0, 0)),
            pl.BlockSpec((1, B_DIM), lambda b, i: (0, 0)),
        ],
        out_specs=[
            pl.BlockSpec((None, HGRN_BLOCK, width), lambda b, i: (b, i, 0)),
            pl.BlockSpec((None, nh, B_DIM, B_DIM), lambda b, i: (b, 0, 0, 0)),
        ],
        out_shape=[
            jax.ShapeDtypeStruct((bsz, t, width), BF16),
            jax.ShapeDtypeStruct((bsz, nh, B_DIM, B_DIM), F32),
        ],
        scratch_shapes=[pltpu.VMEM((nh, B_DIM, B_DIM), F32)],
        compiler_params=_params(("parallel", "arbitrary")),
        name="hgrn_prompt",
    )(proj, proj, proj, proj, proj, proj, proj, proj, lower.reshape(1, width), norm_g.reshape(1, B_DIM))


def _gdn_kernel(q_ref, k_ref, v_ref, gate_ref, ab_ref, cw_ref, alog_ref, dtb_ref, ng_ref,
                o_ref, s_out_ref, s_ref, xs_ref):
    i = pl.program_id(1)
    n = GDN_CHUNK

    @pl.when(i == 0)
    def _():
        s_ref[...] = jnp.zeros_like(s_ref)
        xs_ref[0:8, :] = jnp.zeros((8, C_CONV_DIM), F32)

    @pl.when(i > 0)
    def _():
        xs_ref[0:8, :] = xs_ref[n:n + 8, :]

    xs_ref[8:n + 8, 0:C_QK] = q_ref[...]
    xs_ref[8:n + 8, C_QK:2 * C_QK] = k_ref[...]
    xs_ref[8:n + 8, 2 * C_QK:3 * C_QK] = v_ref[...]
    xe = xs_ref[...]
    z = xe * cw_ref[0:1, :]
    for j in range(1, C_CONV):
        z = pltpu.roll(z, 1, 0) + xe * cw_ref[j:j + 1, :]
    qkv = _silu(z[8:n + 8])

    ab = ab_ref[...]
    log_alpha = -jnp.exp(alog_ref[...]) * _softplus(ab + dtb_ref[...])
    beta_all = jax.nn.sigmoid(ab)
    r = lax.broadcasted_iota(jnp.int32, (n, n), 0)
    c = lax.broadcasted_iota(jnp.int32, (n, n), 1)
    causal = c <= r
    strict = c < r
    tri = jnp.where(causal, 1.0, 0.0).astype(F32)
    eye = jnp.where(r == c, 1.0, 0.0).astype(F32)
    base_mask = (r >> GDN_BASE_LOG) == (c >> GDN_BASE_LOG)
    level_masks = [((r >> (k + 1)) == (c >> (k + 1))) & ((r >> k) != (c >> k))
                   for k in range(GDN_BASE_LOG, GDN_CHUNK_LOG)]
    g_cum =_dot(tri, log_alpha, HIGHEST)
    g_cum_t = g_cum.T

    heads = range(C_HEADS)
    qn, kn, kn_b, vh, g_col, beta, dec_incl, e_g = [], [], [], [], [], [], [], []
    for h in heads:
        q_h = qkv[:, h * C_DIM:(h + 1) * C_DIM]
        k_h = qkv[:, C_QK + h * C_DIM:C_QK + (h + 1) * C_DIM]
        vh.append(qkv[:, 2 * C_QK + h * C_DIM:2 * C_QK + (h + 1) * C_DIM])
        qn.append(q_h * lax.rsqrt(jnp.sum(q_h * q_h, axis=-1, keepdims=True) + NORM_EPS) * (C_DIM ** -0.5))
        kn.append(k_h * lax.rsqrt(jnp.sum(k_h * k_h, axis=-1, keepdims=True) + NORM_EPS))
        kn_b.append(kn[h].astype(BF16))
        g_col.append(g_cum[:, h:h + 1])
        beta.append(beta_all[:, C_HEADS + h:C_HEADS + h + 1])
        dec_incl.append(jnp.exp(jnp.where(causal, g_col[h] - g_cum_t[h:h + 1, :], -jnp.inf)))
        e_g.append(jnp.exp(g_col[h]))
    kkt = [_dot_nt(kn_b[h], kn_b[h]) for h in heads]
    qk = [_dot_nt(qn[h].astype(BF16), kn_b[h]) for h in heads]
    l_mat = [beta[h] * kkt[h] * jnp.where(strict, dec_incl[h], 0.0) for h in heads]
    a_pow = [jnp.where(base_mask, -l_mat[h], 0.0) for h in heads]
    t_inv = [eye + a_pow[h] for h in heads]
    for _ in range(GDN_BASE_LOG - 1):
        a_pow = [_dot_inv(a_pow[h], a_pow[h]) for h in heads]
        t_inv = [t_inv[h] + _dot_inv(t_inv[h], a_pow[h]) for h in heads]
    for lm in level_masks:
        x = [_dot_inv(jnp.where(lm, l_mat[h], 0.0), t_inv[h]) for h in heads]
        t_inv = [t_inv[h] - _dot_inv(t_inv[h], x[h]) for h in heads]
    uw = [_dot_inv(t_inv[h], jnp.concatenate([beta[h] * vh[h], (beta[h] * e_g[h]) * kn[h]], axis=1))
          for h in heads]
    s_old = [s_ref[h] for h in heads]
    s_b = [s_old[h].astype(BF16) for h in heads]
    ws = [_dot(uw[h][:, C_DIM:].astype(BF16), s_b[h]) for h in heads]
    qs = [_dot((qn[h] * e_g[h]).astype(BF16), s_b[h]) for h in heads]
    delta_b = [(uw[h][:, :C_DIM] - ws[h]).astype(BF16) for h in heads]
    o = [qs[h] + _dot((qk[h] * dec_incl[h]).astype(BF16), delta_b[h]) for h in heads]
    for h in heads:
        g_last = g_col[h][n - 1:n]
        k_dec = kn[h] * jnp.exp(g_last - g_col[h])
        s_ref[h] = jnp.exp(g_last) * s_old[h] + _dot(k_dec.T.astype(BF16), delta_b[h])
    for h in heads:
        hs = slice(h * C_DIM, (h + 1) * C_DIM)
        o_ref[:, hs] = _rms_gate(o[h], ng_ref[...], gate_ref[:, hs]).astype(o_ref.dtype)

    @pl.when(i == pl.num_programs(1) - 1)
    def _():
        s_out_ref[...] = s_ref[...]


def _gdn_prompt(proj, conv_w, alog_row, dtb_row, norm_g):
    bsz, t, _ = proj.shape
    n = GDN_CHUNK

    def col(cidx):
        return pl.BlockSpec((None, n, C_QK), lambda b, i: (b, i, cidx))

    one = pl.BlockSpec((1, LANES), lambda b, i: (0, 0))
    return pl.pallas_call(
        _gdn_kernel,
        grid=(bsz, t // n),
        in_specs=[
            col(0), col(1), col(2), col(3),
            pl.BlockSpec((None, n, LANES), lambda b, i: (b, i, (C_CONV_DIM + C_QK) // LANES)),
            pl.BlockSpec((C_CONV, C_CONV_DIM), lambda b, i: (0, 0)),
            one, one, one,
        ],
        out_specs=[
            pl.BlockSpec((None, n, C_QK), lambda b, i: (b, i, 0)),
            pl.BlockSpec((None, C_HEADS, C_DIM, C_DIM), lambda b, i: (b, 0, 0, 0)),
        ],
        out_shape=[
            jax.ShapeDtypeStruct((bsz, t, C_QK), BF16),
            jax.ShapeDtypeStruct((bsz, C_HEADS, C_DIM, C_DIM), F32),
        ],
        scratch_shapes=[pltpu.VMEM((C_HEADS, C_DIM, C_DIM), F32), pltpu.VMEM((n + 8, C_CONV_DIM), F32)],
        compiler_params=_params(("parallel", "arbitrary")),
        name="gdn_prompt",
    )(proj, proj, proj, proj, proj, conv_w, alog_row, dtb_row, norm_g.reshape(1, C_DIM))


def _swa_dec_kernel(sink_ref, q_ref, kn_ref, vn_ref, ck_ref, cv_ref, o_ref, ok_ref, ov_ref):
    batch = range(q_ref.shape[0])
    row = lax.broadcasted_iota(jnp.int32, (WINDOW, LANES), 0)
    head = lax.broadcasted_iota(jnp.int32, (A_HEADS, WINDOW), 0)
    dist = (WINDOW - 1 - lax.broadcasted_iota(jnp.int32, (A_HEADS, WINDOW), 1)).astype(F32)
    scale = A_HEAD_DIM ** -0.5
    slope = jnp.zeros((A_HEADS, WINDOW), F32)
    sink = jnp.zeros((A_HEADS, 1), F32)
    for h in range(A_HEADS):
        slope = jnp.where(head == h, 2.0 ** (-8.0 * (h + 1) / A_HEADS), slope)
        sink = jnp.where(head[:, 0:1] == h, sink_ref[h], sink)
    bias = slope * dist
    first_kv = head[:, 0:A_HEAD_DIM] < A_GROUP
    keys = [jnp.where(row == WINDOW - 1, kn_ref[b], pltpu.roll(ck_ref[b], WINDOW - 1, 0)) for b in batch]
    vals = [jnp.where(row == WINDOW - 1, vn_ref[b], pltpu.roll(cv_ref[b], WINDOW - 1, 0)) for b in batch]
    for b in batch:
        ok_ref[b] = keys[b]
        ov_ref[b] = vals[b]
    q8 = [jnp.concatenate([jnp.where(first_kv, q_ref[b], 0.0), jnp.where(first_kv, 0.0, q_ref[b])],
                          axis=1).astype(BF16) for b in batch]
    s = [_dot_nt(q8[b], keys[b].astype(BF16)) * scale - bias for b in batch]
    m = [jnp.maximum(jnp.max(s[b], axis=-1, keepdims=True), sink) for b in batch]
    p = [jnp.exp(s[b] - m[b]) for b in batch]
    den = [jnp.sum(p[b], axis=-1, keepdims=True) + jnp.exp(sink - m[b]) for b in batch]
    o = [_dot(p[b].astype(BF16), vals[b].astype(BF16)) for b in batch]
    for b in batch:
        o_ref[b] = jnp.where(first_kv, o[b][:, :A_HEAD_DIM], o[b][:, A_HEAD_DIM:]) / den[b]


def _swa_decode(proj2, cache_k, cache_v, sinks):
    nbatch = proj2.shape[0]
    bb = 8
    q3 = proj2[:, :A_Q].reshape(nbatch, A_HEADS, A_HEAD_DIM)
    kn = proj2[:, A_Q:A_Q + A_KV].reshape(nbatch, 1, A_KV)
    vn = proj2[:, A_Q + A_KV:A_Q + 2 * A_KV].reshape(nbatch, 1, A_KV)
    ck = cache_k.reshape(nbatch, WINDOW, A_KV)
    cv = cache_v.reshape(nbatch, WINDOW, A_KV)
    qspec = pl.BlockSpec((bb, A_HEADS, A_HEAD_DIM), lambda j: (j, 0, 0))
    nspec = pl.BlockSpec((bb, 1, A_KV), lambda j: (j, 0, 0))
    cspec = pl.BlockSpec((bb, WINDOW, A_KV), lambda j: (j, 0, 0))
    o, nk, nv = pl.pallas_call(
        _swa_dec_kernel,
        grid=(nbatch // bb,),
        in_specs=[pl.BlockSpec(memory_space=pltpu.SMEM), qspec, nspec, nspec, cspec, cspec],
        out_specs=[qspec, cspec, cspec],
        out_shape=[
            jax.ShapeDtypeStruct((nbatch, A_HEADS, A_HEAD_DIM), F32),
            jax.ShapeDtypeStruct((nbatch, WINDOW, A_KV), F32),
            jax.ShapeDtypeStruct((nbatch, WINDOW, A_KV), F32),
        ],
        compiler_params=_params(("parallel",)),
        name="swa_decode",
    )(sinks, q3, kn, vn, ck, cv)
    shape5 = (nbatch, WINDOW, A_KV_HEADS, A_HEAD_DIM)
    return o.reshape(nbatch, A_Q), nk.reshape(shape5), nv.reshape(shape5)


def _cols(rows, heads):
    nbatch = rows.shape[0]
    x = rows.reshape(nbatch // DEC_BLOCK, DEC_BLOCK, heads, LANES)
    return x.transpose(2, 0, 3, 1)


def _hgrn_dec_kernel(qt_ref, ft_ref, lb_ref, v_ref, g_ref, ng_ref, s_ref, o_ref, so_ref, orow_ref):
    lb = lb_ref[...]
    qc = _silu(qt_ref[...])
    f = lb + (1.0 - lb) * jax.nn.sigmoid(ft_ref[...])
    kk = 1.0 - f
    v = v_ref[...]
    for bb in range(DEC_BLOCK):
        s_new = f[:, bb:bb + 1] * s_ref[bb] + kk[:, bb:bb + 1] * v[bb:bb + 1, :]
        so_ref[bb] = s_new
        orow_ref[bb:bb + 1, :] = jnp.sum(qc[:, bb:bb + 1] * s_new, axis=0, keepdims=True)
    o_ref[...] = _rms_gate(orow_ref[...], ng_ref[...], g_ref[...])


def _hgrn_decode(proj2, state, lower, norm_g):
    nbatch = proj2.shape[0]
    nh = B_HEADS
    base = A_Q + 2 * A_KV
    width = nh * B_DIM
    qt = _cols(proj2[:, base:base + width], nh)
    ft = _cols(proj2[:, base + width:base + 2 * width], nh)
    colspec = pl.BlockSpec((None, None, B_DIM, DEC_BLOCK), lambda h, j: (h, j, 0, 0))
    sspec = pl.BlockSpec((DEC_BLOCK, None, B_DIM, B_DIM), lambda h, j: (j, h, 0, 0))
    cb = base // LANES
    o, s_new = pl.pallas_call(
        _hgrn_dec_kernel,
        grid=(nh, nbatch // DEC_BLOCK),
        in_specs=[
            colspec, colspec,
            pl.BlockSpec((None, B_DIM, 1), lambda h, j: (h, 0, 0)),
            pl.BlockSpec((DEC_BLOCK, B_DIM), lambda h, j: (j, cb + 2 * nh + h)),
            pl.BlockSpec((DEC_BLOCK, B_DIM), lambda h, j: (j, cb + 3 * nh + h)),
            pl.BlockSpec((1, B_DIM), lambda h, j: (0, 0)),
            sspec,
        ],
        out_specs=[pl.BlockSpec((DEC_BLOCK, B_DIM), lambda h, j: (j, h)), sspec],
        out_shape=[
            jax.ShapeDtypeStruct((nbatch, width), F32),
            jax.ShapeDtypeStruct(state.shape, F32),
        ],
        scratch_shapes=[pltpu.VMEM((DEC_BLOCK, B_DIM), F32)],
        compiler_params=_params(("parallel", "parallel")),
        name="hgrn_decode",
    )(qt, ft, lower.reshape(nh, B_DIM, 1), proj2, proj2, norm_g.reshape(1, B_DIM), state)
    return o, s_new


def _gdn_prep_kernel(p_ref, h0_ref, h1_ref, h2_ref, cw_ref, alog_ref, dtb_ref, q_ref, k_ref, v_ref, ab_ref):
    acc = (h0_ref[...] * cw_ref[0:1, :] + h1_ref[...] * cw_ref[1:2, :] + h2_ref[...] * cw_ref[2:3, :]
           + p_ref[:, 0:C_CONV_DIM] * cw_ref[3:4, :])
    qkv = _silu(acc)
    for h in range(C_HEADS):
        hs = slice(h * C_DIM, (h + 1) * C_DIM)
        qh = qkv[:, h * C_DIM:(h + 1) * C_DIM]
        kh = qkv[:, C_QK + h * C_DIM:C_QK + (h + 1) * C_DIM]
        q_ref[:, hs] = qh * lax.rsqrt(jnp.sum(qh * qh, axis=-1, keepdims=True) + NORM_EPS) * (C_DIM ** -0.5)
        k_ref[:, hs] = kh * lax.rsqrt(jnp.sum(kh * kh, axis=-1, keepdims=True) + NORM_EPS)
    v_ref[...] = qkv[:, 2 * C_QK:3 * C_QK]
    ab = p_ref[:, C_CONV_DIM + C_QK:C_CONV_DIM + C_QK + LANES]
    alpha = jnp.exp(-jnp.exp(alog_ref[...]) * _softplus(ab + dtb_ref[...]))
    beta = jax.nn.sigmoid(ab)
    lane = lax.broadcasted_iota(jnp.int32, ab.shape, 1)
    ab_ref[...] = jnp.where(lane < C_HEADS, alpha, beta)


def _gdn_dec_kernel(qt_ref, kt_ref, v_ref, gate_ref, al_ref, be_ref, ng_ref, s_ref, o_ref, so_ref, orow_ref):
    qt = qt_ref[...]
    kt = kt_ref[...]
    v = v_ref[...]
    al = al_ref[...]
    be = be_ref[...]
    for bb in range(DEC_BLOCK):
        s = s_ref[bb]
        kc = kt[:, bb:bb + 1]
        a = al[bb:bb + 1, :]
        ks = jnp.sum(kc * s, axis=0, keepdims=True)
        delta = be[bb:bb + 1, :] * (v[bb:bb + 1, :] - a * ks)
        s_new = a * s + kc * delta
        so_ref[bb] = s_new
        orow_ref[bb:bb + 1, :] = jnp.sum(qt[:, bb:bb + 1] * s_new, axis=0, keepdims=True)
    o_ref[...] = _rms_gate(orow_ref[...], ng_ref[...], gate_ref[...])


def _gdn_decode(proj2, hist, state, conv_w, alog_row, dtb_row, norm_g):
    nbatch = proj2.shape[0]
    nh = C_HEADS
    full = lambda shape: pl.BlockSpec(shape, lambda: tuple(0 for _ in shape))
    hspec = full((nbatch, C_CONV_DIM))
    qn, kn, vc, ab = pl.pallas_call(
        _gdn_prep_kernel,
        grid=(),
        in_specs=[
            full((nbatch, ODD_IN_PAD)), hspec, hspec, hspec,
            full((C_CONV, C_CONV_DIM)), full((1, LANES)), full((1, LANES)),
        ],
        out_specs=[full((nbatch, C_QK)), full((nbatch, C_QK)), full((nbatch, C_QK)), full((nbatch, LANES))],
        out_shape=[
            jax.ShapeDtypeStruct((nbatch, C_QK), F32),
            jax.ShapeDtypeStruct((nbatch, C_QK), F32),
            jax.ShapeDtypeStruct((nbatch, C_QK), F32),
            jax.ShapeDtypeStruct((nbatch, LANES), F32),
        ],
        compiler_params=pltpu.CompilerParams(vmem_limit_bytes=VMEM_LIMIT),
        name="gdn_decode_prep",
    )(proj2, hist[:, 0], hist[:, 1], hist[:, 2], conv_w, alog_row, dtb_row)
    qt = _cols(qn, nh)
    kt = _cols(kn, nh)
    al = jnp.broadcast_to(ab[:, :nh].T[:, :, None], (nh, nbatch, LANES))
    be = jnp.broadcast_to(ab[:, nh:2 * nh].T[:, :, None], (nh, nbatch, LANES))
    colspec = pl.BlockSpec((None, None, C_DIM, DEC_BLOCK), lambda h, j: (h, j, 0, 0))
    sspec = pl.BlockSpec((DEC_BLOCK, None, C_DIM, C_DIM), lambda h, j: (j, h, 0, 0))
    rspec = pl.BlockSpec((None, DEC_BLOCK, LANES), lambda h, j: (h, j, 0))
    gcol = C_CONV_DIM // LANES
    o, s_new = pl.pallas_call(
        _gdn_dec_kernel,
        grid=(nh, nbatch // DEC_BLOCK),
        in_specs=[
            colspec, colspec,
            pl.BlockSpec((DEC_BLOCK, C_DIM), lambda h, j: (j, h)),
            pl.BlockSpec((DEC_BLOCK, C_DIM), lambda h, j: (j, gcol + h)),
            rspec, rspec,
            pl.BlockSpec((1, C_DIM), lambda h, j: (0, 0)),
            sspec,
        ],
        out_specs=[pl.BlockSpec((DEC_BLOCK, C_DIM), lambda h, j: (j, h)), sspec],
        out_shape=[
            jax.ShapeDtypeStruct((nbatch, C_QK), F32),
            jax.ShapeDtypeStruct(state.shape, F32),
        ],
        scratch_shapes=[pltpu.VMEM((DEC_BLOCK, C_DIM), F32)],
        compiler_params=_params(("parallel", "parallel")),
        name="gdn_decode",
    )(qt, kt, vc, proj2, al, be, norm_g.reshape(1, C_DIM), state)
    return o, s_new


def _trunk(x, mods, caches, p, tm):
    decode = caches is not None
    m = mods[0]
    ffn = functools.partial(_ffn, w_up=p["w_up"], w_down=p["w_down"], ln_g=p["ln_g"], ln_b=p["ln_b"], tm=tm)
    x = ffn(x, m, 0, layer=0, sub=0)
    proj = _inproj(x, m, 3, p["even_in"], tm)
    if decode:
        o_a, new_k, new_v = _swa_decode(proj[0], caches[0][0], caches[1][0], p["sinks"])
        o_b, s_hgrn = _hgrn_decode(proj[0], caches[2][0], p["lower"], p["hgrn_norm_g"])
        o_a, o_b = o_a[None], o_b[None]
    else:
        bsz = x.shape[0]
        o_a = _swa_prompt(proj, p["sinks"])
        new_k = proj[:, -WINDOW:, A_Q:A_Q + A_KV].reshape(bsz, WINDOW, A_KV_HEADS, A_HEAD_DIM)
        new_v = proj[:, -WINDOW:, A_Q + A_KV:A_Q + 2 * A_KV].reshape(bsz, WINDOW, A_KV_HEADS, A_HEAD_DIM)
        o_b, s_hgrn = _hgrn_prompt(proj, p["lower"], p["hgrn_norm_g"])
    x = _outproj(x, o_a, 0, o_b, 0, m, 5, p["even_out"], p["ln_g"][0, 1], p["ln_b"][0, 1], tm)
    x = ffn(x, m, 6, layer=0, sub=1)
    m = mods[1]
    x = ffn(x, m, 0, layer=1, sub=0)
    proj = _inproj(x, m, 3, p["odd_in"], tm)
    if decode:
        hist = caches[4][0]
        o_c, s_gdn = _gdn_decode(proj[0], hist, caches[3][0], p["conv_w"], p["alog_row"], p["dtb_row"],
                                 p["gdn_norm_g"])
        o_c = o_c[None]
        new_hist = jnp.concatenate([hist[:, 1:], proj[0][:, None, :C_CONV_DIM]], axis=1)
    else:
        o_c, s_gdn = _gdn_prompt(proj, p["conv_w"], p["alog_row"], p["dtb_row"], p["gdn_norm_g"])
        new_hist = proj[:, -(C_CONV - 1):, :C_CONV_DIM]
    x = _outproj(x, o_c, 0, o_c, 1, m, 5, p["odd_out"], p["ln_g"][1, 1], p["ln_b"][1, 1], tm)
    x = ffn(x, m, 6, layer=1, sub=1)
    return x, new_k[None], new_v[None], s_hgrn[None], s_gdn[None], new_hist[None]


def kernel(x_prompt, x_sample, cache_swa_k, cache_swa_v, state_hgrn, state_gdn, state_gdn_conv, c_prompt, c_sample, ada_w, ada_b, ln_g, ln_b, ffn_w_up, ffn_w_down, even_w_in, even_w_out, swa_sinks, hgrn_norm_g, hgrn_lb_logits, odd_w_in, odd_w_out, gdn_conv_w, gdn_a_log, gdn_dt_bias, gdn_norm_g):
    n_prompt = c_prompt.shape[0]
    n_sample = c_sample.shape[0]
    pad_rows = (-(n_prompt + n_sample)) % 8
    c_all = jnp.concatenate([c_prompt, c_sample, jnp.zeros((pad_rows, D_MODEL), F32)], axis=0)
    mods = _ada_mods(c_all, ada_w, ada_b)
    mods_p = mods[:, :n_prompt].reshape(DEPTH, n_prompt, 1, N_MOD * D_MODEL)
    mods_s = mods[:, n_prompt:n_prompt + n_sample].reshape(DEPTH, 1, n_sample, N_MOD * D_MODEL)

    probs = jax.nn.softmax(hgrn_lb_logits.astype(F32), axis=0)
    lower = (jnp.cumsum(probs, axis=0)[1:] - probs[0])[0]

    def lane_row(v):
        return jnp.pad(v.astype(F32), (0, LANES - v.shape[0])).reshape(1, LANES)

    p = dict(
        w_up=ffn_w_up.astype(BF16), w_down=ffn_w_down.astype(BF16),
        ln_g=ln_g, ln_b=ln_b,
        even_in=even_w_in[0].astype(BF16), even_out=even_w_out[0].astype(BF16),
        odd_in=jnp.pad(odd_w_in[0], ((0, 0), (0, ODD_IN_PAD - ODD_IN))).astype(BF16),
        odd_out=odd_w_out[0].astype(BF16),
        sinks=swa_sinks[0], lower=lower, hgrn_norm_g=hgrn_norm_g[0],
        conv_w=gdn_conv_w[0], alog_row=lane_row(gdn_a_log[0]), dtb_row=lane_row(gdn_dt_bias[0]),
        gdn_norm_g=gdn_norm_g[0],
    )
    y_p, p_k, p_v, p_hgrn, p_gdn, p_conv = _trunk(x_prompt, mods_p, None, p, 512)
    caches = (cache_swa_k, cache_swa_v, state_hgrn, state_gdn, state_gdn_conv)
    x_s = x_sample.reshape(1, n_sample, D_MODEL)
    y_s, s_k, s_v, s_hgrn, s_gdn, s_conv = _trunk(x_s, mods_s, caches, p, n_sample)
    y_s = y_s.reshape(n_sample, 1, D_MODEL)
    return (y_p, y_s, p_k, p_v, p_hgrn, p_gdn, p_conv, s_k, s_v, s_hgrn, s_gdn, s_conv)
```

```python
import functools

import jax
import jax.numpy as jnp
from jax import lax
from jax.experimental import pallas as pl
from jax.experimental.pallas import tpu as pltpu

F32 = jnp.float32
BF16 = jnp.bfloat16
HIGHEST = lax.Precision.HIGHEST

D_MODEL = 1024
DEPTH = 2
WINDOW = 128
A_HEADS = 8
A_KV_HEADS = 2
A_GROUP = A_HEADS // A_KV_HEADS
A_HEAD_DIM = 64
A_Q = A_HEADS * A_HEAD_DIM
A_KV = A_KV_HEADS * A_HEAD_DIM
B_HEADS = 4
B_DIM = 128
EVEN_IN = 2816
C_HEADS = 8
C_DIM = 128
C_QK = C_HEADS * C_DIM
C_CONV = 4
C_CONV_DIM = 3 * C_QK
ODD_IN = C_CONV_DIM + C_QK + 2 * C_HEADS
ODD_IN_PAD = 4224
D_FF = 2816
N_MOD = 9
DN_ALPHA = (2 * DEPTH) ** 0.25
LN_EPS = 1e-5
NORM_EPS = 1e-6
LANES = 128

FF_CHUNK = 256
FFN_TM = 1024
FFN_VMEM_LIMIT = 56 << 20
PROJ_TILE = 1408
HGRN_BLOCK = 128
HGRN_SUB = 8
GDN_CHUNK_LOG = 7
GDN_CHUNK = 1 << GDN_CHUNK_LOG
GDN_BASE_LOG = 3
DEC_BLOCK = 64
VMEM_LIMIT = 48 << 20


def _params(sem, vmem=VMEM_LIMIT):
    return pltpu.CompilerParams(dimension_semantics=sem, vmem_limit_bytes=vmem)


def _silu(x):
    return x * jax.nn.sigmoid(x)


def _softplus(x):
    return jnp.maximum(x, 0.0) + jnp.log(1.0 + jnp.exp(-jnp.abs(x)))


def _layer_norm(y, g, b):
    mu = jnp.mean(y, axis=-1, keepdims=True)
    d = y - mu
    var = jnp.mean(d * d, axis=-1, keepdims=True)
    return d * lax.rsqrt(var + LN_EPS) * g + b


def _rms_gate(o, norm_g, gate):
    y = o * lax.rsqrt(jnp.mean(o * o, axis=-1, keepdims=True) + NORM_EPS)
    return y * norm_g * _silu(gate)


def _dot(a, b, precision=None):
    return jnp.dot(a, b, preferred_element_type=F32, precision=precision)


def _dot_inv(a, b):
    return _dot(a.astype(BF16), b.astype(BF16))


def _dot_nt(a, b, precision=None):
    return lax.dot_general(a, b, (((1,), (1,)), ((), ())), preferred_element_type=F32, precision=precision)


def _ada_kernel(c_ref, w_ref, b_ref, o_ref):
    cs = _silu(c_ref[...]).astype(BF16)
    o_ref[...] = _dot(cs, w_ref[...].astype(BF16)) + b_ref[...]


def _ada_mods(c_all, ada_w, ada_b):
    m = c_all.shape[0]
    n = N_MOD * D_MODEL
    tn = 1152
    return pl.pallas_call(
        _ada_kernel,
        grid=(DEPTH, n // tn),
        in_specs=[
            pl.BlockSpec((m, D_MODEL), lambda l, j: (0, 0)),
            pl.BlockSpec((None, D_MODEL, tn), lambda l, j: (l, 0, j)),
            pl.BlockSpec((None, 1, tn), lambda l, j: (l, 0, j)),
        ],
        out_specs=pl.BlockSpec((None, m, tn), lambda l, j: (l, 0, j)),
        out_shape=jax.ShapeDtypeStruct((DEPTH, m, n), F32),
        compiler_params=_params(("parallel", "parallel")),
        name="ada_mods",
    )(c_all, ada_w, ada_b.reshape(DEPTH, 1, n))


def _mod_spec(mods, k, tm, grid_rank):
    per_token = mods.shape[1] != 1
    rows = tm if per_token else 1
    if grid_rank == 3:
        return pl.BlockSpec((None, rows, D_MODEL), lambda b, i, j: (b, i if per_token else 0, k))
    return pl.BlockSpec((None, rows, D_MODEL), lambda b, i: (b, i if per_token else 0, k))


def _ffn_kernel(x_ref, sh_ref, sc_ref, g_ref, wu_ref, wd_ref, lg_ref, lb_ref, o_ref, acc_ref):
    x = x_ref[...]
    h = (x * (1.0 + sc_ref[...]) + sh_ref[...]).astype(BF16)
    for c in range(D_FF // FF_CHUNK):
        lo = c * FF_CHUNK
        gate = _dot(h, wu_ref[:, lo:lo + FF_CHUNK])
        up = _dot(h, wu_ref[:, D_FF + lo:D_FF + lo + FF_CHUNK])
        act = (_silu(gate) * up).astype(BF16)
        part = _dot(act, wd_ref[lo:lo + FF_CHUNK, :])
        if c == 0:
            acc_ref[...] = part
        else:
            acc_ref[...] += part
    y = DN_ALPHA * x + (0.5 * g_ref[...]) * acc_ref[...]
    o_ref[...] = _layer_norm(y, lg_ref[...], lb_ref[...])


def _ffn(x, mods, k0, w_up, w_down, ln_g, ln_b, layer, sub, tm):
    bsz, t, _ = x.shape
    tm = min(t, max(tm, FFN_TM))
    row = pl.BlockSpec((None, tm, D_MODEL), lambda b, i: (b, i, 0))
    vec = pl.BlockSpec((1, D_MODEL), lambda b, i: (0, 0))
    resident = pl.Buffered(1)
    ln_g = ln_g[layer, 2 * sub]
    ln_b = ln_b[layer, 2 * sub]
    return pl.pallas_call(
        _ffn_kernel,
        grid=(bsz, t // tm),
        in_specs=[
            row,
            _mod_spec(mods, k0, tm, 2), _mod_spec(mods, k0 + 1, tm, 2), _mod_spec(mods, k0 + 2, tm, 2),
            pl.BlockSpec((None, None, D_MODEL, 2 * D_FF), lambda b, i: (layer, sub, 0, 0), pipeline_mode=resident),
            pl.BlockSpec((None, None, D_FF, D_MODEL), lambda b, i: (layer, sub, 0, 0), pipeline_mode=resident),
            vec, vec,
        ],
        out_specs=row,
        out_shape=jax.ShapeDtypeStruct(x.shape, F32),
        scratch_shapes=[pltpu.VMEM((tm, D_MODEL), F32)],
        compiler_params=_params(("parallel", "parallel"), FFN_VMEM_LIMIT),
        name="ffn",
    )(x, mods, mods, mods, w_up, w_down, ln_g.reshape(1, D_MODEL), ln_b.reshape(1, D_MODEL))


def _inproj_kernel(x_ref, sh_ref, sc_ref, w_ref, o_ref):
    h = (x_ref[...] * (1.0 + sc_ref[...]) + sh_ref[...]).astype(BF16)
    for lo in range(0, w_ref.shape[1], PROJ_TILE):
        o_ref[:, lo:lo + PROJ_TILE] = _dot(h, w_ref[:, lo:lo + PROJ_TILE])


def _inproj(x, mods, k0, w, tm):
    bsz, t, _ = x.shape
    n = w.shape[1]
    return pl.pallas_call(
        _inproj_kernel,
        grid=(bsz, t // tm),
        in_specs=[
            pl.BlockSpec((None, tm, D_MODEL), lambda b, i: (b, i, 0)),
            _mod_spec(mods, k0, tm, 2), _mod_spec(mods, k0 + 1, tm, 2),
            pl.BlockSpec((D_MODEL, n), lambda b, i: (0, 0), pipeline_mode=pl.Buffered(1)),
        ],
        out_specs=pl.BlockSpec((None, tm, n), lambda b, i: (b, i, 0)),
        out_shape=jax.ShapeDtypeStruct((bsz, t, n), F32),
        compiler_params=_params(("parallel", "parallel")),
        name="inproj",
    )(x, mods, mods, w)


def _inproj_even_kernel(x_ref, sh_ref, sc_ref, w_ref, lb_ref, o_ref):
    h = (x_ref[...] * (1.0 + sc_ref[...]) + sh_ref[...]).astype(BF16)
    attn = A_Q + 2 * A_KV
    width = B_HEADS * B_DIM

    def proj(group):
        return _dot(h, w_ref[:, attn + group * width:attn + (group + 1) * width])

    o_ref[:, 0:attn] = _dot(h, w_ref[:, 0:attn])
    o_ref[:, attn:attn + width] = _silu(proj(0))
    lb = lb_ref[...]
    f = lb + (1.0 - lb) * jax.nn.sigmoid(proj(1))
    o_ref[:, attn + width:attn + 2 * width] = jnp.log(f)
    o_ref[:, attn + 2 * width:attn + 3 * width] = 1.0 - f
    o_ref[:, attn + 3 * width:attn + 4 * width] = proj(2)
    o_ref[:, attn + 4 * width:attn + 5 * width] = _silu(proj(3))


def _inproj_even(x, mods, k0, w, lower, tm):
    bsz, t, _ = x.shape
    n_in = w.shape[1]
    width = B_HEADS * B_DIM
    n_out = n_in + width
    return pl.pallas_call(
        _inproj_even_kernel,
        grid=(bsz, t // tm),
        in_specs=[
            pl.BlockSpec((None, tm, D_MODEL), lambda b, i: (b, i, 0)),
            _mod_spec(mods, k0, tm, 2), _mod_spec(mods, k0 + 1, tm, 2),
            pl.BlockSpec((D_MODEL, n_in), lambda b, i: (0, 0), pipeline_mode=pl.Buffered(1)),
            pl.BlockSpec((1, width), lambda b, i: (0, 0)),
        ],
        out_specs=pl.BlockSpec((None, tm, n_out), lambda b, i: (b, i, 0)),
        out_shape=jax.ShapeDtypeStruct((bsz, t, n_out), F32),
        compiler_params=_params(("parallel", "parallel")),
        name="inproj_even",
    )(x, mods, mods, w, lower.reshape(1, width))


def _inproj_gdn_kernel(x_ref, sh_ref, sc_ref, w_ref, cw_ref, o_ref, hist_ref, carry_ref):
    i = pl.program_id(1)
    tm = x_ref.shape[0]

    @pl.when(i == 0)
    def _():
        carry_ref[...] = jnp.zeros_like(carry_ref)

    h = (x_ref[...] * (1.0 + sc_ref[...]) + sh_ref[...]).astype(BF16)
    for sec in range(3):
        cs = slice(sec * C_QK, (sec + 1) * C_QK)
        raw = _dot(h, w_ref[:, cs])
        ext = jnp.concatenate([carry_ref[:, cs], raw], axis=0)
        z = ext * cw_ref[0:1, cs]
        for j in range(1, C_CONV):
            z = pltpu.roll(z, 1, 0) + ext * cw_ref[j:j + 1, cs]
        act = _silu(z[8:])
        if sec == 2:
            o_ref[:, cs] = act
        else:
            for hd in range(C_HEADS):
                a = act[:, hd * C_DIM:(hd + 1) * C_DIM]
                a = a * lax.rsqrt(jnp.sum(a * a, axis=-1, keepdims=True) + NORM_EPS)
                if sec == 0:
                    a = a * (C_DIM ** -0.5)
                o_ref[:, sec * C_QK + hd * C_DIM:sec * C_QK + (hd + 1) * C_DIM] = a
        carry_ref[:, cs] = raw[tm - 8:tm]
        hist_ref[:, cs] = raw[tm - 8:tm]
    gs = slice(C_CONV_DIM, C_CONV_DIM + C_QK)
    o_ref[:, gs] = _silu(_dot(h, w_ref[:, gs]))
    o_ref[:, C_CONV_DIM + C_QK:] = _dot(h, w_ref[:, C_CONV_DIM + C_QK:])


def _inproj_gdn(x, mods, k0, w, conv_w, tm):
    bsz, t, _ = x.shape
    n = w.shape[1]
    return pl.pallas_call(
        _inproj_gdn_kernel,
        grid=(bsz, t // tm),
        in_specs=[
            pl.BlockSpec((None, tm, D_MODEL), lambda b, i: (b, i, 0)),
            _mod_spec(mods, k0, tm, 2), _mod_spec(mods, k0 + 1, tm, 2),
            pl.BlockSpec((D_MODEL, n), lambda b, i: (0, 0), pipeline_mode=pl.Buffered(1)),
            pl.BlockSpec((C_CONV, C_CONV_DIM), lambda b, i: (0, 0)),
        ],
        out_specs=[
            pl.BlockSpec((None, tm, n), lambda b, i: (b, i, 0)),
            pl.BlockSpec((None, 8, C_CONV_DIM), lambda b, i: (b, 0, 0)),
        ],
        out_shape=[
            jax.ShapeDtypeStruct((bsz, t, n), F32),
            jax.ShapeDtypeStruct((bsz, 8, C_CONV_DIM), F32),
        ],
        scratch_shapes=[pltpu.VMEM((8, C_CONV_DIM), F32)],
        compiler_params=_params(("parallel", "arbitrary")),
        name="inproj_gdn",
    )(x, mods, mods, w, conv_w)


def _outproj_kernel(x_ref, o1_ref, o2_ref, g_ref, w1_ref, w2_ref, lg_ref, lb_ref, y_ref):
    mix = _dot(o1_ref[...].astype(BF16), w1_ref[...]) + _dot(o2_ref[...].astype(BF16), w2_ref[...])
    y = DN_ALPHA * x_ref[...] + g_ref[...] * mix
    y_ref[...] = _layer_norm(y, lg_ref[...], lb_ref[...])


def _outproj(x, o1, c1, o2, c2, mods, kg, w_out, ln_g, ln_b, tm):
    bsz, t, _ = x.shape
    half = D_MODEL // 2
    row = pl.BlockSpec((None, tm, D_MODEL), lambda b, i: (b, i, 0))
    vec = pl.BlockSpec((1, D_MODEL), lambda b, i: (0, 0))
    return pl.pallas_call(
        _outproj_kernel,
        grid=(bsz, t // tm),
        in_specs=[
            row,
            pl.BlockSpec((None, tm, half), lambda b, i: (b, i, c1)),
            pl.BlockSpec((None, tm, half), lambda b, i: (b, i, c2)),
            _mod_spec(mods, kg, tm, 2),
            pl.BlockSpec((half, D_MODEL), lambda b, i: (0, 0)),
            pl.BlockSpec((half, D_MODEL), lambda b, i: (1, 0)),
            vec, vec,
        ],
        out_specs=row,
        out_shape=jax.ShapeDtypeStruct(x.shape, F32),
        compiler_params=_params(("parallel", "parallel")),
        name="outproj",
    )(x, o1, o2, mods, w_out, w_out, ln_g.reshape(1, D_MODEL), ln_b.reshape(1, D_MODEL))


def _swa_kernel(sink_ref, q_ref, kc_ref, kp_ref, vc_ref, vp_ref, o_ref):
    i = pl.program_id(1)
    q = q_ref[...]
    kc = kc_ref[...].astype(BF16)
    kp = kp_ref[...].astype(BF16)
    vc = vc_ref[...].astype(BF16)
    vp = vp_ref[...].astype(BF16)
    r = lax.broadcasted_iota(jnp.int32, (WINDOW, WINDOW), 0)
    c = lax.broadcasted_iota(jnp.int32, (WINDOW, WINDOW), 1)
    dist_c = (r - c).astype(F32)
    dist_p = dist_c + float(WINDOW)
    valid_c = c <= r
    valid_p = c > r + jnp.where(i > 0, 0, WINDOW)
    scale = A_HEAD_DIM ** -0.5
    heads = range(A_HEADS)
    hsl = [slice(h * A_HEAD_DIM, (h + 1) * A_HEAD_DIM) for h in heads]
    ksl = [slice((h // A_GROUP) * A_HEAD_DIM, (h // A_GROUP + 1) * A_HEAD_DIM) for h in heads]
    slope = [2.0 ** (-8.0 * (h + 1) / A_HEADS) for h in heads]
    qh = [q[:, hsl[h]].astype(BF16) for h in heads]
    s_c = [_dot_nt(qh[h], kc[:, ksl[h]]) for h in heads]
    s_p = [_dot_nt(qh[h], kp[:, ksl[h]]) for h in heads]
    s_c = [jnp.where(valid_c, s_c[h] * scale - slope[h] * dist_c, -jnp.inf) for h in heads]
    s_p = [jnp.where(valid_p, s_p[h] * scale - slope[h] * dist_p, -jnp.inf) for h in heads]
    m = [jnp.maximum(jnp.maximum(jnp.max(s_c[h], axis=-1, keepdims=True), jnp.max(s_p[h], axis=-1, keepdims=True)),
                     sink_ref[h]) for h in heads]
    p_c = [jnp.exp(s_c[h] - m[h]) for h in heads]
    p_p = [jnp.exp(s_p[h] - m[h]) for h in heads]
    den = [jnp.sum(p_c[h], axis=-1, keepdims=True) + jnp.sum(p_p[h], axis=-1, keepdims=True)
           + jnp.exp(sink_ref[h] - m[h]) for h in heads]
    o = [_dot(p_c[h].astype(BF16), vc[:, ksl[h]]) + _dot(p_p[h].astype(BF16), vp[:, ksl[h]]) for h in heads]
    o_ref[...] = jnp.concatenate([o[h] / den[h] for h in heads], axis=1).astype(o_ref.dtype)


def _swa_prompt(proj, sinks):
    bsz, t, _ = proj.shape
    nb = t // WINDOW
    kcol = A_Q // LANES
    vcol = kcol + 1

    def cur(col):
        return pl.BlockSpec((None, WINDOW, LANES), lambda b, i: (b, i, col))

    def prev(col):
        return pl.BlockSpec((None, WINDOW, LANES), lambda b, i: (b, jnp.maximum(i - 1, 0), col))

    return pl.pallas_call(
        _swa_kernel,
        grid=(bsz, nb),
        in_specs=[
            pl.BlockSpec(memory_space=pltpu.SMEM),
            pl.BlockSpec((None, WINDOW, A_Q), lambda b, i: (b, i, 0)),
            cur(kcol), prev(kcol), cur(vcol), prev(vcol),
        ],
        out_specs=pl.BlockSpec((None, WINDOW, A_Q), lambda b, i: (b, i, 0)),
        out_shape=jax.ShapeDtypeStruct((bsz, t, A_Q), BF16),
        compiler_params=_params(("parallel", "parallel")),
        name="swa_prompt",
    )(sinks, proj, proj, proj, proj, proj)


def _split3(x):
    hi = x.astype(BF16)
    r1 = x - hi.astype(F32)
    mid = r1.astype(BF16)
    lo = (r1 - mid.astype(F32)).astype(BF16)
    return hi, mid, lo


def _hgrn_kernel(q0_ref, q1_ref, lf0_ref, lf1_ref, k0_ref, k1_ref, v0_ref, v1_ref, g0_ref, g1_ref, ng_ref,
                 o_ref, s_ref, st_ref):
    i = pl.program_id(1)

    @pl.when(i == 0)
    def _():
        st_ref[...] = jnp.zeros_like(st_ref)

    n = HGRN_BLOCK
    nsub = n // HGRN_SUB
    heads = range(B_HEADS)
    qb = jnp.concatenate([q0_ref[...], q1_ref[...]], axis=1)
    lf = jnp.concatenate([lf0_ref[...], lf1_ref[...]], axis=1)
    kk = jnp.concatenate([k0_ref[...], k1_ref[...]], axis=1)
    v = jnp.concatenate([v0_ref[...], v1_ref[...]], axis=1)
    r = lax.broadcasted_iota(jnp.int32, (n, n), 0)
    c = lax.broadcasted_iota(jnp.int32, (n, n), 1)
    tri = jnp.where((c <= r) & (c >= r - (r & (HGRN_SUB - 1))), 1.0, 0.0).astype(BF16)
    lf_hi, lf_mid, lf_lo = _split3(lf)
    a_all = _dot(tri, lf_hi) + (_dot(tri, lf_mid) + _dot(tri, lf_lo))
    a3 = a_all.reshape(nsub, HGRN_SUB, B_HEADS * B_DIM)
    a_last3 = a3[:, HGRN_SUB - 1:HGRN_SUB, :]
    qb3 = qb.reshape(a3.shape)
    kk3 = kk.reshape(a3.shape)
    v3 = v.reshape(a3.shape)
    qe3 = qb3 * jnp.exp(a3)
    kd3 = kk3 * jnp.exp(a_last3 - a3)
    dec3 = jnp.exp(a_last3)
    trow = lax.broadcasted_iota(jnp.int32, (1, HGRN_SUB, 1), 1)
    o_diag = []
    v_t = []
    for h in heads:
        hs = slice(h * B_DIM, (h + 1) * B_DIM)
        a_h, q_h, k_h, v_h = a3[:, :, hs], qb3[:, :, hs], kk3[:, :, hs], v3[:, :, hs]
        acc = jnp.zeros((nsub, HGRN_SUB, B_DIM), F32)
        for s in range(HGRN_SUB):
            e = jnp.exp(jnp.where(trow >= s, a_h - a_h[:, s:s + 1, :], -jnp.inf))
            col = jnp.sum(q_h * e * k_h[:, s:s + 1, :], axis=-1, keepdims=True)
            acc = acc + col * v_h[:, s:s + 1, :]
        o_diag.append(acc.reshape(n, B_DIM))
        v_t.append(v[:, hs].T.astype(BF16))
    state = [st_ref[h] for h in heads]
    o_state = [[] for _ in heads]
    for ch in range(nsub):
        sl = slice(ch * HGRN_SUB, (ch + 1) * HGRN_SUB)
        for h in heads:
            hs = slice(h * B_DIM, (h + 1) * B_DIM)
            o_state[h].append(_dot_nt(qe3[ch, :, hs].astype(BF16), state[h].astype(BF16)))
            state[h] = state[h] * dec3[ch, :, hs] + _dot(v_t[h][:, sl], kd3[ch, :, hs].astype(BF16))
    g = jnp.concatenate([g0_ref[...], g1_ref[...]], axis=1)
    for h in heads:
        hs = slice(h * B_DIM, (h + 1) * B_DIM)
        st_ref[h] = state[h]
        o = jnp.concatenate(o_state[h], axis=0) + o_diag[h]
        y = o * lax.rsqrt(jnp.mean(o * o, axis=-1, keepdims=True) + NORM_EPS)
        o_ref[:, hs] = (y * ng_ref[...] * g[:, hs]).astype(o_ref.dtype)

    @pl.when(i == pl.num_programs(1) - 1)
    def _():
        for h in heads:
            s_ref[h] = state[h].T


def _hgrn_prompt(proj, norm_g):
    bsz, t, _ = proj.shape
    nh = B_HEADS
    width = nh * B_DIM
    half = width // 2
    base = (A_Q + 2 * A_KV) // half

    def col(off):
        return pl.BlockSpec((None, HGRN_BLOCK, half), lambda b, i: (b, i, base + off))

    return pl.pallas_call(
        _hgrn_kernel,
        grid=(bsz, t // HGRN_BLOCK),
        in_specs=[
            col(0), col(1), col(2), col(3), col(4), col(5), col(6), col(7), col(8), col(9),
            pl.BlockSpec((1, B_DIM), lambda b, i: (0, 0)),
        ],
        out_specs=[
            pl.BlockSpec((None, HGRN_BLOCK, width), lambda b, i: (b, i, 0)),
            pl.BlockSpec((None, nh, B_DIM, B_DIM), lambda b, i: (b, 0, 0, 0)),
        ],
        out_shape=[
            jax.ShapeDtypeStruct((bsz, t, width), BF16),
            jax.ShapeDtypeStruct((bsz, nh, B_DIM, B_DIM), F32),
        ],
        scratch_shapes=[pltpu.VMEM((nh, B_DIM, B_DIM), F32)],
        compiler_params=_params(("parallel", "arbitrary")),
        name="hgrn_prompt",
    )(*([proj] * 10), norm_g.reshape(1, B_DIM))


def _gdn_kernel(q_ref, k_ref, v_ref, gate_ref, ab_ref, alog_ref, dtb_ref, ng_ref, o_ref, s_out_ref, s_ref):
    i = pl.program_id(1)
    n = GDN_CHUNK

    @pl.when(i == 0)
    def _():
        s_ref[...] = jnp.zeros_like(s_ref)

    ab = ab_ref[...]
    log_alpha = -jnp.exp(alog_ref[...]) * _softplus(ab + dtb_ref[...])
    beta_all = jax.nn.sigmoid(ab)
    r = lax.broadcasted_iota(jnp.int32, (n, n), 0)
    c = lax.broadcasted_iota(jnp.int32, (n, n), 1)
    causal = c <= r
    strict = c < r
    tri = jnp.where(causal, 1.0, 0.0).astype(F32)
    eye = jnp.where(r == c, 1.0, 0.0).astype(F32)
    base_mask = (r >> GDN_BASE_LOG) == (c >> GDN_BASE_LOG)
    level_masks = [((r >> (k + 1)) == (c >> (k + 1))) & ((r >> k) != (c >> k))
                   for k in range(GDN_BASE_LOG, GDN_CHUNK_LOG)]
    g_cum =_dot(tri, log_alpha, HIGHEST)
    g_cum_t = g_cum.T

    heads = range(C_HEADS)
    qn, kn, kn_b, vh, g_col, beta, dec_incl, e_g = [], [], [], [], [], [], [], []
    for h in heads:
        hs = slice(h * C_DIM, (h + 1) * C_DIM)
        qn.append(q_ref[:, hs])
        kn.append(k_ref[:, hs])
        vh.append(v_ref[:, hs])
        kn_b.append(kn[h].astype(BF16))
        g_col.append(g_cum[:, h:h + 1])
        beta.append(beta_all[:, C_HEADS + h:C_HEADS + h + 1])
        dec_incl.append(jnp.exp(jnp.where(causal, g_col[h] - g_cum_t[h:h + 1, :], -jnp.inf)))
        e_g.append(jnp.exp(g_col[h]))
    kkt = [_dot_nt(kn_b[h], kn_b[h]) for h in heads]
    qk = [_dot_nt(qn[h].astype(BF16), kn_b[h]) for h in heads]
    l_mat = [beta[h] * kkt[h] * jnp.where(strict, dec_incl[h], 0.0) for h in heads]
    a_pow = [jnp.where(base_mask, -l_mat[h], 0.0) for h in heads]
    t_inv = [eye + a_pow[h] for h in heads]
    for _ in range(GDN_BASE_LOG - 1):
        a_pow = [_dot_inv(a_pow[h], a_pow[h]) for h in heads]
        t_inv = [t_inv[h] + _dot_inv(t_inv[h], a_pow[h]) for h in heads]
    for lm in level_masks:
        x = [_dot_inv(jnp.where(lm, l_mat[h], 0.0), t_inv[h]) for h in heads]
        t_inv = [t_inv[h] - _dot_inv(t_inv[h], x[h]) for h in heads]
    uw = [_dot_inv(t_inv[h], jnp.concatenate([beta[h] * vh[h], (beta[h] * e_g[h]) * kn[h]], axis=1))
          for h in heads]
    s_old = [s_ref[h] for h in heads]
    s_b = [s_old[h].astype(BF16) for h in heads]
    ws = [_dot(uw[h][:, C_DIM:].astype(BF16), s_b[h]) for h in heads]
    qs = [_dot((qn[h] * e_g[h]).astype(BF16), s_b[h]) for h in heads]
    delta_b = [(uw[h][:, :C_DIM] - ws[h]).astype(BF16) for h in heads]
    o = [qs[h] + _dot((qk[h] * dec_incl[h]).astype(BF16), delta_b[h]) for h in heads]
    for h in heads:
        g_last = g_col[h][n - 1:n]
        k_dec = kn[h] * jnp.exp(g_last - g_col[h])
        s_ref[h] = jnp.exp(g_last) * s_old[h] + _dot(k_dec.T.astype(BF16), delta_b[h])
    for h in heads:
        hs = slice(h * C_DIM, (h + 1) * C_DIM)
        y = o[h] * lax.rsqrt(jnp.mean(o[h] * o[h], axis=-1, keepdims=True) + NORM_EPS)
        o_ref[:, hs] = (y * ng_ref[...] * gate_ref[:, hs]).astype(o_ref.dtype)

    @pl.when(i == pl.num_programs(1) - 1)
    def _():
        s_out_ref[...] = s_ref[...]


def _gdn_prompt(proj, alog_row, dtb_row, norm_g):
    bsz, t, _ = proj.shape
    n = GDN_CHUNK

    def col(cidx):
        return pl.BlockSpec((None, n, C_QK), lambda b, i: (b, i, cidx))

    one = pl.BlockSpec((1, LANES), lambda b, i: (0, 0))
    return pl.pallas_call(
        _gdn_kernel,
        grid=(bsz, t // n),
        in_specs=[
            col(0), col(1), col(2), col(3),
            pl.BlockSpec((None, n, LANES), lambda b, i: (b, i, (C_CONV_DIM + C_QK) // LANES)),
            one, one, one,
        ],
        out_specs=[
            pl.BlockSpec((None, n, C_QK), lambda b, i: (b, i, 0)),
            pl.BlockSpec((None, C_HEADS, C_DIM, C_DIM), lambda b, i: (b, 0, 0, 0)),
        ],
        out_shape=[
            jax.ShapeDtypeStruct((bsz, t, C_QK), BF16),
            jax.ShapeDtypeStruct((bsz, C_HEADS, C_DIM, C_DIM), F32),
        ],
        scratch_shapes=[pltpu.VMEM((C_HEADS, C_DIM, C_DIM), F32)],
        compiler_params=_params(("parallel", "arbitrary")),
        name="gdn_prompt",
    )(proj, proj, proj, proj, proj, alog_row, dtb_row, norm_g.reshape(1, C_DIM))


def _swa_dec_kernel(sink_ref, q_ref, kn_ref, vn_ref, ck_ref, cv_ref, o_ref, ok_ref, ov_ref):
    batch = range(q_ref.shape[0])
    row = lax.broadcasted_iota(jnp.int32, (WINDOW, LANES), 0)
    head = lax.broadcasted_iota(jnp.int32, (A_HEADS, WINDOW), 0)
    dist = (WINDOW - 1 - lax.broadcasted_iota(jnp.int32, (A_HEADS, WINDOW), 1)).astype(F32)
    scale = A_HEAD_DIM ** -0.5
    slope = jnp.zeros((A_HEADS, WINDOW), F32)
    sink = jnp.zeros((A_HEADS, 1), F32)
    for h in range(A_HEADS):
        slope = jnp.where(head == h, 2.0 ** (-8.0 * (h + 1) / A_HEADS), slope)
        sink = jnp.where(head[:, 0:1] == h, sink_ref[h], sink)
    bias = slope * dist
    first_kv = head[:, 0:A_HEAD_DIM] < A_GROUP
    keys = [jnp.where(row == WINDOW - 1, kn_ref[b], pltpu.roll(ck_ref[b], WINDOW - 1, 0)) for b in batch]
    vals = [jnp.where(row == WINDOW - 1, vn_ref[b], pltpu.roll(cv_ref[b], WINDOW - 1, 0)) for b in batch]
    for b in batch:
        ok_ref[b] = keys[b]
        ov_ref[b] = vals[b]
    q8 = [jnp.concatenate([jnp.where(first_kv, q_ref[b], 0.0), jnp.where(first_kv, 0.0, q_ref[b])],
                          axis=1).astype(BF16) for b in batch]
    s = [_dot_nt(q8[b], keys[b].astype(BF16)) * scale - bias for b in batch]
    m = [jnp.maximum(jnp.max(s[b], axis=-1, keepdims=True), sink) for b in batch]
    p = [jnp.exp(s[b] - m[b]) for b in batch]
    den = [jnp.sum(p[b], axis=-1, keepdims=True) + jnp.exp(sink - m[b]) for b in batch]
    o = [_dot(p[b].astype(BF16), vals[b].astype(BF16)) for b in batch]
    for b in batch:
        o_ref[b] = jnp.where(first_kv, o[b][:, :A_HEAD_DIM], o[b][:, A_HEAD_DIM:]) / den[b]


def _swa_decode(proj2, cache_k, cache_v, sinks):
    nbatch = proj2.shape[0]
    bb = 8
    q3 = proj2[:, :A_Q].reshape(nbatch, A_HEADS, A_HEAD_DIM)
    kn = proj2[:, A_Q:A_Q + A_KV].reshape(nbatch, 1, A_KV)
    vn = proj2[:, A_Q + A_KV:A_Q + 2 * A_KV].reshape(nbatch, 1, A_KV)
    ck = cache_k.reshape(nbatch, WINDOW, A_KV)
    cv = cache_v.reshape(nbatch, WINDOW, A_KV)
    qspec = pl.BlockSpec((bb, A_HEADS, A_HEAD_DIM), lambda j: (j, 0, 0))
    nspec = pl.BlockSpec((bb, 1, A_KV), lambda j: (j, 0, 0))
    cspec = pl.BlockSpec((bb, WINDOW, A_KV), lambda j: (j, 0, 0))
    o, nk, nv = pl.pallas_call(
        _swa_dec_kernel,
        grid=(nbatch // bb,),
        in_specs=[pl.BlockSpec(memory_space=pltpu.SMEM), qspec, nspec, nspec, cspec, cspec],
        out_specs=[qspec, cspec, cspec],
        out_shape=[
            jax.ShapeDtypeStruct((nbatch, A_HEADS, A_HEAD_DIM), F32),
            jax.ShapeDtypeStruct((nbatch, WINDOW, A_KV), F32),
            jax.ShapeDtypeStruct((nbatch, WINDOW, A_KV), F32),
        ],
        compiler_params=_params(("parallel",)),
        name="swa_decode",
    )(sinks, q3, kn, vn, ck, cv)
    shape5 = (nbatch, WINDOW, A_KV_HEADS, A_HEAD_DIM)
    return o.reshape(nbatch, A_Q), nk.reshape(shape5), nv.reshape(shape5)


def _cols(rows, heads):
    nbatch = rows.shape[0]
    x = rows.reshape(nbatch // DEC_BLOCK, DEC_BLOCK, heads, LANES)
    return x.transpose(2, 0, 3, 1)


def _hgrn_dec_kernel(qt_ref, ft_ref, lb_ref, v_ref, g_ref, ng_ref, s_ref, o_ref, so_ref, orow_ref):
    lb = lb_ref[...]
    qc = _silu(qt_ref[...])
    f = lb + (1.0 - lb) * jax.nn.sigmoid(ft_ref[...])
    kk = 1.0 - f
    v = v_ref[...]
    for bb in range(DEC_BLOCK):
        s_new = f[:, bb:bb + 1] * s_ref[bb] + kk[:, bb:bb + 1] * v[bb:bb + 1, :]
        so_ref[bb] = s_new
        orow_ref[bb:bb + 1, :] = jnp.sum(qc[:, bb:bb + 1] * s_new, axis=0, keepdims=True)
    o_ref[...] = _rms_gate(orow_ref[...], ng_ref[...], g_ref[...])


def _hgrn_decode(proj2, state, lower, norm_g):
    nbatch = proj2.shape[0]
    nh = B_HEADS
    base = A_Q + 2 * A_KV
    width = nh * B_DIM
    qt = _cols(proj2[:, base:base + width], nh)
    ft = _cols(proj2[:, base + width:base + 2 * width], nh)
    colspec = pl.BlockSpec((None, None, B_DIM, DEC_BLOCK), lambda h, j: (h, j, 0, 0))
    sspec = pl.BlockSpec((DEC_BLOCK, None, B_DIM, B_DIM), lambda h, j: (j, h, 0, 0))
    cb = base // LANES
    o, s_new = pl.pallas_call(
        _hgrn_dec_kernel,
        grid=(nh, nbatch // DEC_BLOCK),
        in_specs=[
            colspec, colspec,
            pl.BlockSpec((None, B_DIM, 1), lambda h, j: (h, 0, 0)),
            pl.BlockSpec((DEC_BLOCK, B_DIM), lambda h, j: (j, cb + 2 * nh + h)),
            pl.BlockSpec((DEC_BLOCK, B_DIM), lambda h, j: (j, cb + 3 * nh + h)),
            pl.BlockSpec((1, B_DIM), lambda h, j: (0, 0)),
            sspec,
        ],
        out_specs=[pl.BlockSpec((DEC_BLOCK, B_DIM), lambda h, j: (j, h)), sspec],
        out_shape=[
            jax.ShapeDtypeStruct((nbatch, width), F32),
            jax.ShapeDtypeStruct(state.shape, F32),
        ],
        scratch_shapes=[pltpu.VMEM((DEC_BLOCK, B_DIM), F32)],
        compiler_params=_params(("parallel", "parallel")),
        name="hgrn_decode",
    )(qt, ft, lower.reshape(nh, B_DIM, 1), proj2, proj2, norm_g.reshape(1, B_DIM), state)
    return o, s_new


def _gdn_prep_kernel(p_ref, h0_ref, h1_ref, h2_ref, cw_ref, alog_ref, dtb_ref, q_ref, k_ref, v_ref, ab_ref):
    acc = (h0_ref[...] * cw_ref[0:1, :] + h1_ref[...] * cw_ref[1:2, :] + h2_ref[...] * cw_ref[2:3, :]
           + p_ref[:, 0:C_CONV_DIM] * cw_ref[3:4, :])
    qkv = _silu(acc)
    for h in range(C_HEADS):
        hs = slice(h * C_DIM, (h + 1) * C_DIM)
        qh = qkv[:, h * C_DIM:(h + 1) * C_DIM]
        kh = qkv[:, C_QK + h * C_DIM:C_QK + (h + 1) * C_DIM]
        q_ref[:, hs] = qh * lax.rsqrt(jnp.sum(qh * qh, axis=-1, keepdims=True) + NORM_EPS) * (C_DIM ** -0.5)
        k_ref[:, hs] = kh * lax.rsqrt(jnp.sum(kh * kh, axis=-1, keepdims=True) + NORM_EPS)
    v_ref[...] = qkv[:, 2 * C_QK:3 * C_QK]
    ab = p_ref[:, C_CONV_DIM + C_QK:C_CONV_DIM + C_QK + LANES]
    alpha = jnp.exp(-jnp.exp(alog_ref[...]) * _softplus(ab + dtb_ref[...]))
    beta = jax.nn.sigmoid(ab)
    lane = lax.broadcasted_iota(jnp.int32, ab.shape, 1)
    ab_ref[...] = jnp.where(lane < C_HEADS, alpha, beta)


def _gdn_dec_kernel(qt_ref, kt_ref, v_ref, gate_ref, al_ref, be_ref, ng_ref, s_ref, o_ref, so_ref, orow_ref):
    qt = qt_ref[...]
    kt = kt_ref[...]
    v = v_ref[...]
    al = al_ref[...]
    be = be_ref[...]
    for bb in range(DEC_BLOCK):
        s = s_ref[bb]
        kc = kt[:, bb:bb + 1]
        a = al[bb:bb + 1, :]
        ks = jnp.sum(kc * s, axis=0, keepdims=True)
        delta = be[bb:bb + 1, :] * (v[bb:bb + 1, :] - a * ks)
        s_new = a * s + kc * delta
        so_ref[bb] = s_new
        orow_ref[bb:bb + 1, :] = jnp.sum(qt[:, bb:bb + 1] * s_new, axis=0, keepdims=True)
    o_ref[...] = _rms_gate(orow_ref[...], ng_ref[...], gate_ref[...])


def _gdn_decode(proj2, hist, state, conv_w, alog_row, dtb_row, norm_g):
    nbatch = proj2.shape[0]
    nh = C_HEADS
    full = lambda shape: pl.BlockSpec(shape, lambda: tuple(0 for _ in shape))
    hspec = full((nbatch, C_CONV_DIM))
    qn, kn, vc, ab = pl.pallas_call(
        _gdn_prep_kernel,
        grid=(),
        in_specs=[
            full((nbatch, ODD_IN_PAD)), hspec, hspec, hspec,
            full((C_CONV, C_CONV_DIM)), full((1, LANES)), full((1, LANES)),
        ],
        out_specs=[full((nbatch, C_QK)), full((nbatch, C_QK)), full((nbatch, C_QK)), full((nbatch, LANES))],
        out_shape=[
            jax.ShapeDtypeStruct((nbatch, C_QK), F32),
            jax.ShapeDtypeStruct((nbatch, C_QK), F32),
            jax.ShapeDtypeStruct((nbatch, C_QK), F32),
            jax.ShapeDtypeStruct((nbatch, LANES), F32),
        ],
        compiler_params=pltpu.CompilerParams(vmem_limit_bytes=VMEM_LIMIT),
        name="gdn_decode_prep",
    )(proj2, hist[:, 0], hist[:, 1], hist[:, 2], conv_w, alog_row, dtb_row)
    qt = _cols(qn, nh)
    kt = _cols(kn, nh)
    al = jnp.broadcast_to(ab[:, :nh].T[:, :, None], (nh, nbatch, LANES))
    be = jnp.broadcast_to(ab[:, nh:2 * nh].T[:, :, None], (nh, nbatch, LANES))
    colspec = pl.BlockSpec((None, None, C_DIM, DEC_BLOCK), lambda h, j: (h, j, 0, 0))
    sspec = pl.BlockSpec((DEC_BLOCK, None, C_DIM, C_DIM), lambda h, j: (j, h, 0, 0))
    rspec = pl.BlockSpec((None, DEC_BLOCK, LANES), lambda h, j: (h, j, 0))
    gcol = C_CONV_DIM // LANES
    o, s_new = pl.pallas_call(
        _gdn_dec_kernel,
        grid=(nh, nbatch // DEC_BLOCK),
        in_specs=[
            colspec, colspec,
            pl.BlockSpec((DEC_BLOCK, C_DIM), lambda h, j: (j, h)),
            pl.BlockSpec((DEC_BLOCK, C_DIM), lambda h, j: (j, gcol + h)),
            rspec, rspec,
            pl.BlockSpec((1, C_DIM), lambda h, j: (0, 0)),
            sspec,
        ],
        out_specs=[pl.BlockSpec((DEC_BLOCK, C_DIM), lambda h, j: (j, h)), sspec],
        out_shape=[
            jax.ShapeDtypeStruct((nbatch, C_QK), F32),
            jax.ShapeDtypeStruct(state.shape, F32),
        ],
        scratch_shapes=[pltpu.VMEM((DEC_BLOCK, C_DIM), F32)],
        compiler_params=_params(("parallel", "parallel")),
        name="gdn_decode",
    )(qt, kt, vc, proj2, al, be, norm_g.reshape(1, C_DIM), state)
    return o, s_new


def _trunk(x, mods, caches, p, tm):
    decode = caches is not None
    m = mods[0]
    ffn = functools.partial(_ffn, w_up=p["w_up"], w_down=p["w_down"], ln_g=p["ln_g"], ln_b=p["ln_b"], tm=tm)
    x = ffn(x, m, 0, layer=0, sub=0)
    if decode:
        proj = _inproj(x, m, 3, p["even_in"], tm)
        o_a, new_k, new_v = _swa_decode(proj[0], caches[0][0], caches[1][0], p["sinks"])
        o_b, s_hgrn = _hgrn_decode(proj[0], caches[2][0], p["lower"], p["hgrn_norm_g"])
        o_a, o_b = o_a[None], o_b[None]
    else:
        bsz = x.shape[0]
        proj = _inproj_even(x, m, 3, p["even_in"], p["lower"], tm)
        o_a = _swa_prompt(proj, p["sinks"])
        new_k = proj[:, -WINDOW:, A_Q:A_Q + A_KV].reshape(bsz, WINDOW, A_KV_HEADS, A_HEAD_DIM)
        new_v = proj[:, -WINDOW:, A_Q + A_KV:A_Q + 2 * A_KV].reshape(bsz, WINDOW, A_KV_HEADS, A_HEAD_DIM)
        o_b, s_hgrn = _hgrn_prompt(proj, p["hgrn_norm_g"])
    x = _outproj(x, o_a, 0, o_b, 0, m, 5, p["even_out"], p["ln_g"][0, 1], p["ln_b"][0, 1], tm)
    x = ffn(x, m, 6, layer=0, sub=1)
    m = mods[1]
    x = ffn(x, m, 0, layer=1, sub=0)
    if decode:
        proj = _inproj(x, m, 3, p["odd_in"], tm)
        hist = caches[4][0]
        o_c, s_gdn = _gdn_decode(proj[0], hist, caches[3][0], p["conv_w"], p["alog_row"], p["dtb_row"],
                                 p["gdn_norm_g"])
        o_c = o_c[None]
        new_hist = jnp.concatenate([hist[:, 1:], proj[0][:, None, :C_CONV_DIM]], axis=1)
    else:
        proj, last_rows = _inproj_gdn(x, m, 3, p["odd_in"], p["conv_w"], tm)
        o_c, s_gdn = _gdn_prompt(proj, p["alog_row"], p["dtb_row"], p["gdn_norm_g"])
        new_hist = last_rows[:, -(C_CONV - 1):]
    x = _outproj(x, o_c, 0, o_c, 1, m, 5, p["odd_out"], p["ln_g"][1, 1], p["ln_b"][1, 1], tm)
    x = ffn(x, m, 6, layer=1, sub=1)
    return x, new_k[None], new_v[None], s_hgrn[None], s_gdn[None], new_hist[None]


def kernel(x_prompt, x_sample, cache_swa_k, cache_swa_v, state_hgrn, state_gdn, state_gdn_conv, c_prompt, c_sample, ada_w, ada_b, ln_g, ln_b, ffn_w_up, ffn_w_down, even_w_in, even_w_out, swa_sinks, hgrn_norm_g, hgrn_lb_logits, odd_w_in, odd_w_out, gdn_conv_w, gdn_a_log, gdn_dt_bias, gdn_norm_g):
    n_prompt = c_prompt.shape[0]
    n_sample = c_sample.shape[0]
    pad_rows = (-(n_prompt + n_sample)) % 8
    c_all = jnp.concatenate([c_prompt, c_sample, jnp.zeros((pad_rows, D_MODEL), F32)], axis=0)
    mods = _ada_mods(c_all, ada_w, ada_b)
    mods_p = mods[:, :n_prompt].reshape(DEPTH, n_prompt, 1, N_MOD * D_MODEL)
    mods_s = mods[:, n_prompt:n_prompt + n_sample].reshape(DEPTH, 1, n_sample, N_MOD * D_MODEL)

    probs = jax.nn.softmax(hgrn_lb_logits.astype(F32), axis=0)
    lower = (jnp.cumsum(probs, axis=0)[1:] - probs[0])[0]

    def lane_row(v):
        return jnp.pad(v.astype(F32), (0, LANES - v.shape[0])).reshape(1, LANES)

    p = dict(
        w_up=ffn_w_up.astype(BF16), w_down=ffn_w_down.astype(BF16),
        ln_g=ln_g, ln_b=ln_b,
        even_in=even_w_in[0].astype(BF16), even_out=even_w_out[0].astype(BF16),
        odd_in=jnp.concatenate([odd_w_in[0].astype(BF16), jnp.zeros((D_MODEL, ODD_IN_PAD - ODD_IN), BF16)], axis=1),
        odd_out=odd_w_out[0].astype(BF16),
        sinks=swa_sinks[0], lower=lower, hgrn_norm_g=hgrn_norm_g[0],
        conv_w=gdn_conv_w[0], alog_row=lane_row(gdn_a_log[0]), dtb_row=lane_row(gdn_dt_bias[0]),
        gdn_norm_g=gdn_norm_g[0],
    )
    y_p, p_k, p_v, p_hgrn, p_gdn, p_conv = _trunk(x_prompt, mods_p, None, p, 512)
    caches = (cache_swa_k, cache_swa_v, state_hgrn, state_gdn, state_gdn_conv)
    x_s = x_sample.reshape(1, n_sample, D_MODEL)
    y_s, s_k, s_v, s_hgrn, s_gdn, s_conv = _trunk(x_s, mods_s, caches, p, n_sample)
    y_s = y_s.reshape(n_sample, 1, D_MODEL)
    return (y_p, y_s, p_k, p_v, p_hgrn, p_gdn, p_conv, s_k, s_v, s_hgrn, s_gdn, s_conv)
```

```python
import functools

import jax
import jax.numpy as jnp
from jax import lax
from jax.experimental import pallas as pl
from jax.experimental.pallas import tpu as pltpu

F32 = jnp.float32
BF16 = jnp.bfloat16
HIGHEST = lax.Precision.HIGHEST

D_MODEL = 1024
DEPTH = 2
WINDOW = 128
A_HEADS = 8
A_KV_HEADS = 2
A_GROUP = A_HEADS // A_KV_HEADS
A_HEAD_DIM = 64
A_Q = A_HEADS * A_HEAD_DIM
A_KV = A_KV_HEADS * A_HEAD_DIM
B_HEADS = 4
B_DIM = 128
EVEN_IN = 2816
C_HEADS = 8
C_DIM = 128
C_QK = C_HEADS * C_DIM
C_CONV = 4
C_CONV_DIM = 3 * C_QK
ODD_IN = C_CONV_DIM + C_QK + 2 * C_HEADS
ODD_IN_PAD = 4224
D_FF = 2816
N_MOD = 9
DN_ALPHA = (2 * DEPTH) ** 0.25
LN_EPS = 1e-5
NORM_EPS = 1e-6
LANES = 128

FF_CHUNK = 256
FFN_TM = 1024
FFN_VMEM_LIMIT = 56 << 20
PROJ_TILE = 1408
HGRN_BLOCK = 128
HGRN_SUB = 8
GDN_CHUNK_LOG = 7
GDN_CHUNK = 1 << GDN_CHUNK_LOG
GDN_BASE_LOG = 3
DEC_BLOCK = 64
VMEM_LIMIT = 48 << 20


def _params(sem, vmem=VMEM_LIMIT):
    return pltpu.CompilerParams(dimension_semantics=sem, vmem_limit_bytes=vmem)


def _silu(x):
    return x * jax.nn.sigmoid(x)


def _softplus(x):
    return jnp.maximum(x, 0.0) + jnp.log(1.0 + jnp.exp(-jnp.abs(x)))


def _layer_norm(y, g, b):
    mu = jnp.mean(y, axis=-1, keepdims=True)
    d = y - mu
    var = jnp.mean(d * d, axis=-1, keepdims=True)
    return d * lax.rsqrt(var + LN_EPS) * g + b


def _rms_gate(o, norm_g, gate):
    y = o * lax.rsqrt(jnp.mean(o * o, axis=-1, keepdims=True) + NORM_EPS)
    return y * norm_g * _silu(gate)


def _dot(a, b, precision=None):
    return jnp.dot(a, b, preferred_element_type=F32, precision=precision)


def _dot_inv(a, b):
    return _dot(a.astype(BF16), b.astype(BF16))


def _dot_nt(a, b, precision=None):
    return lax.dot_general(a, b, (((1,), (1,)), ((), ())), preferred_element_type=F32, precision=precision)


def _ada_kernel(c_ref, w_ref, b_ref, o_ref):
    cs = _silu(c_ref[...]).astype(BF16)
    o_ref[...] = _dot(cs, w_ref[...].astype(BF16)) + b_ref[...]


def _ada_mods(c_all, ada_w, ada_b):
    m = c_all.shape[0]
    n = N_MOD * D_MODEL
    tn = 1152
    return pl.pallas_call(
        _ada_kernel,
        grid=(DEPTH, n // tn),
        in_specs=[
            pl.BlockSpec((m, D_MODEL), lambda l, j: (0, 0)),
            pl.BlockSpec((None, D_MODEL, tn), lambda l, j: (l, 0, j)),
            pl.BlockSpec((None, 1, tn), lambda l, j: (l, 0, j)),
        ],
        out_specs=pl.BlockSpec((None, m, tn), lambda l, j: (l, 0, j)),
        out_shape=jax.ShapeDtypeStruct((DEPTH, m, n), F32),
        compiler_params=_params(("parallel", "parallel")),
        name="ada_mods",
    )(c_all, ada_w, ada_b.reshape(DEPTH, 1, n))


def _mod_spec(mods, k, tm, grid_rank):
    per_token = mods.shape[1] != 1
    rows = tm if per_token else 1
    if grid_rank == 3:
        return pl.BlockSpec((None, rows, D_MODEL), lambda b, i, j: (b, i if per_token else 0, k))
    return pl.BlockSpec((None, rows, D_MODEL), lambda b, i: (b, i if per_token else 0, k))


def _ffn_body(x, sh_ref, sc_ref, g_ref, wu_ref, wd_ref, lg_ref, lb_ref, o_ref):
    h = (x * (1.0 + sc_ref[...]) + sh_ref[...]).astype(BF16)
    for c in range(D_FF // FF_CHUNK):
        lo = c * FF_CHUNK
        gate = _dot(h, wu_ref[:, lo:lo + FF_CHUNK])
        up = _dot(h, wu_ref[:, D_FF + lo:D_FF + lo + FF_CHUNK])
        act = (_silu(gate) * up).astype(BF16)
        part = _dot(act, wd_ref[lo:lo + FF_CHUNK, :])
        if c == 0:
            o_ref[...] = part
        else:
            o_ref[...] += part
    y = DN_ALPHA * x + (0.5 * g_ref[...]) * o_ref[...]
    o_ref[...] = _layer_norm(y, lg_ref[...], lb_ref[...])


def _ffn_kernel(x_ref, sh_ref, sc_ref, g_ref, wu_ref, wd_ref, lg_ref, lb_ref, o_ref):
    _ffn_body(x_ref[...], sh_ref, sc_ref, g_ref, wu_ref, wd_ref, lg_ref, lb_ref, o_ref)


def _mix_ffn_kernel(x_ref, o1_ref, o2_ref, gm_ref, w1_ref, w2_ref, lgm_ref, lbm_ref,
                    sh_ref, sc_ref, g_ref, wu_ref, wd_ref, lg_ref, lb_ref, o_ref):
    mix = _dot(o1_ref[...].astype(BF16), w1_ref[...]) + _dot(o2_ref[...].astype(BF16), w2_ref[...])
    x1 = _layer_norm(DN_ALPHA * x_ref[...] + gm_ref[...] * mix, lgm_ref[...], lbm_ref[...])
    _ffn_body(x1, sh_ref, sc_ref, g_ref, wu_ref, wd_ref, lg_ref, lb_ref, o_ref)


def _ffn(x, mods, k0, w_up, w_down, ln_g, ln_b, layer, sub, tm, mixer=None):
    bsz, t, _ = x.shape
    tm = min(t, max(tm, FFN_TM))
    half = D_MODEL // 2
    row = pl.BlockSpec((None, tm, D_MODEL), lambda b, i: (b, i, 0))
    vec = pl.BlockSpec((1, D_MODEL), lambda b, i: (0, 0))
    resident = pl.Buffered(1)
    ffn_specs = [
        _mod_spec(mods, k0, tm, 2), _mod_spec(mods, k0 + 1, tm, 2), _mod_spec(mods, k0 + 2, tm, 2),
        pl.BlockSpec((None, None, D_MODEL, 2 * D_FF), lambda b, i: (layer, sub, 0, 0), pipeline_mode=resident),
        pl.BlockSpec((None, None, D_FF, D_MODEL), lambda b, i: (layer, sub, 0, 0), pipeline_mode=resident),
        vec, vec,
    ]
    ffn_args = (mods, mods, mods, w_up, w_down,
                ln_g[layer, 2 * sub].reshape(1, D_MODEL), ln_b[layer, 2 * sub].reshape(1, D_MODEL))
    if mixer is None:
        body, specs, args = _ffn_kernel, [row] + ffn_specs, (x,) + ffn_args
    else:
        o1, c1, o2, c2, w_out, gate_k = mixer
        body = _mix_ffn_kernel
        specs = [
            row,
            pl.BlockSpec((None, tm, half), lambda b, i: (b, i, c1)),
            pl.BlockSpec((None, tm, half), lambda b, i: (b, i, c2)),
            _mod_spec(mods, gate_k, tm, 2),
            pl.BlockSpec((half, D_MODEL), lambda b, i: (0, 0), pipeline_mode=resident),
            pl.BlockSpec((half, D_MODEL), lambda b, i: (1, 0), pipeline_mode=resident),
            vec, vec,
        ] + ffn_specs
        args = (x, o1, o2, mods, w_out, w_out,
                ln_g[layer, 1].reshape(1, D_MODEL), ln_b[layer, 1].reshape(1, D_MODEL)) + ffn_args
    return pl.pallas_call(
        body,
        grid=(bsz, t // tm),
        in_specs=specs,
        out_specs=row,
        out_shape=jax.ShapeDtypeStruct(x.shape, F32),
        compiler_params=_params(("parallel", "parallel"), FFN_VMEM_LIMIT),
        name="ffn" if mixer is None else "mix_ffn",
    )(*args)


def _inproj_kernel(x_ref, sh_ref, sc_ref, w_ref, o_ref):
    h = (x_ref[...] * (1.0 + sc_ref[...]) + sh_ref[...]).astype(BF16)
    for lo in range(0, w_ref.shape[1], PROJ_TILE):
        o_ref[:, lo:lo + PROJ_TILE] = _dot(h, w_ref[:, lo:lo + PROJ_TILE])


def _inproj(x, mods, k0, w, tm):
    bsz, t, _ = x.shape
    n = w.shape[1]
    return pl.pallas_call(
        _inproj_kernel,
        grid=(bsz, t // tm),
        in_specs=[
            pl.BlockSpec((None, tm, D_MODEL), lambda b, i: (b, i, 0)),
            _mod_spec(mods, k0, tm, 2), _mod_spec(mods, k0 + 1, tm, 2),
            pl.BlockSpec((D_MODEL, n), lambda b, i: (0, 0), pipeline_mode=pl.Buffered(1)),
        ],
        out_specs=pl.BlockSpec((None, tm, n), lambda b, i: (b, i, 0)),
        out_shape=jax.ShapeDtypeStruct((bsz, t, n), F32),
        compiler_params=_params(("parallel", "parallel")),
        name="inproj",
    )(x, mods, mods, w)


def _inproj_even_kernel(x_ref, sh_ref, sc_ref, w_ref, lb_ref, o_ref):
    h = (x_ref[...] * (1.0 + sc_ref[...]) + sh_ref[...]).astype(BF16)
    attn = A_Q + 2 * A_KV
    width = B_HEADS * B_DIM

    def proj(group):
        return _dot(h, w_ref[:, attn + group * width:attn + (group + 1) * width])

    o_ref[:, 0:attn] = _dot(h, w_ref[:, 0:attn])
    o_ref[:, attn:attn + width] = _silu(proj(0))
    lb = lb_ref[...]
    f = lb + (1.0 - lb) * jax.nn.sigmoid(proj(1))
    o_ref[:, attn + width:attn + 2 * width] = jnp.log2(f)
    o_ref[:, attn + 2 * width:attn + 3 * width] = 1.0 - f
    o_ref[:, attn + 3 * width:attn + 4 * width] = proj(2)
    o_ref[:, attn + 4 * width:attn + 5 * width] = _silu(proj(3))


def _inproj_even(x, mods, k0, w, lower, tm):
    bsz, t, _ = x.shape
    n_in = w.shape[1]
    width = B_HEADS * B_DIM
    n_out = n_in + width
    return pl.pallas_call(
        _inproj_even_kernel,
        grid=(bsz, t // tm),
        in_specs=[
            pl.BlockSpec((None, tm, D_MODEL), lambda b, i: (b, i, 0)),
            _mod_spec(mods, k0, tm, 2), _mod_spec(mods, k0 + 1, tm, 2),
            pl.BlockSpec((D_MODEL, n_in), lambda b, i: (0, 0), pipeline_mode=pl.Buffered(1)),
            pl.BlockSpec((1, width), lambda b, i: (0, 0)),
        ],
        out_specs=pl.BlockSpec((None, tm, n_out), lambda b, i: (b, i, 0)),
        out_shape=jax.ShapeDtypeStruct((bsz, t, n_out), F32),
        compiler_params=_params(("parallel", "parallel")),
        name="inproj_even",
    )(x, mods, mods, w, lower.reshape(1, width))


def _inproj_gdn_kernel(x_ref, sh_ref, sc_ref, w_ref, cw_ref, o_ref, hist_ref, carry_ref):
    i = pl.program_id(1)
    tm = x_ref.shape[0]

    @pl.when(i == 0)
    def _():
        carry_ref[...] = jnp.zeros_like(carry_ref)

    h = (x_ref[...] * (1.0 + sc_ref[...]) + sh_ref[...]).astype(BF16)
    for sec in range(3):
        cs = slice(sec * C_QK, (sec + 1) * C_QK)
        raw = _dot(h, w_ref[:, cs])
        ext = jnp.concatenate([carry_ref[:, cs], raw], axis=0)
        z = ext * cw_ref[0:1, cs]
        for j in range(1, C_CONV):
            z = pltpu.roll(z, 1, 0) + ext * cw_ref[j:j + 1, cs]
        act = _silu(z[8:])
        if sec == 2:
            o_ref[:, cs] = act
        else:
            for hd in range(C_HEADS):
                a = act[:, hd * C_DIM:(hd + 1) * C_DIM]
                a = a * lax.rsqrt(jnp.sum(a * a, axis=-1, keepdims=True) + NORM_EPS)
                if sec == 0:
                    a = a * (C_DIM ** -0.5)
                o_ref[:, sec * C_QK + hd * C_DIM:sec * C_QK + (hd + 1) * C_DIM] = a
        carry_ref[:, cs] = raw[tm - 8:tm]
        hist_ref[:, cs] = raw[tm - 8:tm]
    gs = slice(C_CONV_DIM, C_CONV_DIM + C_QK)
    o_ref[:, gs] = _silu(_dot(h, w_ref[:, gs]))
    o_ref[:, C_CONV_DIM + C_QK:] = _dot(h, w_ref[:, C_CONV_DIM + C_QK:])


def _inproj_gdn(x, mods, k0, w, conv_w, tm):
    bsz, t, _ = x.shape
    n = w.shape[1]
    return pl.pallas_call(
        _inproj_gdn_kernel,
        grid=(bsz, t // tm),
        in_specs=[
            pl.BlockSpec((None, tm, D_MODEL), lambda b, i: (b, i, 0)),
            _mod_spec(mods, k0, tm, 2), _mod_spec(mods, k0 + 1, tm, 2),
            pl.BlockSpec((D_MODEL, n), lambda b, i: (0, 0), pipeline_mode=pl.Buffered(1)),
            pl.BlockSpec((C_CONV, C_CONV_DIM), lambda b, i: (0, 0)),
        ],
        out_specs=[
            pl.BlockSpec((None, tm, n), lambda b, i: (b, i, 0)),
            pl.BlockSpec((None, 8, C_CONV_DIM), lambda b, i: (b, 0, 0)),
        ],
        out_shape=[
            jax.ShapeDtypeStruct((bsz, t, n), F32),
            jax.ShapeDtypeStruct((bsz, 8, C_CONV_DIM), F32),
        ],
        scratch_shapes=[pltpu.VMEM((8, C_CONV_DIM), F32)],
        compiler_params=_params(("parallel", "arbitrary")),
        name="inproj_gdn",
    )(x, mods, mods, w, conv_w)


def _swa_kernel(sink_ref, q_ref, kc_ref, kp_ref, vc_ref, vp_ref, o_ref):
    i = pl.program_id(1)
    q = q_ref[...]
    kc = kc_ref[...].astype(BF16)
    kp = kp_ref[...].astype(BF16)
    vc = vc_ref[...].astype(BF16)
    vp = vp_ref[...].astype(BF16)
    r = lax.broadcasted_iota(jnp.int32, (WINDOW, WINDOW), 0)
    c = lax.broadcasted_iota(jnp.int32, (WINDOW, WINDOW), 1)
    dist_c = (r - c).astype(F32)
    dist_p = dist_c + float(WINDOW)
    valid_c = c <= r
    valid_p = c > r + jnp.where(i > 0, 0, WINDOW)
    scale = A_HEAD_DIM ** -0.5
    heads = range(A_HEADS)
    hsl = [slice(h * A_HEAD_DIM, (h + 1) * A_HEAD_DIM) for h in heads]
    ksl = [slice((h // A_GROUP) * A_HEAD_DIM, (h // A_GROUP + 1) * A_HEAD_DIM) for h in heads]
    slope = [2.0 ** (-8.0 * (h + 1) / A_HEADS) for h in heads]
    qh = [q[:, hsl[h]].astype(BF16) for h in heads]
    s_c = [_dot_nt(qh[h], kc[:, ksl[h]]) for h in heads]
    s_p = [_dot_nt(qh[h], kp[:, ksl[h]]) for h in heads]
    s_c = [jnp.where(valid_c, s_c[h] * scale - slope[h] * dist_c, -jnp.inf) for h in heads]
    s_p = [jnp.where(valid_p, s_p[h] * scale - slope[h] * dist_p, -jnp.inf) for h in heads]
    m = [jnp.maximum(jnp.maximum(jnp.max(s_c[h], axis=-1, keepdims=True), jnp.max(s_p[h], axis=-1, keepdims=True)),
                     sink_ref[h]) for h in heads]
    p_c = [jnp.exp(s_c[h] - m[h]) for h in heads]
    p_p = [jnp.exp(s_p[h] - m[h]) for h in heads]
    den = [jnp.sum(p_c[h], axis=-1, keepdims=True) + jnp.sum(p_p[h], axis=-1, keepdims=True)
           + jnp.exp(sink_ref[h] - m[h]) for h in heads]
    o = [_dot(p_c[h].astype(BF16), vc[:, ksl[h]]) + _dot(p_p[h].astype(BF16), vp[:, ksl[h]]) for h in heads]
    o_ref[...] = jnp.concatenate([o[h] / den[h] for h in heads], axis=1).astype(o_ref.dtype)


def _swa_prompt(proj, sinks):
    bsz, t, _ = proj.shape
    nb = t // WINDOW
    kcol = A_Q // LANES
    vcol = kcol + 1

    def cur(col):
        return pl.BlockSpec((None, WINDOW, LANES), lambda b, i: (b, i, col))

    def prev(col):
        return pl.BlockSpec((None, WINDOW, LANES), lambda b, i: (b, jnp.maximum(i - 1, 0), col))

    return pl.pallas_call(
        _swa_kernel,
        grid=(bsz, nb),
        in_specs=[
            pl.BlockSpec(memory_space=pltpu.SMEM),
            pl.BlockSpec((None, WINDOW, A_Q), lambda b, i: (b, i, 0)),
            cur(kcol), prev(kcol), cur(vcol), prev(vcol),
        ],
        out_specs=pl.BlockSpec((None, WINDOW, A_Q), lambda b, i: (b, i, 0)),
        out_shape=jax.ShapeDtypeStruct((bsz, t, A_Q), BF16),
        compiler_params=_params(("parallel", "parallel")),
        name="swa_prompt",
    )(sinks, proj, proj, proj, proj, proj)


def _split3(x):
    hi = x.astype(BF16)
    r1 = x - hi.astype(F32)
    mid = r1.astype(BF16)
    lo = (r1 - mid.astype(F32)).astype(BF16)
    return hi, mid, lo


def _hgrn_kernel(q0_ref, q1_ref, lf0_ref, lf1_ref, k0_ref, k1_ref, v0_ref, v1_ref, g0_ref, g1_ref, ng_ref,
                 o_ref, s_ref, st_ref):
    i = pl.program_id(1)

    @pl.when(i == 0)
    def _():
        st_ref[...] = jnp.zeros_like(st_ref)

    n = HGRN_BLOCK
    nsub = n // HGRN_SUB
    heads = range(B_HEADS)
    qb = jnp.concatenate([q0_ref[...], q1_ref[...]], axis=1)
    lf = jnp.concatenate([lf0_ref[...], lf1_ref[...]], axis=1)
    kk = jnp.concatenate([k0_ref[...], k1_ref[...]], axis=1)
    v = jnp.concatenate([v0_ref[...], v1_ref[...]], axis=1)
    r = lax.broadcasted_iota(jnp.int32, (n, n), 0)
    c = lax.broadcasted_iota(jnp.int32, (n, n), 1)
    tri = jnp.where((c <= r) & (c >= r - (r & (HGRN_SUB - 1))), 1.0, 0.0).astype(BF16)
    lf_hi, lf_mid, lf_lo = _split3(lf)
    a_all = _dot(tri, lf_hi) + (_dot(tri, lf_mid) + _dot(tri, lf_lo))
    a3 = a_all.reshape(nsub, HGRN_SUB, B_HEADS * B_DIM)
    a_last3 = a3[:, HGRN_SUB - 1:HGRN_SUB, :]
    qb3 = qb.reshape(a3.shape)
    kk3 = kk.reshape(a3.shape)
    v3 = v.reshape(a3.shape)
    qe3 = qb3 * jnp.exp2(a3)
    kd3 = kk3 * jnp.exp2(a_last3 - a3)
    dec3 = jnp.exp2(a_last3)
    trow = lax.broadcasted_iota(jnp.int32, (1, HGRN_SUB, 1), 1)
    o_diag = []
    v_t = []
    for h in heads:
        hs = slice(h * B_DIM, (h + 1) * B_DIM)
        a_h, q_h, k_h, v_h = a3[:, :, hs], qb3[:, :, hs], kk3[:, :, hs], v3[:, :, hs]
        acc = jnp.zeros((nsub, HGRN_SUB, B_DIM), F32)
        for s in range(HGRN_SUB):
            e = jnp.exp2(jnp.where(trow >= s, a_h - a_h[:, s:s + 1, :], -jnp.inf))
            col = jnp.sum(q_h * e * k_h[:, s:s + 1, :], axis=-1, keepdims=True)
            acc = acc + col * v_h[:, s:s + 1, :]
        o_diag.append(acc.reshape(n, B_DIM))
        v_t.append(v[:, hs].T.astype(BF16))
    state = [st_ref[h] for h in heads]
    o_state = [[] for _ in heads]
    for ch in range(nsub):
        sl = slice(ch * HGRN_SUB, (ch + 1) * HGRN_SUB)
        for h in heads:
            hs = slice(h * B_DIM, (h + 1) * B_DIM)
            o_state[h].append(_dot_nt(qe3[ch, :, hs].astype(BF16), state[h].astype(BF16)))
            state[h] = state[h] * dec3[ch, :, hs] + _dot(v_t[h][:, sl], kd3[ch, :, hs].astype(BF16))
    g = jnp.concatenate([g0_ref[...], g1_ref[...]], axis=1)
    for h in heads:
        hs = slice(h * B_DIM, (h + 1) * B_DIM)
        st_ref[h] = state[h]
        o = jnp.concatenate(o_state[h], axis=0) + o_diag[h]
        y = o * lax.rsqrt(jnp.mean(o * o, axis=-1, keepdims=True) + NORM_EPS)
        o_ref[:, hs] = (y * ng_ref[...] * g[:, hs]).astype(o_ref.dtype)

    @pl.when(i == pl.num_programs(1) - 1)
    def _():
        for h in heads:
            s_ref[h] = state[h].T


def _hgrn_prompt(proj, norm_g):
    bsz, t, _ = proj.shape
    nh = B_HEADS
    width = nh * B_DIM
    half = width // 2
    base = (A_Q + 2 * A_KV) // half

    def col(off):
        return pl.BlockSpec((None, HGRN_BLOCK, half), lambda b, i: (b, i, base + off))

    return pl.pallas_call(
        _hgrn_kernel,
        grid=(bsz, t // HGRN_BLOCK),
        in_specs=[
            col(0), col(1), col(2), col(3), col(4), col(5), col(6), col(7), col(8), col(9),
            pl.BlockSpec((1, B_DIM), lambda b, i: (0, 0)),
        ],
        out_specs=[
            pl.BlockSpec((None, HGRN_BLOCK, width), lambda b, i: (b, i, 0)),
            pl.BlockSpec((None, nh, B_DIM, B_DIM), lambda b, i: (b, 0, 0, 0)),
        ],
        out_shape=[
            jax.ShapeDtypeStruct((bsz, t, width), BF16),
            jax.ShapeDtypeStruct((bsz, nh, B_DIM, B_DIM), F32),
        ],
        scratch_shapes=[pltpu.VMEM((nh, B_DIM, B_DIM), F32)],
        compiler_params=_params(("parallel", "arbitrary")),
        name="hgrn_prompt",
    )(*([proj] * 10), norm_g.reshape(1, B_DIM))


def _gdn_kernel(q_ref, k_ref, v_ref, gate_ref, ab_ref, alog_ref, dtb_ref, ng_ref, o_ref, s_out_ref, s_ref):
    i = pl.program_id(1)
    n = GDN_CHUNK

    @pl.when(i == 0)
    def _():
        s_ref[...] = jnp.zeros_like(s_ref)

    ab = ab_ref[...]
    log_alpha = -jnp.exp(alog_ref[...]) * _softplus(ab + dtb_ref[...])
    beta_all = jax.nn.sigmoid(ab)
    r = lax.broadcasted_iota(jnp.int32, (n, n), 0)
    c = lax.broadcasted_iota(jnp.int32, (n, n), 1)
    causal = c <= r
    strict = c < r
    tri = jnp.where(causal, 1.0, 0.0).astype(F32)
    eye = jnp.where(r == c, 1.0, 0.0).astype(F32)
    base_mask = (r >> GDN_BASE_LOG) == (c >> GDN_BASE_LOG)
    level_masks = [((r >> (k + 1)) == (c >> (k + 1))) & ((r >> k) != (c >> k))
                   for k in range(GDN_BASE_LOG, GDN_CHUNK_LOG)]
    g_cum =_dot(tri, log_alpha, HIGHEST)
    g_cum_t = g_cum.T

    heads = range(C_HEADS)
    qn, kn, kn_b, vh, g_col, beta, dec_incl, e_g = [], [], [], [], [], [], [], []
    for h in heads:
        hs = slice(h * C_DIM, (h + 1) * C_DIM)
        qn.append(q_ref[:, hs])
        kn.append(k_ref[:, hs])
        vh.append(v_ref[:, hs])
        kn_b.append(kn[h].astype(BF16))
        g_col.append(g_cum[:, h:h + 1])
        beta.append(beta_all[:, C_HEADS + h:C_HEADS + h + 1])
        dec_incl.append(jnp.exp(jnp.where(causal, g_col[h] - g_cum_t[h:h + 1, :], -jnp.inf)))
        e_g.append(jnp.exp(g_col[h]))
    kkt = [_dot_nt(kn_b[h], kn_b[h]) for h in heads]
    qk = [_dot_nt(qn[h].astype(BF16), kn_b[h]) for h in heads]
    l_mat = [beta[h] * kkt[h] * jnp.where(strict, dec_incl[h], 0.0) for h in heads]
    a_pow = [jnp.where(base_mask, -l_mat[h], 0.0) for h in heads]
    t_inv = [eye + a_pow[h] for h in heads]
    for _ in range(GDN_BASE_LOG - 1):
        a_pow = [_dot_inv(a_pow[h], a_pow[h]) for h in heads]
        t_inv = [t_inv[h] + _dot_inv(t_inv[h], a_pow[h]) for h in heads]
    for lm in level_masks:
        x = [_dot_inv(jnp.where(lm, l_mat[h], 0.0), t_inv[h]) for h in heads]
        t_inv = [t_inv[h] - _dot_inv(t_inv[h], x[h]) for h in heads]
    uw = [_dot_inv(t_inv[h], jnp.concatenate([beta[h] * vh[h], (beta[h] * e_g[h]) * kn[h]], axis=1))
          for h in heads]
    s_old = [s_ref[h] for h in heads]
    s_b = [s_old[h].astype(BF16) for h in heads]
    ws = [_dot(uw[h][:, C_DIM:].astype(BF16), s_b[h]) for h in heads]
    qs = [_dot((qn[h] * e_g[h]).astype(BF16), s_b[h]) for h in heads]
    delta_b = [(uw[h][:, :C_DIM] - ws[h]).astype(BF16) for h in heads]
    o = [qs[h] + _dot((qk[h] * dec_incl[h]).astype(BF16), delta_b[h]) for h in heads]
    for h in heads:
        g_last = g_col[h][n - 1:n]
        k_dec = kn[h] * jnp.exp(g_last - g_col[h])
        s_ref[h] = jnp.exp(g_last) * s_old[h] + _dot(k_dec.T.astype(BF16), delta_b[h])
    for h in heads:
        hs = slice(h * C_DIM, (h + 1) * C_DIM)
        y = o[h] * lax.rsqrt(jnp.mean(o[h] * o[h], axis=-1, keepdims=True) + NORM_EPS)
        o_ref[:, hs] = (y * ng_ref[...] * gate_ref[:, hs]).astype(o_ref.dtype)

    @pl.when(i == pl.num_programs(1) - 1)
    def _():
        s_out_ref[...] = s_ref[...]


def _gdn_prompt(proj, alog_row, dtb_row, norm_g):
    bsz, t, _ = proj.shape
    n = GDN_CHUNK

    def col(cidx):
        return pl.BlockSpec((None, n, C_QK), lambda b, i: (b, i, cidx))

    one = pl.BlockSpec((1, LANES), lambda b, i: (0, 0))
    return pl.pallas_call(
        _gdn_kernel,
        grid=(bsz, t // n),
        in_specs=[
            col(0), col(1), col(2), col(3),
            pl.BlockSpec((None, n, LANES), lambda b, i: (b, i, (C_CONV_DIM + C_QK) // LANES)),
            one, one, one,
        ],
        out_specs=[
            pl.BlockSpec((None, n, C_QK), lambda b, i: (b, i, 0)),
            pl.BlockSpec((None, C_HEADS, C_DIM, C_DIM), lambda b, i: (b, 0, 0, 0)),
        ],
        out_shape=[
            jax.ShapeDtypeStruct((bsz, t, C_QK), BF16),
            jax.ShapeDtypeStruct((bsz, C_HEADS, C_DIM, C_DIM), F32),
        ],
        scratch_shapes=[pltpu.VMEM((C_HEADS, C_DIM, C_DIM), F32)],
        compiler_params=_params(("parallel", "arbitrary")),
        name="gdn_prompt",
    )(proj, proj, proj, proj, proj, alog_row, dtb_row, norm_g.reshape(1, C_DIM))


def _swa_dec_kernel(sink_ref, q_ref, kn_ref, vn_ref, ck_ref, cv_ref, o_ref, ok_ref, ov_ref):
    batch = range(q_ref.shape[0])
    row = lax.broadcasted_iota(jnp.int32, (WINDOW, LANES), 0)
    head = lax.broadcasted_iota(jnp.int32, (A_HEADS, WINDOW), 0)
    dist = (WINDOW - 1 - lax.broadcasted_iota(jnp.int32, (A_HEADS, WINDOW), 1)).astype(F32)
    scale = A_HEAD_DIM ** -0.5
    slope = jnp.zeros((A_HEADS, WINDOW), F32)
    sink = jnp.zeros((A_HEADS, 1), F32)
    for h in range(A_HEADS):
        slope = jnp.where(head == h, 2.0 ** (-8.0 * (h + 1) / A_HEADS), slope)
        sink = jnp.where(head[:, 0:1] == h, sink_ref[h], sink)
    bias = slope * dist
    first_kv = head[:, 0:A_HEAD_DIM] < A_GROUP
    keys = [jnp.where(row == WINDOW - 1, kn_ref[b], pltpu.roll(ck_ref[b], WINDOW - 1, 0)) for b in batch]
    vals = [jnp.where(row == WINDOW - 1, vn_ref[b], pltpu.roll(cv_ref[b], WINDOW - 1, 0)) for b in batch]
    for b in batch:
        ok_ref[b] = keys[b]
        ov_ref[b] = vals[b]
    q8 = [jnp.concatenate([jnp.where(first_kv, q_ref[b], 0.0), jnp.where(first_kv, 0.0, q_ref[b])],
                          axis=1).astype(BF16) for b in batch]
    s = [_dot_nt(q8[b], keys[b].astype(BF16)) * scale - bias for b in batch]
    m = [jnp.maximum(jnp.max(s[b], axis=-1, keepdims=True), sink) for b in batch]
    p = [jnp.exp(s[b] - m[b]) for b in batch]
    den = [jnp.sum(p[b], axis=-1, keepdims=True) + jnp.exp(sink - m[b]) for b in batch]
    o = [_dot(p[b].astype(BF16), vals[b].astype(BF16)) for b in batch]
    for b in batch:
        o_ref[b] = jnp.where(first_kv, o[b][:, :A_HEAD_DIM], o[b][:, A_HEAD_DIM:]) / den[b]


def _swa_decode(proj2, cache_k, cache_v, sinks):
    nbatch = proj2.shape[0]
    bb = 8
    q3 = proj2[:, :A_Q].reshape(nbatch, A_HEADS, A_HEAD_DIM)
    kn = proj2[:, A_Q:A_Q + A_KV].reshape(nbatch, 1, A_KV)
    vn = proj2[:, A_Q + A_KV:A_Q + 2 * A_KV].reshape(nbatch, 1, A_KV)
    ck = cache_k.reshape(nbatch, WINDOW, A_KV)
    cv = cache_v.reshape(nbatch, WINDOW, A_KV)
    qspec = pl.BlockSpec((bb, A_HEADS, A_HEAD_DIM), lambda j: (j, 0, 0))
    nspec = pl.BlockSpec((bb, 1, A_KV), lambda j: (j, 0, 0))
    cspec = pl.BlockSpec((bb, WINDOW, A_KV), lambda j: (j, 0, 0))
    o, nk, nv = pl.pallas_call(
        _swa_dec_kernel,
        grid=(nbatch // bb,),
        in_specs=[pl.BlockSpec(memory_space=pltpu.SMEM), qspec, nspec, nspec, cspec, cspec],
        out_specs=[qspec, cspec, cspec],
        out_shape=[
            jax.ShapeDtypeStruct((nbatch, A_HEADS, A_HEAD_DIM), F32),
            jax.ShapeDtypeStruct((nbatch, WINDOW, A_KV), F32),
            jax.ShapeDtypeStruct((nbatch, WINDOW, A_KV), F32),
        ],
        compiler_params=_params(("parallel",)),
        name="swa_decode",
    )(sinks, q3, kn, vn, ck, cv)
    shape5 = (nbatch, WINDOW, A_KV_HEADS, A_HEAD_DIM)
    return o.reshape(nbatch, A_Q), nk.reshape(shape5), nv.reshape(shape5)


def _cols(rows, heads):
    nbatch = rows.shape[0]
    x = rows.reshape(nbatch // DEC_BLOCK, DEC_BLOCK, heads, LANES)
    return x.transpose(2, 0, 3, 1)


def _hgrn_dec_kernel(qt_ref, ft_ref, lb_ref, v_ref, g_ref, ng_ref, s_ref, o_ref, so_ref, orow_ref):
    lb = lb_ref[...]
    qc = _silu(qt_ref[...])
    f = lb + (1.0 - lb) * jax.nn.sigmoid(ft_ref[...])
    kk = 1.0 - f
    v = v_ref[...]
    for bb in range(DEC_BLOCK):
        s_new = f[:, bb:bb + 1] * s_ref[bb] + kk[:, bb:bb + 1] * v[bb:bb + 1, :]
        so_ref[bb] = s_new
        orow_ref[bb:bb + 1, :] = jnp.sum(qc[:, bb:bb + 1] * s_new, axis=0, keepdims=True)
    o_ref[...] = _rms_gate(orow_ref[...], ng_ref[...], g_ref[...])


def _hgrn_decode(proj2, state, lower, norm_g):
    nbatch = proj2.shape[0]
    nh = B_HEADS
    base = A_Q + 2 * A_KV
    width = nh * B_DIM
    qt = _cols(proj2[:, base:base + width], nh)
    ft = _cols(proj2[:, base + width:base + 2 * width], nh)
    colspec = pl.BlockSpec((None, None, B_DIM, DEC_BLOCK), lambda h, j: (h, j, 0, 0))
    sspec = pl.BlockSpec((DEC_BLOCK, None, B_DIM, B_DIM), lambda h, j: (j, h, 0, 0))
    cb = base // LANES
    o, s_new = pl.pallas_call(
        _hgrn_dec_kernel,
        grid=(nh, nbatch // DEC_BLOCK),
        in_specs=[
            colspec, colspec,
            pl.BlockSpec((None, B_DIM, 1), lambda h, j: (h, 0, 0)),
            pl.BlockSpec((DEC_BLOCK, B_DIM), lambda h, j: (j, cb + 2 * nh + h)),
            pl.BlockSpec((DEC_BLOCK, B_DIM), lambda h, j: (j, cb + 3 * nh + h)),
            pl.BlockSpec((1, B_DIM), lambda h, j: (0, 0)),
            sspec,
        ],
        out_specs=[pl.BlockSpec((DEC_BLOCK, B_DIM), lambda h, j: (j, h)), sspec],
        out_shape=[
            jax.ShapeDtypeStruct((nbatch, width), F32),
            jax.ShapeDtypeStruct(state.shape, F32),
        ],
        scratch_shapes=[pltpu.VMEM((DEC_BLOCK, B_DIM), F32)],
        compiler_params=_params(("parallel", "parallel")),
        name="hgrn_decode",
    )(qt, ft, lower.reshape(nh, B_DIM, 1), proj2, proj2, norm_g.reshape(1, B_DIM), state)
    return o, s_new


def _gdn_prep_kernel(p_ref, h0_ref, h1_ref, h2_ref, cw_ref, alog_ref, dtb_ref, q_ref, k_ref, v_ref, ab_ref):
    acc = (h0_ref[...] * cw_ref[0:1, :] + h1_ref[...] * cw_ref[1:2, :] + h2_ref[...] * cw_ref[2:3, :]
           + p_ref[:, 0:C_CONV_DIM] * cw_ref[3:4, :])
    qkv = _silu(acc)
    for h in range(C_HEADS):
        hs = slice(h * C_DIM, (h + 1) * C_DIM)
        qh = qkv[:, h * C_DIM:(h + 1) * C_DIM]
        kh = qkv[:, C_QK + h * C_DIM:C_QK + (h + 1) * C_DIM]
        q_ref[:, hs] = qh * lax.rsqrt(jnp.sum(qh * qh, axis=-1, keepdims=True) + NORM_EPS) * (C_DIM ** -0.5)
        k_ref[:, hs] = kh * lax.rsqrt(jnp.sum(kh * kh, axis=-1, keepdims=True) + NORM_EPS)
    v_ref[...] = qkv[:, 2 * C_QK:3 * C_QK]
    ab = p_ref[:, C_CONV_DIM + C_QK:C_CONV_DIM + C_QK + LANES]
    alpha = jnp.exp(-jnp.exp(alog_ref[...]) * _softplus(ab + dtb_ref[...]))
    beta = jax.nn.sigmoid(ab)
    lane = lax.broadcasted_iota(jnp.int32, ab.shape, 1)
    ab_ref[...] = jnp.where(lane < C_HEADS, alpha, beta)


def _gdn_dec_kernel(qt_ref, kt_ref, v_ref, gate_ref, al_ref, be_ref, ng_ref, s_ref, o_ref, so_ref, orow_ref):
    qt = qt_ref[...]
    kt = kt_ref[...]
    v = v_ref[...]
    al = al_ref[...]
    be = be_ref[...]
    for bb in range(DEC_BLOCK):
        s = s_ref[bb]
        kc = kt[:, bb:bb + 1]
        a = al[bb:bb + 1, :]
        ks = jnp.sum(kc * s, axis=0, keepdims=True)
        delta = be[bb:bb + 1, :] * (v[bb:bb + 1, :] - a * ks)
        s_new = a * s + kc * delta
        so_ref[bb] = s_new
        orow_ref[bb:bb + 1, :] = jnp.sum(qt[:, bb:bb + 1] * s_new, axis=0, keepdims=True)
    o_ref[...] = _rms_gate(orow_ref[...], ng_ref[...], gate_ref[...])


def _gdn_decode(proj2, hist, state, conv_w, alog_row, dtb_row, norm_g):
    nbatch = proj2.shape[0]
    nh = C_HEADS
    full = lambda shape: pl.BlockSpec(shape, lambda: tuple(0 for _ in shape))
    hspec = full((nbatch, C_CONV_DIM))
    qn, kn, vc, ab = pl.pallas_call(
        _gdn_prep_kernel,
        grid=(),
        in_specs=[
            full((nbatch, ODD_IN_PAD)), hspec, hspec, hspec,
            full((C_CONV, C_CONV_DIM)), full((1, LANES)), full((1, LANES)),
        ],
        out_specs=[full((nbatch, C_QK)), full((nbatch, C_QK)), full((nbatch, C_QK)), full((nbatch, LANES))],
        out_shape=[
            jax.ShapeDtypeStruct((nbatch, C_QK), F32),
            jax.ShapeDtypeStruct((nbatch, C_QK), F32),
            jax.ShapeDtypeStruct((nbatch, C_QK), F32),
            jax.ShapeDtypeStruct((nbatch, LANES), F32),
        ],
        compiler_params=pltpu.CompilerParams(vmem_limit_bytes=VMEM_LIMIT),
        name="gdn_decode_prep",
    )(proj2, hist[:, 0], hist[:, 1], hist[:, 2], conv_w, alog_row, dtb_row)
    qt = _cols(qn, nh)
    kt = _cols(kn, nh)
    al = jnp.broadcast_to(ab[:, :nh].T[:, :, None], (nh, nbatch, LANES))
    be = jnp.broadcast_to(ab[:, nh:2 * nh].T[:, :, None], (nh, nbatch, LANES))
    colspec = pl.BlockSpec((None, None, C_DIM, DEC_BLOCK), lambda h, j: (h, j, 0, 0))
    sspec = pl.BlockSpec((DEC_BLOCK, None, C_DIM, C_DIM), lambda h, j: (j, h, 0, 0))
    rspec = pl.BlockSpec((None, DEC_BLOCK, LANES), lambda h, j: (h, j, 0))
    gcol = C_CONV_DIM // LANES
    o, s_new = pl.pallas_call(
        _gdn_dec_kernel,
        grid=(nh, nbatch // DEC_BLOCK),
        in_specs=[
            colspec, colspec,
            pl.BlockSpec((DEC_BLOCK, C_DIM), lambda h, j: (j, h)),
            pl.BlockSpec((DEC_BLOCK, C_DIM), lambda h, j: (j, gcol + h)),
            rspec, rspec,
            pl.BlockSpec((1, C_DIM), lambda h, j: (0, 0)),
            sspec,
        ],
        out_specs=[pl.BlockSpec((DEC_BLOCK, C_DIM), lambda h, j: (j, h)), sspec],
        out_shape=[
            jax.ShapeDtypeStruct((nbatch, C_QK), F32),
            jax.ShapeDtypeStruct(state.shape, F32),
        ],
        scratch_shapes=[pltpu.VMEM((DEC_BLOCK, C_DIM), F32)],
        compiler_params=_params(("parallel", "parallel")),
        name="gdn_decode",
    )(qt, kt, vc, proj2, al, be, norm_g.reshape(1, C_DIM), state)
    return o, s_new


def _trunk(x, mods, caches, p, tm):
    decode = caches is not None
    m = mods[0]
    ffn = functools.partial(_ffn, w_up=p["w_up"], w_down=p["w_down"], ln_g=p["ln_g"], ln_b=p["ln_b"], tm=tm)
    x = ffn(x, m, 0, layer=0, sub=0)
    if decode:
        proj = _inproj(x, m, 3, p["even_in"], tm)
        o_a, new_k, new_v = _swa_decode(proj[0], caches[0][0], caches[1][0], p["sinks"])
        o_b, s_hgrn = _hgrn_decode(proj[0], caches[2][0], p["lower"], p["hgrn_norm_g"])
        o_a, o_b = o_a[None], o_b[None]
    else:
        bsz = x.shape[0]
        proj = _inproj_even(x, m, 3, p["even_in"], p["lower"], tm)
        o_a = _swa_prompt(proj, p["sinks"])
        new_k = proj[:, -WINDOW:, A_Q:A_Q + A_KV].reshape(bsz, WINDOW, A_KV_HEADS, A_HEAD_DIM)
        new_v = proj[:, -WINDOW:, A_Q + A_KV:A_Q + 2 * A_KV].reshape(bsz, WINDOW, A_KV_HEADS, A_HEAD_DIM)
        o_b, s_hgrn = _hgrn_prompt(proj, p["hgrn_norm_g"])
    x = ffn(x, m, 6, layer=0, sub=1, mixer=(o_a, 0, o_b, 0, p["even_out"], 5))
    m = mods[1]
    x = ffn(x, m, 0, layer=1, sub=0)
    if decode:
        proj = _inproj(x, m, 3, p["odd_in"], tm)
        hist = caches[4][0]
        o_c, s_gdn = _gdn_decode(proj[0], hist, caches[3][0], p["conv_w"], p["alog_row"], p["dtb_row"],
                                 p["gdn_norm_g"])
        o_c = o_c[None]
        new_hist = jnp.concatenate([hist[:, 1:], proj[0][:, None, :C_CONV_DIM]], axis=1)
    else:
        proj, last_rows = _inproj_gdn(x, m, 3, p["odd_in"], p["conv_w"], tm)
        o_c, s_gdn = _gdn_prompt(proj, p["alog_row"], p["dtb_row"], p["gdn_norm_g"])
        new_hist = last_rows[:, -(C_CONV - 1):]
    x = ffn(x, m, 6, layer=1, sub=1, mixer=(o_c, 0, o_c, 1, p["odd_out"], 5))
    return x, new_k[None], new_v[None], s_hgrn[None], s_gdn[None], new_hist[None]


def kernel(x_prompt, x_sample, cache_swa_k, cache_swa_v, state_hgrn, state_gdn, state_gdn_conv, c_prompt, c_sample, ada_w, ada_b, ln_g, ln_b, ffn_w_up, ffn_w_down, even_w_in, even_w_out, swa_sinks, hgrn_norm_g, hgrn_lb_logits, odd_w_in, odd_w_out, gdn_conv_w, gdn_a_log, gdn_dt_bias, gdn_norm_g):
    n_prompt = c_prompt.shape[0]
    n_sample = c_sample.shape[0]
    pad_rows = (-(n_prompt + n_sample)) % 8
    c_all = jnp.concatenate([c_prompt, c_sample, jnp.zeros((pad_rows, D_MODEL), F32)], axis=0)
    mods = _ada_mods(c_all, ada_w, ada_b)
    mods_p = mods[:, :n_prompt].reshape(DEPTH, n_prompt, 1, N_MOD * D_MODEL)
    mods_s = mods[:, n_prompt:n_prompt + n_sample].reshape(DEPTH, 1, n_sample, N_MOD * D_MODEL)

    probs = jax.nn.softmax(hgrn_lb_logits.astype(F32), axis=0)
    lower = (jnp.cumsum(probs, axis=0)[1:] - probs[0])[0]

    def lane_row(v):
        return jnp.pad(v.astype(F32), (0, LANES - v.shape[0])).reshape(1, LANES)

    p = dict(
        w_up=ffn_w_up.astype(BF16), w_down=ffn_w_down.astype(BF16),
        ln_g=ln_g, ln_b=ln_b,
        even_in=even_w_in[0].astype(BF16), even_out=even_w_out[0].astype(BF16),
        odd_in=jnp.concatenate([odd_w_in[0].astype(BF16), jnp.zeros((D_MODEL, ODD_IN_PAD - ODD_IN), BF16)], axis=1),
        odd_out=odd_w_out[0].astype(BF16),
        sinks=swa_sinks[0], lower=lower, hgrn_norm_g=hgrn_norm_g[0],
        conv_w=gdn_conv_w[0], alog_row=lane_row(gdn_a_log[0]), dtb_row=lane_row(gdn_dt_bias[0]),
        gdn_norm_g=gdn_norm_g[0],
    )
    y_p, p_k, p_v, p_hgrn, p_gdn, p_conv = _trunk(x_prompt, mods_p, None, p, 512)
    caches = (cache_swa_k, cache_swa_v, state_hgrn, state_gdn, state_gdn_conv)
    x_s = x_sample.reshape(1, n_sample, D_MODEL)
    y_s, s_k, s_v, s_hgrn, s_gdn, s_conv = _trunk(x_s, mods_s, caches, p, n_sample)
    y_s = y_s.reshape(n_sample, 1, D_MODEL)
    return (y_p, y_s, p_k, p_v, p_hgrn, p_gdn, p_conv, s_k, s_v, s_hgrn, s_gdn, s_conv)
```

```python
import functools

import jax
import jax.numpy as jnp
from jax import lax
from jax.experimental import pallas as pl
from jax.experimental.pallas import tpu as pltpu

F32 = jnp.float32
BF16 = jnp.bfloat16
HIGHEST = lax.Precision.HIGHEST

D_MODEL = 1024
DEPTH = 2
WINDOW = 128
A_HEADS = 8
A_KV_HEADS = 2
A_GROUP = A_HEADS // A_KV_HEADS
A_HEAD_DIM = 64
A_Q = A_HEADS * A_HEAD_DIM
A_KV = A_KV_HEADS * A_HEAD_DIM
B_HEADS = 4
B_DIM = 128
EVEN_IN = 2816
C_HEADS = 8
C_DIM = 128
C_QK = C_HEADS * C_DIM
C_CONV = 4
C_CONV_DIM = 3 * C_QK
ODD_IN = C_CONV_DIM + C_QK + 2 * C_HEADS
ODD_IN_PAD = 4224
D_FF = 2816
N_MOD = 9
DN_ALPHA = (2 * DEPTH) ** 0.25
LN_EPS = 1e-5
NORM_EPS = 1e-6
LANES = 128

FF_CHUNK = 256
FFN_TM = 1024
FFN_VMEM_LIMIT = 56 << 20
PROJ_TILE = 1408
HGRN_BLOCK = 256
HGRN_SUB = 8
GDN_CHUNK_LOG = 7
GDN_CHUNK = 1 << GDN_CHUNK_LOG
GDN_BASE_LOG = 3
GDN_STEP_CHUNKS = 2
DEC_BLOCK = 64
VMEM_LIMIT = 48 << 20


def _params(sem, vmem=VMEM_LIMIT):
    return pltpu.CompilerParams(dimension_semantics=sem, vmem_limit_bytes=vmem)


def _silu(x):
    return x * jax.nn.sigmoid(x)


def _softplus(x):
    return jnp.maximum(x, 0.0) + jnp.log(1.0 + jnp.exp(-jnp.abs(x)))


def _layer_norm(y, g, b):
    mu = jnp.mean(y, axis=-1, keepdims=True)
    d = y - mu
    var = jnp.mean(d * d, axis=-1, keepdims=True)
    return d * lax.rsqrt(var + LN_EPS) * g + b


def _rms_gate(o, norm_g, gate):
    y = o * lax.rsqrt(jnp.mean(o * o, axis=-1, keepdims=True) + NORM_EPS)
    return y * norm_g * _silu(gate)


def _dot(a, b, precision=None):
    return jnp.dot(a, b, preferred_element_type=F32, precision=precision)


def _dot_inv(a, b):
    return _dot(a.astype(BF16), b.astype(BF16))


def _dot_nt(a, b, precision=None):
    return lax.dot_general(a, b, (((1,), (1,)), ((), ())), preferred_element_type=F32, precision=precision)


def _ada_kernel(c_ref, w_ref, b_ref, o_ref):
    cs = _silu(c_ref[...]).astype(BF16)
    o_ref[...] = _dot(cs, w_ref[...].astype(BF16)) + b_ref[...]


def _ada_mods(c_all, ada_w, ada_b):
    m = c_all.shape[0]
    n = N_MOD * D_MODEL
    tn = 1152
    return pl.pallas_call(
        _ada_kernel,
        grid=(DEPTH, n // tn),
        in_specs=[
            pl.BlockSpec((m, D_MODEL), lambda l, j: (0, 0)),
            pl.BlockSpec((None, D_MODEL, tn), lambda l, j: (l, 0, j)),
            pl.BlockSpec((None, 1, tn), lambda l, j: (l, 0, j)),
        ],
        out_specs=pl.BlockSpec((None, m, tn), lambda l, j: (l, 0, j)),
        out_shape=jax.ShapeDtypeStruct((DEPTH, m, n), F32),
        compiler_params=_params(("parallel", "parallel")),
        name="ada_mods",
    )(c_all, ada_w, ada_b.reshape(DEPTH, 1, n))


def _mod_spec(mods, k, tm, grid_rank):
    per_token = mods.shape[1] != 1
    rows = tm if per_token else 1
    if grid_rank == 3:
        return pl.BlockSpec((None, rows, D_MODEL), lambda b, i, j: (b, i if per_token else 0, k))
    return pl.BlockSpec((None, rows, D_MODEL), lambda b, i: (b, i if per_token else 0, k))


def _ffn_body(x, sh_ref, sc_ref, g_ref, wu_ref, wd_ref, lg_ref, lb_ref, o_ref):
    h = (x * (1.0 + sc_ref[...]) + sh_ref[...]).astype(BF16)
    for c in range(D_FF // FF_CHUNK):
        lo = c * FF_CHUNK
        gate = _dot(h, wu_ref[:, lo:lo + FF_CHUNK])
        up = _dot(h, wu_ref[:, D_FF + lo:D_FF + lo + FF_CHUNK])
        act = (_silu(gate) * up).astype(BF16)
        part = _dot(act, wd_ref[lo:lo + FF_CHUNK, :])
        if c == 0:
            o_ref[...] = part
        else:
            o_ref[...] += part
    y = DN_ALPHA * x + (0.5 * g_ref[...]) * o_ref[...]
    o_ref[...] = _layer_norm(y, lg_ref[...], lb_ref[...])


def _ffn_kernel(x_ref, sh_ref, sc_ref, g_ref, wu_ref, wd_ref, lg_ref, lb_ref, o_ref):
    _ffn_body(x_ref[...], sh_ref, sc_ref, g_ref, wu_ref, wd_ref, lg_ref, lb_ref, o_ref)


def _mix_ffn_kernel(x_ref, o1_ref, o2_ref, gm_ref, w1_ref, w2_ref, lgm_ref, lbm_ref,
                    sh_ref, sc_ref, g_ref, wu_ref, wd_ref, lg_ref, lb_ref, o_ref):
    mix = _dot(o1_ref[...].astype(BF16), w1_ref[...]) + _dot(o2_ref[...].astype(BF16), w2_ref[...])
    x1 = _layer_norm(DN_ALPHA * x_ref[...] + gm_ref[...] * mix, lgm_ref[...], lbm_ref[...])
    _ffn_body(x1, sh_ref, sc_ref, g_ref, wu_ref, wd_ref, lg_ref, lb_ref, o_ref)


def _ffn(x, mods, k0, w_up, w_down, ln_g, ln_b, layer, sub, tm, mixer=None):
    bsz, t, _ = x.shape
    tm = min(t, max(tm, FFN_TM))
    half = D_MODEL // 2
    row = pl.BlockSpec((None, tm, D_MODEL), lambda b, i: (b, i, 0))
    vec = pl.BlockSpec((1, D_MODEL), lambda b, i: (0, 0))
    resident = pl.Buffered(1)
    ffn_specs = [
        _mod_spec(mods, k0, tm, 2), _mod_spec(mods, k0 + 1, tm, 2), _mod_spec(mods, k0 + 2, tm, 2),
        pl.BlockSpec((None, None, D_MODEL, 2 * D_FF), lambda b, i: (layer, sub, 0, 0), pipeline_mode=resident),
        pl.BlockSpec((None, None, D_FF, D_MODEL), lambda b, i: (layer, sub, 0, 0), pipeline_mode=resident),
        vec, vec,
    ]
    ffn_args = (mods, mods, mods, w_up, w_down,
                ln_g[layer, 2 * sub].reshape(1, D_MODEL), ln_b[layer, 2 * sub].reshape(1, D_MODEL))
    if mixer is None:
        body, specs, args = _ffn_kernel, [row] + ffn_specs, (x,) + ffn_args
    else:
        o1, c1, o2, c2, w_out, gate_k = mixer
        body = _mix_ffn_kernel
        specs = [
            row,
            pl.BlockSpec((None, tm, half), lambda b, i: (b, i, c1)),
            pl.BlockSpec((None, tm, half), lambda b, i: (b, i, c2)),
            _mod_spec(mods, gate_k, tm, 2),
            pl.BlockSpec((half, D_MODEL), lambda b, i: (0, 0), pipeline_mode=resident),
            pl.BlockSpec((half, D_MODEL), lambda b, i: (1, 0), pipeline_mode=resident),
            vec, vec,
        ] + ffn_specs
        args = (x, o1, o2, mods, w_out, w_out,
                ln_g[layer, 1].reshape(1, D_MODEL), ln_b[layer, 1].reshape(1, D_MODEL)) + ffn_args
    return pl.pallas_call(
        body,
        grid=(bsz, t // tm),
        in_specs=specs,
        out_specs=row,
        out_shape=jax.ShapeDtypeStruct(x.shape, F32),
        compiler_params=_params(("parallel", "parallel"), FFN_VMEM_LIMIT),
        name="ffn" if mixer is None else "mix_ffn",
    )(*args)


def _inproj_kernel(x_ref, sh_ref, sc_ref, w_ref, o_ref):
    h = (x_ref[...] * (1.0 + sc_ref[...]) + sh_ref[...]).astype(BF16)
    for lo in range(0, w_ref.shape[1], PROJ_TILE):
        o_ref[:, lo:lo + PROJ_TILE] = _dot(h, w_ref[:, lo:lo + PROJ_TILE])


def _inproj(x, mods, k0, w, tm):
    bsz, t, _ = x.shape
    n = w.shape[1]
    return pl.pallas_call(
        _inproj_kernel,
        grid=(bsz, t // tm),
        in_specs=[
            pl.BlockSpec((None, tm, D_MODEL), lambda b, i: (b, i, 0)),
            _mod_spec(mods, k0, tm, 2), _mod_spec(mods, k0 + 1, tm, 2),
            pl.BlockSpec((D_MODEL, n), lambda b, i: (0, 0), pipeline_mode=pl.Buffered(1)),
        ],
        out_specs=pl.BlockSpec((None, tm, n), lambda b, i: (b, i, 0)),
        out_shape=jax.ShapeDtypeStruct((bsz, t, n), F32),
        compiler_params=_params(("parallel", "parallel")),
        name="inproj",
    )(x, mods, mods, w)


def _inproj_even_kernel(x_ref, sh_ref, sc_ref, w_ref, lb_ref, o_ref):
    h = (x_ref[...] * (1.0 + sc_ref[...]) + sh_ref[...]).astype(BF16)
    attn = A_Q + 2 * A_KV
    width = B_HEADS * B_DIM

    def proj(group):
        return _dot(h, w_ref[:, attn + group * width:attn + (group + 1) * width])

    o_ref[:, 0:attn] = _dot(h, w_ref[:, 0:attn])
    o_ref[:, attn:attn + width] = _silu(proj(0))
    lb = lb_ref[...]
    f = lb + (1.0 - lb) * jax.nn.sigmoid(proj(1))
    o_ref[:, attn + width:attn + 2 * width] = jnp.log2(f)
    o_ref[:, attn + 2 * width:attn + 3 * width] = jnp.log2(1.0 - f)
    o_ref[:, attn + 3 * width:attn + 4 * width] = proj(2)
    o_ref[:, attn + 4 * width:attn + 5 * width] = _silu(proj(3))


def _inproj_even(x, mods, k0, w, lower, tm):
    bsz, t, _ = x.shape
    n_in = w.shape[1]
    width = B_HEADS * B_DIM
    n_out = n_in + width
    return pl.pallas_call(
        _inproj_even_kernel,
        grid=(bsz, t // tm),
        in_specs=[
            pl.BlockSpec((None, tm, D_MODEL), lambda b, i: (b, i, 0)),
            _mod_spec(mods, k0, tm, 2), _mod_spec(mods, k0 + 1, tm, 2),
            pl.BlockSpec((D_MODEL, n_in), lambda b, i: (0, 0), pipeline_mode=pl.Buffered(1)),
            pl.BlockSpec((1, width), lambda b, i: (0, 0)),
        ],
        out_specs=pl.BlockSpec((None, tm, n_out), lambda b, i: (b, i, 0)),
        out_shape=jax.ShapeDtypeStruct((bsz, t, n_out), F32),
        compiler_params=_params(("parallel", "parallel")),
        name="inproj_even",
    )(x, mods, mods, w, lower.reshape(1, width))


def _inproj_gdn_kernel(x_ref, sh_ref, sc_ref, w_ref, cw_ref, o_ref, hist_ref, carry_ref):
    i = pl.program_id(1)
    tm = x_ref.shape[0]

    @pl.when(i == 0)
    def _():
        carry_ref[...] = jnp.zeros_like(carry_ref)

    h = (x_ref[...] * (1.0 + sc_ref[...]) + sh_ref[...]).astype(BF16)
    for sec in range(3):
        cs = slice(sec * C_QK, (sec + 1) * C_QK)
        raw = _dot(h, w_ref[:, cs])
        ext = jnp.concatenate([carry_ref[:, cs], raw], axis=0)
        z = ext * cw_ref[0:1, cs]
        for j in range(1, C_CONV):
            z = pltpu.roll(z, 1, 0) + ext * cw_ref[j:j + 1, cs]
        act = _silu(z[8:])
        if sec == 2:
            o_ref[:, cs] = act
        else:
            for hd in range(C_HEADS):
                a = act[:, hd * C_DIM:(hd + 1) * C_DIM]
                a = a * lax.rsqrt(jnp.sum(a * a, axis=-1, keepdims=True) + NORM_EPS)
                if sec == 0:
                    a = a * (C_DIM ** -0.5)
                o_ref[:, sec * C_QK + hd * C_DIM:sec * C_QK + (hd + 1) * C_DIM] = a
        carry_ref[:, cs] = raw[tm - 8:tm]
        hist_ref[:, cs] = raw[tm - 8:tm]
    gs = slice(C_CONV_DIM, C_CONV_DIM + C_QK)
    o_ref[:, gs] = _silu(_dot(h, w_ref[:, gs]))
    o_ref[:, C_CONV_DIM + C_QK:] = _dot(h, w_ref[:, C_CONV_DIM + C_QK:])


def _inproj_gdn(x, mods, k0, w, conv_w, tm):
    bsz, t, _ = x.shape
    n = w.shape[1]
    return pl.pallas_call(
        _inproj_gdn_kernel,
        grid=(bsz, t // tm),
        in_specs=[
            pl.BlockSpec((None, tm, D_MODEL), lambda b, i: (b, i, 0)),
            _mod_spec(mods, k0, tm, 2), _mod_spec(mods, k0 + 1, tm, 2),
            pl.BlockSpec((D_MODEL, n), lambda b, i: (0, 0), pipeline_mode=pl.Buffered(1)),
            pl.BlockSpec((C_CONV, C_CONV_DIM), lambda b, i: (0, 0)),
        ],
        out_specs=[
            pl.BlockSpec((None, tm, n), lambda b, i: (b, i, 0)),
            pl.BlockSpec((None, 8, C_CONV_DIM), lambda b, i: (b, 0, 0)),
        ],
        out_shape=[
            jax.ShapeDtypeStruct((bsz, t, n), F32),
            jax.ShapeDtypeStruct((bsz, 8, C_CONV_DIM), F32),
        ],
        scratch_shapes=[pltpu.VMEM((8, C_CONV_DIM), F32)],
        compiler_params=_params(("parallel", "arbitrary")),
        name="inproj_gdn",
    )(x, mods, mods, w, conv_w)


def _swa_kernel(sink_ref, q_ref, kc_ref, kp_ref, vc_ref, vp_ref, o_ref):
    i = pl.program_id(1)
    q = q_ref[...]
    kc = kc_ref[...].astype(BF16)
    kp = kp_ref[...].astype(BF16)
    vc = vc_ref[...].astype(BF16)
    vp = vp_ref[...].astype(BF16)
    r = lax.broadcasted_iota(jnp.int32, (WINDOW, WINDOW), 0)
    c = lax.broadcasted_iota(jnp.int32, (WINDOW, WINDOW), 1)
    dist_c = (r - c).astype(F32)
    dist_p = dist_c + float(WINDOW)
    valid_c = c <= r
    valid_p = c > r + jnp.where(i > 0, 0, WINDOW)
    scale = A_HEAD_DIM ** -0.5
    heads = range(A_HEADS)
    hsl = [slice(h * A_HEAD_DIM, (h + 1) * A_HEAD_DIM) for h in heads]
    ksl = [slice((h // A_GROUP) * A_HEAD_DIM, (h // A_GROUP + 1) * A_HEAD_DIM) for h in heads]
    slope = [2.0 ** (-8.0 * (h + 1) / A_HEADS) for h in heads]
    qh = [q[:, hsl[h]].astype(BF16) for h in heads]
    s_c = [_dot_nt(qh[h], kc[:, ksl[h]]) for h in heads]
    s_p = [_dot_nt(qh[h], kp[:, ksl[h]]) for h in heads]
    s_c = [jnp.where(valid_c, s_c[h] * scale - slope[h] * dist_c, -jnp.inf) for h in heads]
    s_p = [jnp.where(valid_p, s_p[h] * scale - slope[h] * dist_p, -jnp.inf) for h in heads]
    m = [jnp.maximum(jnp.maximum(jnp.max(s_c[h], axis=-1, keepdims=True), jnp.max(s_p[h], axis=-1, keepdims=True)),
                     sink_ref[h]) for h in heads]
    p_c = [jnp.exp(s_c[h] - m[h]) for h in heads]
    p_p = [jnp.exp(s_p[h] - m[h]) for h in heads]
    den = [jnp.sum(p_c[h], axis=-1, keepdims=True) + jnp.sum(p_p[h], axis=-1, keepdims=True)
           + jnp.exp(sink_ref[h] - m[h]) for h in heads]
    o = [_dot(p_c[h].astype(BF16), vc[:, ksl[h]]) + _dot(p_p[h].astype(BF16), vp[:, ksl[h]]) for h in heads]
    o_ref[...] = jnp.concatenate([o[h] / den[h] for h in heads], axis=1).astype(o_ref.dtype)


def _swa_prompt(proj, sinks):
    bsz, t, _ = proj.shape
    nb = t // WINDOW
    kcol = A_Q // LANES
    vcol = kcol + 1

    def cur(col):
        return pl.BlockSpec((None, WINDOW, LANES), lambda b, i: (b, i, col))

    def prev(col):
        return pl.BlockSpec((None, WINDOW, LANES), lambda b, i: (b, jnp.maximum(i - 1, 0), col))

    return pl.pallas_call(
        _swa_kernel,
        grid=(bsz, nb),
        in_specs=[
            pl.BlockSpec(memory_space=pltpu.SMEM),
            pl.BlockSpec((None, WINDOW, A_Q), lambda b, i: (b, i, 0)),
            cur(kcol), prev(kcol), cur(vcol), prev(vcol),
        ],
        out_specs=pl.BlockSpec((None, WINDOW, A_Q), lambda b, i: (b, i, 0)),
        out_shape=jax.ShapeDtypeStruct((bsz, t, A_Q), BF16),
        compiler_params=_params(("parallel", "parallel")),
        name="swa_prompt",
    )(sinks, proj, proj, proj, proj, proj)


def _split3(x):
    hi = x.astype(BF16)
    r1 = x - hi.astype(F32)
    mid = r1.astype(BF16)
    lo = (r1 - mid.astype(F32)).astype(BF16)
    return hi, mid, lo


def _hgrn_kernel(q0_ref, q1_ref, lf0_ref, lf1_ref, k0_ref, k1_ref, v0_ref, v1_ref, g0_ref, g1_ref, ng_ref,
                 o_ref, s_ref, st_ref):
    i = pl.program_id(1)

    @pl.when(i == 0)
    def _():
        st_ref[...] = jnp.zeros_like(st_ref)

    n = HGRN_BLOCK
    nsub = n // HGRN_SUB
    heads = range(B_HEADS)
    qb = jnp.concatenate([q0_ref[...], q1_ref[...]], axis=1)
    lf = jnp.concatenate([lf0_ref[...], lf1_ref[...]], axis=1)
    lk = jnp.concatenate([k0_ref[...], k1_ref[...]], axis=1)
    v = jnp.concatenate([v0_ref[...], v1_ref[...]], axis=1)
    r = lax.broadcasted_iota(jnp.int32, (n, n), 0)
    c = lax.broadcasted_iota(jnp.int32, (n, n), 1)
    tri = jnp.where((c <= r) & (c >= r - (r & (HGRN_SUB - 1))), 1.0, 0.0).astype(BF16)
    lf_hi, lf_mid, lf_lo = _split3(lf)
    a_all = _dot(tri, lf_hi) + (_dot(tri, lf_mid) + _dot(tri, lf_lo))
    a3 = a_all.reshape(nsub, HGRN_SUB, B_HEADS * B_DIM)
    a_last3 = a3[:, HGRN_SUB - 1:HGRN_SUB, :]
    qb3 = qb.reshape(a3.shape)
    v3 = v.reshape(a3.shape)
    b3 = a3 - lk.reshape(a3.shape)
    qe3 = qb3 * jnp.exp2(a3)
    kd3 = jnp.exp2(a_last3 - b3)
    dec3 = jnp.exp2(a_last3)
    trow = lax.broadcasted_iota(jnp.int32, (1, HGRN_SUB, 1), 1)
    o_diag = []
    v_t = []
    for h in heads:
        hs = slice(h * B_DIM, (h + 1) * B_DIM)
        a_h, q_h, b_h, v_h = a3[:, :, hs], qb3[:, :, hs], b3[:, :, hs], v3[:, :, hs]
        acc = jnp.zeros((nsub, HGRN_SUB, B_DIM), F32)
        for s in range(HGRN_SUB):
            e = jnp.exp2(jnp.where(trow >= s, a_h - b_h[:, s:s + 1, :], -jnp.inf))
            col = jnp.sum(q_h * e, axis=-1, keepdims=True)
            acc = acc + col * v_h[:, s:s + 1, :]
        o_diag.append(acc.reshape(n, B_DIM))
        v_t.append(v[:, hs].T.astype(BF16))
    state = [st_ref[h] for h in heads]
    o_state = [[] for _ in heads]
    for ch in range(nsub):
        sl = slice(ch * HGRN_SUB, (ch + 1) * HGRN_SUB)
        for h in heads:
            hs = slice(h * B_DIM, (h + 1) * B_DIM)
            o_state[h].append(_dot_nt(qe3[ch, :, hs].astype(BF16), state[h].astype(BF16)))
            state[h] = state[h] * dec3[ch, :, hs] + _dot(v_t[h][:, sl], kd3[ch, :, hs].astype(BF16))
    g = jnp.concatenate([g0_ref[...], g1_ref[...]], axis=1)
    for h in heads:
        hs = slice(h * B_DIM, (h + 1) * B_DIM)
        st_ref[h] = state[h]
        o = jnp.concatenate(o_state[h], axis=0) + o_diag[h]
        y = o * lax.rsqrt(jnp.mean(o * o, axis=-1, keepdims=True) + NORM_EPS)
        o_ref[:, hs] = (y * ng_ref[...] * g[:, hs]).astype(o_ref.dtype)

    @pl.when(i == pl.num_programs(1) - 1)
    def _():
        for h in heads:
            s_ref[h] = state[h].T


def _hgrn_prompt(proj, norm_g):
    bsz, t, _ = proj.shape
    nh = B_HEADS
    width = nh * B_DIM
    half = width // 2
    base = (A_Q + 2 * A_KV) // half

    def col(off):
        return pl.BlockSpec((None, HGRN_BLOCK, half), lambda b, i: (b, i, base + off))

    return pl.pallas_call(
        _hgrn_kernel,
        grid=(bsz, t // HGRN_BLOCK),
        in_specs=[
            col(0), col(1), col(2), col(3), col(4), col(5), col(6), col(7), col(8), col(9),
            pl.BlockSpec((1, B_DIM), lambda b, i: (0, 0)),
        ],
        out_specs=[
            pl.BlockSpec((None, HGRN_BLOCK, width), lambda b, i: (b, i, 0)),
            pl.BlockSpec((None, nh, B_DIM, B_DIM), lambda b, i: (b, 0, 0, 0)),
        ],
        out_shape=[
            jax.ShapeDtypeStruct((bsz, t, width), BF16),
            jax.ShapeDtypeStruct((bsz, nh, B_DIM, B_DIM), F32),
        ],
        scratch_shapes=[pltpu.VMEM((nh, B_DIM, B_DIM), F32)],
        compiler_params=_params(("parallel", "arbitrary")),
        name="hgrn_prompt",
    )(*([proj] * 10), norm_g.reshape(1, B_DIM))


def _gdn_kernel(q_ref, k_ref, v_ref, gate_ref, ab_ref, alog_ref, dtb_ref, ng_ref, o_ref, s_out_ref, s_ref):
    i = pl.program_id(1)
    n = GDN_CHUNK

    @pl.when(i == 0)
    def _():
        s_ref[...] = jnp.zeros_like(s_ref)

    r = lax.broadcasted_iota(jnp.int32, (n, n), 0)
    c = lax.broadcasted_iota(jnp.int32, (n, n), 1)
    causal = c <= r
    strict = c < r
    tri = jnp.where(causal, 1.0, 0.0).astype(F32)
    eye = jnp.where(r == c, 1.0, 0.0).astype(F32)
    base_mask = (r >> GDN_BASE_LOG) == (c >> GDN_BASE_LOG)
    level_masks = [((r >> (k + 1)) == (c >> (k + 1))) & ((r >> k) != (c >> k))
                   for k in range(GDN_BASE_LOG, GDN_CHUNK_LOG)]
    chunks = range(GDN_STEP_CHUNKS)
    heads = range(C_HEADS)
    g_cum, g_cum_t, beta_all = [], [], []
    for ch in chunks:
        ab = ab_ref[ch * n:(ch + 1) * n, :]
        log_alpha = -jnp.exp(alog_ref[...]) * _softplus(ab + dtb_ref[...])
        beta_all.append(jax.nn.sigmoid(ab))
        g_cum.append(_dot(tri, log_alpha, HIGHEST))
        g_cum_t.append(g_cum[ch].T)

    items = [(ch, h) for ch in chunks for h in heads]
    pairs = range(len(items))

    def tile(ref, ch, h):
        return ref[ch * n:(ch + 1) * n, h * C_DIM:(h + 1) * C_DIM]

    qn = [tile(q_ref, ch, h) for ch, h in items]
    kn = [tile(k_ref, ch, h) for ch, h in items]
    vh = [tile(v_ref, ch, h) for ch, h in items]
    kn_b = [kn[j].astype(BF16) for j in pairs]
    g_col = [g_cum[ch][:, h:h + 1] for ch, h in items]
    beta = [beta_all[ch][:, C_HEADS + h:C_HEADS + h + 1] for ch, h in items]
    dec_incl = [jnp.exp(jnp.where(causal, g_col[j] - g_cum_t[ch][h:h + 1, :], -jnp.inf))
                for j, (ch, h) in enumerate(items)]
    e_g = [jnp.exp(g_col[j]) for j in pairs]
    kkt = [_dot_nt(kn_b[j], kn_b[j]) for j in pairs]
    qk = [_dot_nt(qn[j].astype(BF16), kn_b[j]) for j in pairs]
    l_mat = [beta[j] * kkt[j] * jnp.where(strict, dec_incl[j], 0.0) for j in pairs]
    a_pow = [jnp.where(base_mask, -l_mat[j], 0.0) for j in pairs]
    t_inv = [eye + a_pow[j] for j in pairs]
    for _ in range(GDN_BASE_LOG - 1):
        a_pow = [_dot_inv(a_pow[j], a_pow[j]) for j in pairs]
        t_inv = [t_inv[j] + _dot_inv(t_inv[j], a_pow[j]) for j in pairs]
    for lm in level_masks:
        x = [_dot_inv(jnp.where(lm, l_mat[j], 0.0), t_inv[j]) for j in pairs]
        t_inv = [t_inv[j] - _dot_inv(t_inv[j], x[j]) for j in pairs]
    uw = [_dot_inv(t_inv[j], jnp.concatenate([beta[j] * vh[j], (beta[j] * e_g[j]) * kn[j]], axis=1))
          for j in pairs]
    w_b = [uw[j][:, C_DIM:].astype(BF16) for j in pairs]
    qe_b = [(qn[j] * e_g[j]).astype(BF16) for j in pairs]
    qk_b = [(qk[j] * dec_incl[j]).astype(BF16) for j in pairs]
    g_last = [g_col[j][n - 1:n] for j in pairs]
    kd_t = [(kn[j] * jnp.exp(g_last[j] - g_col[j])).T.astype(BF16) for j in pairs]

    state = [s_ref[h] for h in heads]
    for ch in chunks:
        js = [ch * C_HEADS + h for h in heads]
        s_b = [state[h].astype(BF16) for h in heads]
        ws = [_dot(w_b[js[h]], s_b[h]) for h in heads]
        qs = [_dot(qe_b[js[h]], s_b[h]) for h in heads]
        delta_b = [(uw[js[h]][:, :C_DIM] - ws[h]).astype(BF16) for h in heads]
        o = [qs[h] + _dot(qk_b[js[h]], delta_b[h]) for h in heads]
        state = [jnp.exp(g_last[js[h]]) * state[h] + _dot(kd_t[js[h]], delta_b[h]) for h in heads]
        for h in heads:
            hs = slice(h * C_DIM, (h + 1) * C_DIM)
            y = o[h] * lax.rsqrt(jnp.mean(o[h] * o[h], axis=-1, keepdims=True) + NORM_EPS)
            o_ref[ch * n:(ch + 1) * n, hs] = (y * ng_ref[...] * tile(gate_ref, ch, h)).astype(o_ref.dtype)
    for h in heads:
        s_ref[h] = state[h]

    @pl.when(i == pl.num_programs(1) - 1)
    def _():
        s_out_ref[...] = s_ref[...]


def _gdn_prompt(proj, alog_row, dtb_row, norm_g):
    bsz, t, _ = proj.shape
    n = GDN_CHUNK * GDN_STEP_CHUNKS

    def col(cidx):
        return pl.BlockSpec((None, n, C_QK), lambda b, i: (b, i, cidx))

    one = pl.BlockSpec((1, LANES), lambda b, i: (0, 0))
    return pl.pallas_call(
        _gdn_kernel,
        grid=(bsz, t // n),
        in_specs=[
            col(0), col(1), col(2), col(3),
            pl.BlockSpec((None, n, LANES), lambda b, i: (b, i, (C_CONV_DIM + C_QK) // LANES)),
            one, one, one,
        ],
        out_specs=[
            pl.BlockSpec((None, n, C_QK), lambda b, i: (b, i, 0)),
            pl.BlockSpec((None, C_HEADS, C_DIM, C_DIM), lambda b, i: (b, 0, 0, 0)),
        ],
        out_shape=[
            jax.ShapeDtypeStruct((bsz, t, C_QK), BF16),
            jax.ShapeDtypeStruct((bsz, C_HEADS, C_DIM, C_DIM), F32),
        ],
        scratch_shapes=[pltpu.VMEM((C_HEADS, C_DIM, C_DIM), F32)],
        compiler_params=_params(("parallel", "arbitrary")),
        name="gdn_prompt",
    )(proj, proj, proj, proj, proj, alog_row, dtb_row, norm_g.reshape(1, C_DIM))


def _swa_dec_kernel(sink_ref, q_ref, kn_ref, vn_ref, ck_ref, cv_ref, o_ref, ok_ref, ov_ref):
    batch = range(q_ref.shape[0])
    row = lax.broadcasted_iota(jnp.int32, (WINDOW, LANES), 0)
    head = lax.broadcasted_iota(jnp.int32, (A_HEADS, WINDOW), 0)
    dist = (WINDOW - 1 - lax.broadcasted_iota(jnp.int32, (A_HEADS, WINDOW), 1)).astype(F32)
    scale = A_HEAD_DIM ** -0.5
    slope = jnp.zeros((A_HEADS, WINDOW), F32)
    sink = jnp.zeros((A_HEADS, 1), F32)
    for h in range(A_HEADS):
        slope = jnp.where(head == h, 2.0 ** (-8.0 * (h + 1) / A_HEADS), slope)
        sink = jnp.where(head[:, 0:1] == h, sink_ref[h], sink)
    bias = slope * dist
    first_kv = head[:, 0:A_HEAD_DIM] < A_GROUP
    keys = [jnp.where(row == WINDOW - 1, kn_ref[b], pltpu.roll(ck_ref[b], WINDOW - 1, 0)) for b in batch]
    vals = [jnp.where(row == WINDOW - 1, vn_ref[b], pltpu.roll(cv_ref[b], WINDOW - 1, 0)) for b in batch]
    for b in batch:
        ok_ref[b] = keys[b]
        ov_ref[b] = vals[b]
    q8 = [jnp.concatenate([jnp.where(first_kv, q_ref[b], 0.0), jnp.where(first_kv, 0.0, q_ref[b])],
                          axis=1).astype(BF16) for b in batch]
    s = [_dot_nt(q8[b], keys[b].astype(BF16)) * scale - bias for b in batch]
    m = [jnp.maximum(jnp.max(s[b], axis=-1, keepdims=True), sink) for b in batch]
    p = [jnp.exp(s[b] - m[b]) for b in batch]
    den = [jnp.sum(p[b], axis=-1, keepdims=True) + jnp.exp(sink - m[b]) for b in batch]
    o = [_dot(p[b].astype(BF16), vals[b].astype(BF16)) for b in batch]
    for b in batch:
        o_ref[b] = jnp.where(first_kv, o[b][:, :A_HEAD_DIM], o[b][:, A_HEAD_DIM:]) / den[b]


def _swa_decode(proj2, cache_k, cache_v, sinks):
    nbatch = proj2.shape[0]
    bb = 8
    q3 = proj2[:, :A_Q].reshape(nbatch, A_HEADS, A_HEAD_DIM)
    kn = proj2[:, A_Q:A_Q + A_KV].reshape(nbatch, 1, A_KV)
    vn = proj2[:, A_Q + A_KV:A_Q + 2 * A_KV].reshape(nbatch, 1, A_KV)
    ck = cache_k.reshape(nbatch, WINDOW, A_KV)
    cv = cache_v.reshape(nbatch, WINDOW, A_KV)
    qspec = pl.BlockSpec((bb, A_HEADS, A_HEAD_DIM), lambda j: (j, 0, 0))
    nspec = pl.BlockSpec((bb, 1, A_KV), lambda j: (j, 0, 0))
    cspec = pl.BlockSpec((bb, WINDOW, A_KV), lambda j: (j, 0, 0))
    o, nk, nv = pl.pallas_call(
        _swa_dec_kernel,
        grid=(nbatch // bb,),
        in_specs=[pl.BlockSpec(memory_space=pltpu.SMEM), qspec, nspec, nspec, cspec, cspec],
        out_specs=[qspec, cspec, cspec],
        out_shape=[
            jax.ShapeDtypeStruct((nbatch, A_HEADS, A_HEAD_DIM), F32),
            jax.ShapeDtypeStruct((nbatch, WINDOW, A_KV), F32),
            jax.ShapeDtypeStruct((nbatch, WINDOW, A_KV), F32),
        ],
        compiler_params=_params(("parallel",)),
        name="swa_decode",
    )(sinks, q3, kn, vn, ck, cv)
    shape5 = (nbatch, WINDOW, A_KV_HEADS, A_HEAD_DIM)
    return o.reshape(nbatch, A_Q), nk.reshape(shape5), nv.reshape(shape5)


def _cols(rows, heads):
    nbatch = rows.shape[0]
    x = rows.reshape(nbatch // DEC_BLOCK, DEC_BLOCK, heads, LANES)
    return x.transpose(2, 0, 3, 1)


def _hgrn_dec_kernel(qt_ref, ft_ref, lb_ref, v_ref, g_ref, ng_ref, s_ref, o_ref, so_ref, orow_ref):
    lb = lb_ref[...]
    qc = _silu(qt_ref[...])
    f = lb + (1.0 - lb) * jax.nn.sigmoid(ft_ref[...])
    kk = 1.0 - f
    v = v_ref[...]
    for bb in range(DEC_BLOCK):
        s_new = f[:, bb:bb + 1] * s_ref[bb] + kk[:, bb:bb + 1] * v[bb:bb + 1, :]
        so_ref[bb] = s_new
        orow_ref[bb:bb + 1, :] = jnp.sum(qc[:, bb:bb + 1] * s_new, axis=0, keepdims=True)
    o_ref[...] = _rms_gate(orow_ref[...], ng_ref[...], g_ref[...])


def _hgrn_decode(proj2, state, lower, norm_g):
    nbatch = proj2.shape[0]
    nh = B_HEADS
    base = A_Q + 2 * A_KV
    width = nh * B_DIM
    qt = _cols(proj2[:, base:base + width], nh)
    ft = _cols(proj2[:, base + width:base + 2 * width], nh)
    colspec = pl.BlockSpec((None, None, B_DIM, DEC_BLOCK), lambda h, j: (h, j, 0, 0))
    sspec = pl.BlockSpec((DEC_BLOCK, None, B_DIM, B_DIM), lambda h, j: (j, h, 0, 0))
    cb = base // LANES
    o, s_new = pl.pallas_call(
        _hgrn_dec_kernel,
        grid=(nh, nbatch // DEC_BLOCK),
        in_specs=[
            colspec, colspec,
            pl.BlockSpec((None, B_DIM, 1), lambda h, j: (h, 0, 0)),
            pl.BlockSpec((DEC_BLOCK, B_DIM), lambda h, j: (j, cb + 2 * nh + h)),
            pl.BlockSpec((DEC_BLOCK, B_DIM), lambda h, j: (j, cb + 3 * nh + h)),
            pl.BlockSpec((1, B_DIM), lambda h, j: (0, 0)),
            sspec,
        ],
        out_specs=[pl.BlockSpec((DEC_BLOCK, B_DIM), lambda h, j: (j, h)), sspec],
        out_shape=[
            jax.ShapeDtypeStruct((nbatch, width), F32),
            jax.ShapeDtypeStruct(state.shape, F32),
        ],
        scratch_shapes=[pltpu.VMEM((DEC_BLOCK, B_DIM), F32)],
        compiler_params=_params(("parallel", "parallel")),
        name="hgrn_decode",
    )(qt, ft, lower.reshape(nh, B_DIM, 1), proj2, proj2, norm_g.reshape(1, B_DIM), state)
    return o, s_new


def _gdn_prep_kernel(p_ref, h0_ref, h1_ref, h2_ref, cw_ref, alog_ref, dtb_ref, q_ref, k_ref, v_ref, ab_ref):
    acc = (h0_ref[...] * cw_ref[0:1, :] + h1_ref[...] * cw_ref[1:2, :] + h2_ref[...] * cw_ref[2:3, :]
           + p_ref[:, 0:C_CONV_DIM] * cw_ref[3:4, :])
    qkv = _silu(acc)
    for h in range(C_HEADS):
        hs = slice(h * C_DIM, (h + 1) * C_DIM)
        qh = qkv[:, h * C_DIM:(h + 1) * C_DIM]
        kh = qkv[:, C_QK + h * C_DIM:C_QK + (h + 1) * C_DIM]
        q_ref[:, hs] = qh * lax.rsqrt(jnp.sum(qh * qh, axis=-1, keepdims=True) + NORM_EPS) * (C_DIM ** -0.5)
        k_ref[:, hs] = kh * lax.rsqrt(jnp.sum(kh * kh, axis=-1, keepdims=True) + NORM_EPS)
    v_ref[...] = qkv[:, 2 * C_QK:3 * C_QK]
    ab = p_ref[:, C_CONV_DIM + C_QK:C_CONV_DIM + C_QK + LANES]
    alpha = jnp.exp(-jnp.exp(alog_ref[...]) * _softplus(ab + dtb_ref[...]))
    beta = jax.nn.sigmoid(ab)
    lane = lax.broadcasted_iota(jnp.int32, ab.shape, 1)
    ab_ref[...] = jnp.where(lane < C_HEADS, alpha, beta)


def _gdn_dec_kernel(qt_ref, kt_ref, v_ref, gate_ref, al_ref, be_ref, ng_ref, s_ref, o_ref, so_ref, orow_ref):
    qt = qt_ref[...]
    kt = kt_ref[...]
    v = v_ref[...]
    al = al_ref[...]
    be = be_ref[...]
    for bb in range(DEC_BLOCK):
        s = s_ref[bb]
        kc = kt[:, bb:bb + 1]
        a = al[bb:bb + 1, :]
        ks = jnp.sum(kc * s, axis=0, keepdims=True)
        delta = be[bb:bb + 1, :] * (v[bb:bb + 1, :] - a * ks)
        s_new = a * s + kc * delta
        so_ref[bb] = s_new
        orow_ref[bb:bb + 1, :] = jnp.sum(qt[:, bb:bb + 1] * s_new, axis=0, keepdims=True)
    o_ref[...] = _rms_gate(orow_ref[...], ng_ref[...], gate_ref[...])


def _gdn_decode(proj2, hist, state, conv_w, alog_row, dtb_row, norm_g):
    nbatch = proj2.shape[0]
    nh = C_HEADS
    full = lambda shape: pl.BlockSpec(shape, lambda: tuple(0 for _ in shape))
    hspec = full((nbatch, C_CONV_DIM))
    qn, kn, vc, ab = pl.pallas_call(
        _gdn_prep_kernel,
        grid=(),
        in_specs=[
            full((nbatch, ODD_IN_PAD)), hspec, hspec, hspec,
            full((C_CONV, C_CONV_DIM)), full((1, LANES)), full((1, LANES)),
        ],
        out_specs=[full((nbatch, C_QK)), full((nbatch, C_QK)), full((nbatch, C_QK)), full((nbatch, LANES))],
        out_shape=[
            jax.ShapeDtypeStruct((nbatch, C_QK), F32),
            jax.ShapeDtypeStruct((nbatch, C_QK), F32),
            jax.ShapeDtypeStruct((nbatch, C_QK), F32),
            jax.ShapeDtypeStruct((nbatch, LANES), F32),
        ],
        compiler_params=pltpu.CompilerParams(vmem_limit_bytes=VMEM_LIMIT),
        name="gdn_decode_prep",
    )(proj2, hist[:, 0], hist[:, 1], hist[:, 2], conv_w, alog_row, dtb_row)
    qt = _cols(qn, nh)
    kt = _cols(kn, nh)
    al = jnp.broadcast_to(ab[:, :nh].T[:, :, None], (nh, nbatch, LANES))
    be = jnp.broadcast_to(ab[:, nh:2 * nh].T[:, :, None], (nh, nbatch, LANES))
    colspec = pl.BlockSpec((None, None, C_DIM, DEC_BLOCK), lambda h, j: (h, j, 0, 0))
    sspec = pl.BlockSpec((DEC_BLOCK, None, C_DIM, C_DIM), lambda h, j: (j, h, 0, 0))
    rspec = pl.BlockSpec((None, DEC_BLOCK, LANES), lambda h, j: (h, j, 0))
    gcol = C_CONV_DIM // LANES
    o, s_new = pl.pallas_call(
        _gdn_dec_kernel,
        grid=(nh, nbatch // DEC_BLOCK),
        in_specs=[
            colspec, colspec,
            pl.BlockSpec((DEC_BLOCK, C_DIM), lambda h, j: (j, h)),
            pl.BlockSpec((DEC_BLOCK, C_DIM), lambda h, j: (j, gcol + h)),
            rspec, rspec,
            pl.BlockSpec((1, C_DIM), lambda h, j: (0, 0)),
            sspec,
        ],
        out_specs=[pl.BlockSpec((DEC_BLOCK, C_DIM), lambda h, j: (j, h)), sspec],
        out_shape=[
            jax.ShapeDtypeStruct((nbatch, C_QK), F32),
            jax.ShapeDtypeStruct(state.shape, F32),
        ],
        scratch_shapes=[pltpu.VMEM((DEC_BLOCK, C_DIM), F32)],
        compiler_params=_params(("parallel", "parallel")),
        name="gdn_decode",
    )(qt, kt, vc, proj2, al, be, norm_g.reshape(1, C_DIM), state)
    return o, s_new


def _trunk(x, mods, caches, p, tm):
    decode = caches is not None
    m = mods[0]
    ffn = functools.partial(_ffn, w_up=p["w_up"], w_down=p["w_down"], ln_g=p["ln_g"], ln_b=p["ln_b"], tm=tm)
    x = ffn(x, m, 0, layer=0, sub=0)
    if decode:
        proj = _inproj(x, m, 3, p["even_in"], tm)
        o_a, new_k, new_v = _swa_decode(proj[0], caches[0][0], caches[1][0], p["sinks"])
        o_b, s_hgrn = _hgrn_decode(proj[0], caches[2][0], p["lower"], p["hgrn_norm_g"])
        o_a, o_b = o_a[None], o_b[None]
    else:
        bsz = x.shape[0]
        proj = _inproj_even(x, m, 3, p["even_in"], p["lower"], tm)
        o_a = _swa_prompt(proj, p["sinks"])
        new_k = proj[:, -WINDOW:, A_Q:A_Q + A_KV].reshape(bsz, WINDOW, A_KV_HEADS, A_HEAD_DIM)
        new_v = proj[:, -WINDOW:, A_Q + A_KV:A_Q + 2 * A_KV].reshape(bsz, WINDOW, A_KV_HEADS, A_HEAD_DIM)
        o_b, s_hgrn = _hgrn_prompt(proj, p["hgrn_norm_g"])
    x = ffn(x, m, 6, layer=0, sub=1, mixer=(o_a, 0, o_b, 0, p["even_out"], 5))
    m = mods[1]
    x = ffn(x, m, 0, layer=1, sub=0)
    if decode:
        proj = _inproj(x, m, 3, p["odd_in"], tm)
        hist = caches[4][0]
        o_c, s_gdn = _gdn_decode(proj[0], hist, caches[3][0], p["conv_w"], p["alog_row"], p["dtb_row"],
                                 p["gdn_norm_g"])
        o_c = o_c[None]
        new_hist = jnp.concatenate([hist[:, 1:], proj[0][:, None, :C_CONV_DIM]], axis=1)
    else:
        proj, last_rows = _inproj_gdn(x, m, 3, p["odd_in"], p["conv_w"], tm)
        o_c, s_gdn = _gdn_prompt(proj, p["alog_row"], p["dtb_row"], p["gdn_norm_g"])
        new_hist = last_rows[:, -(C_CONV - 1):]
    x = ffn(x, m, 6, layer=1, sub=1, mixer=(o_c, 0, o_c, 1, p["odd_out"], 5))
    return x, new_k[None], new_v[None], s_hgrn[None], s_gdn[None], new_hist[None]


def kernel(x_prompt, x_sample, cache_swa_k, cache_swa_v, state_hgrn, state_gdn, state_gdn_conv, c_prompt, c_sample, ada_w, ada_b, ln_g, ln_b, ffn_w_up, ffn_w_down, even_w_in, even_w_out, swa_sinks, hgrn_norm_g, hgrn_lb_logits, odd_w_in, odd_w_out, gdn_conv_w, gdn_a_log, gdn_dt_bias, gdn_norm_g):
    n_prompt = c_prompt.shape[0]
    n_sample = c_sample.shape[0]
    pad_rows = (-(n_prompt + n_sample)) % 8
    c_all = jnp.concatenate([c_prompt, c_sample, jnp.zeros((pad_rows, D_MODEL), F32)], axis=0)
    mods = _ada_mods(c_all, ada_w, ada_b)
    mods_p = mods[:, :n_prompt].reshape(DEPTH, n_prompt, 1, N_MOD * D_MODEL)
    mods_s = mods[:, n_prompt:n_prompt + n_sample].reshape(DEPTH, 1, n_sample, N_MOD * D_MODEL)

    probs = jax.nn.softmax(hgrn_lb_logits.astype(F32), axis=0)
    lower = (jnp.cumsum(probs, axis=0)[1:] - probs[0])[0]

    def lane_row(v):
        return jnp.pad(v.astype(F32), (0, LANES - v.shape[0])).reshape(1, LANES)

    p = dict(
        w_up=ffn_w_up.astype(BF16), w_down=ffn_w_down.astype(BF16),
        ln_g=ln_g, ln_b=ln_b,
        even_in=even_w_in[0].astype(BF16), even_out=even_w_out[0].astype(BF16),
        odd_in=jnp.concatenate([odd_w_in[0].astype(BF16), jnp.zeros((D_MODEL, ODD_IN_PAD - ODD_IN), BF16)], axis=1),
        odd_out=odd_w_out[0].astype(BF16),
        sinks=swa_sinks[0], lower=lower, hgrn_norm_g=hgrn_norm_g[0],
        conv_w=gdn_conv_w[0], alog_row=lane_row(gdn_a_log[0]), dtb_row=lane_row(gdn_dt_bias[0]),
        gdn_norm_g=gdn_norm_g[0],
    )
    y_p, p_k, p_v, p_hgrn, p_gdn, p_conv = _trunk(x_prompt, mods_p, None, p, 512)
    caches = (cache_swa_k, cache_swa_v, state_hgrn, state_gdn, state_gdn_conv)
    x_s = x_sample.reshape(1, n_sample, D_MODEL)
    y_s, s_k, s_v, s_hgrn, s_gdn, s_conv = _trunk(x_s, mods_s, caches, p, n_sample)
    y_s = y_s.reshape(n_sample, 1, D_MODEL)
    return (y_p, y_s, p_k, p_v, p_hgrn, p_gdn, p_conv, s_k, s_v, s_hgrn, s_gdn, s_conv)
```

```python
import functools

import jax
import jax.numpy as jnp
from jax import lax
from jax.experimental import pallas as pl
from jax.experimental.pallas import tpu as pltpu

F32 = jnp.float32
BF16 = jnp.bfloat16
HIGHEST = lax.Precision.HIGHEST

D_MODEL = 1024
DEPTH = 2
WINDOW = 128
A_HEADS = 8
A_KV_HEADS = 2
A_GROUP = A_HEADS // A_KV_HEADS
A_HEAD_DIM = 64
A_Q = A_HEADS * A_HEAD_DIM
A_KV = A_KV_HEADS * A_HEAD_DIM
B_HEADS = 4
B_DIM = 128
EVEN_IN = 2816
C_HEADS = 8
C_DIM = 128
C_QK = C_HEADS * C_DIM
C_CONV = 4
C_CONV_DIM = 3 * C_QK
ODD_IN = C_CONV_DIM + C_QK + 2 * C_HEADS
ODD_IN_PAD = 4224
D_FF = 2816
N_MOD = 9
DN_ALPHA = (2 * DEPTH) ** 0.25
LN_EPS = 1e-5
NORM_EPS = 1e-6
LANES = 128

FF_CHUNK = 256
FFN_TM = 1024
FFN_VMEM_LIMIT = 56 << 20
PROJ_TILE = 1408
SWA_STEP_BLOCKS = 2
HGRN_BLOCK = 256
HGRN_SUB = 8
GDN_CHUNK_LOG = 7
GDN_CHUNK = 1 << GDN_CHUNK_LOG
GDN_BASE_LOG = 3
GDN_STEP_CHUNKS = 2
DEC_BLOCK = 64
VMEM_LIMIT = 48 << 20


def _params(sem, vmem=VMEM_LIMIT):
    return pltpu.CompilerParams(dimension_semantics=sem, vmem_limit_bytes=vmem)


def _silu(x):
    return x * jax.nn.sigmoid(x)


def _softplus(x):
    return jnp.maximum(x, 0.0) + jnp.log(1.0 + jnp.exp(-jnp.abs(x)))


def _layer_norm(y, g, b):
    mu = jnp.mean(y, axis=-1, keepdims=True)
    d = y - mu
    var = jnp.mean(d * d, axis=-1, keepdims=True)
    return d * lax.rsqrt(var + LN_EPS) * g + b


def _rms_gate(o, norm_g, gate):
    y = o * lax.rsqrt(jnp.mean(o * o, axis=-1, keepdims=True) + NORM_EPS)
    return y * norm_g * _silu(gate)


def _dot(a, b, precision=None):
    return jnp.dot(a, b, preferred_element_type=F32, precision=precision)


def _dot_inv(a, b):
    return _dot(a.astype(BF16), b.astype(BF16))


def _dot_nt(a, b, precision=None):
    return lax.dot_general(a, b, (((1,), (1,)), ((), ())), preferred_element_type=F32, precision=precision)


def _ada_kernel(c_ref, w_ref, b_ref, o_ref):
    cs = _silu(c_ref[...]).astype(BF16)
    o_ref[...] = _dot(cs, w_ref[...].astype(BF16)) + b_ref[...]


def _ada_mods(c_all, ada_w, ada_b):
    m = c_all.shape[0]
    n = N_MOD * D_MODEL
    tn = 1152
    return pl.pallas_call(
        _ada_kernel,
        grid=(DEPTH, n // tn),
        in_specs=[
            pl.BlockSpec((m, D_MODEL), lambda l, j: (0, 0)),
            pl.BlockSpec((None, D_MODEL, tn), lambda l, j: (l, 0, j)),
            pl.BlockSpec((None, 1, tn), lambda l, j: (l, 0, j)),
        ],
        out_specs=pl.BlockSpec((None, m, tn), lambda l, j: (l, 0, j)),
        out_shape=jax.ShapeDtypeStruct((DEPTH, m, n), F32),
        compiler_params=_params(("parallel", "parallel")),
        name="ada_mods",
    )(c_all, ada_w, ada_b.reshape(DEPTH, 1, n))


def _mod_spec(mods, k, tm, grid_rank):
    per_token = mods.shape[1] != 1
    rows = tm if per_token else 1
    if grid_rank == 3:
        return pl.BlockSpec((None, rows, D_MODEL), lambda b, i, j: (b, i if per_token else 0, k))
    return pl.BlockSpec((None, rows, D_MODEL), lambda b, i: (b, i if per_token else 0, k))


def _ffn_body(x, sh_ref, sc_ref, g_ref, wu_ref, wd_ref, lg_ref, lb_ref, o_ref):
    h = (x * (1.0 + sc_ref[...]) + sh_ref[...]).astype(BF16)
    for c in range(D_FF // FF_CHUNK):
        lo = c * FF_CHUNK
        gate = _dot(h, wu_ref[:, lo:lo + FF_CHUNK])
        up = _dot(h, wu_ref[:, D_FF + lo:D_FF + lo + FF_CHUNK])
        act = (_silu(gate) * up).astype(BF16)
        part = _dot(act, wd_ref[lo:lo + FF_CHUNK, :])
        if c == 0:
            o_ref[...] = part
        else:
            o_ref[...] += part
    y = DN_ALPHA * x + (0.5 * g_ref[...]) * o_ref[...]
    o_ref[...] = _layer_norm(y, lg_ref[...], lb_ref[...])


def _ffn_kernel(x_ref, sh_ref, sc_ref, g_ref, wu_ref, wd_ref, lg_ref, lb_ref, o_ref):
    _ffn_body(x_ref[...], sh_ref, sc_ref, g_ref, wu_ref, wd_ref, lg_ref, lb_ref, o_ref)


def _mix_ffn_kernel(x_ref, o1_ref, o2_ref, gm_ref, w1_ref, w2_ref, lgm_ref, lbm_ref,
                    sh_ref, sc_ref, g_ref, wu_ref, wd_ref, lg_ref, lb_ref, o_ref):
    mix = _dot(o1_ref[...].astype(BF16), w1_ref[...]) + _dot(o2_ref[...].astype(BF16), w2_ref[...])
    x1 = _layer_norm(DN_ALPHA * x_ref[...] + gm_ref[...] * mix, lgm_ref[...], lbm_ref[...])
    _ffn_body(x1, sh_ref, sc_ref, g_ref, wu_ref, wd_ref, lg_ref, lb_ref, o_ref)


def _ffn(x, mods, k0, w_up, w_down, ln_g, ln_b, layer, sub, tm, mixer=None):
    bsz, t, _ = x.shape
    tm = min(t, max(tm, FFN_TM))
    half = D_MODEL // 2
    row = pl.BlockSpec((None, tm, D_MODEL), lambda b, i: (b, i, 0))
    vec = pl.BlockSpec((1, D_MODEL), lambda b, i: (0, 0))
    resident = pl.Buffered(1)
    ffn_specs = [
        _mod_spec(mods, k0, tm, 2), _mod_spec(mods, k0 + 1, tm, 2), _mod_spec(mods, k0 + 2, tm, 2),
        pl.BlockSpec((None, None, D_MODEL, 2 * D_FF), lambda b, i: (layer, sub, 0, 0), pipeline_mode=resident),
        pl.BlockSpec((None, None, D_FF, D_MODEL), lambda b, i: (layer, sub, 0, 0), pipeline_mode=resident),
        vec, vec,
    ]
    ffn_args = (mods, mods, mods, w_up, w_down,
                ln_g[layer, 2 * sub].reshape(1, D_MODEL), ln_b[layer, 2 * sub].reshape(1, D_MODEL))
    if mixer is None:
        body, specs, args = _ffn_kernel, [row] + ffn_specs, (x,) + ffn_args
    else:
        o1, c1, o2, c2, w_out, gate_k = mixer
        body = _mix_ffn_kernel
        specs = [
            row,
            pl.BlockSpec((None, tm, half), lambda b, i: (b, i, c1)),
            pl.BlockSpec((None, tm, half), lambda b, i: (b, i, c2)),
            _mod_spec(mods, gate_k, tm, 2),
            pl.BlockSpec((half, D_MODEL), lambda b, i: (0, 0), pipeline_mode=resident),
            pl.BlockSpec((half, D_MODEL), lambda b, i: (1, 0), pipeline_mode=resident),
            vec, vec,
        ] + ffn_specs
        args = (x, o1, o2, mods, w_out, w_out,
                ln_g[layer, 1].reshape(1, D_MODEL), ln_b[layer, 1].reshape(1, D_MODEL)) + ffn_args
    return pl.pallas_call(
        body,
        grid=(bsz, t // tm),
        in_specs=specs,
        out_specs=row,
        out_shape=jax.ShapeDtypeStruct(x.shape, F32),
        compiler_params=_params(("parallel", "parallel"), FFN_VMEM_LIMIT),
        name="ffn" if mixer is None else "mix_ffn",
    )(*args)


def _inproj_kernel(x_ref, sh_ref, sc_ref, w_ref, o_ref):
    h = (x_ref[...] * (1.0 + sc_ref[...]) + sh_ref[...]).astype(BF16)
    for lo in range(0, w_ref.shape[1], PROJ_TILE):
        o_ref[:, lo:lo + PROJ_TILE] = _dot(h, w_ref[:, lo:lo + PROJ_TILE])


def _inproj(x, mods, k0, w, tm):
    bsz, t, _ = x.shape
    n = w.shape[1]
    return pl.pallas_call(
        _inproj_kernel,
        grid=(bsz, t // tm),
        in_specs=[
            pl.BlockSpec((None, tm, D_MODEL), lambda b, i: (b, i, 0)),
            _mod_spec(mods, k0, tm, 2), _mod_spec(mods, k0 + 1, tm, 2),
            pl.BlockSpec((D_MODEL, n), lambda b, i: (0, 0), pipeline_mode=pl.Buffered(1)),
        ],
        out_specs=pl.BlockSpec((None, tm, n), lambda b, i: (b, i, 0)),
        out_shape=jax.ShapeDtypeStruct((bsz, t, n), F32),
        compiler_params=_params(("parallel", "parallel")),
        name="inproj",
    )(x, mods, mods, w)


def _inproj_even_kernel(x_ref, sh_ref, sc_ref, w_ref, lb_ref, o_ref):
    h = (x_ref[...] * (1.0 + sc_ref[...]) + sh_ref[...]).astype(BF16)
    attn = A_Q + 2 * A_KV
    width = B_HEADS * B_DIM

    def proj(group):
        return _dot(h, w_ref[:, attn + group * width:attn + (group + 1) * width])

    o_ref[:, 0:attn] = _dot(h, w_ref[:, 0:attn])
    o_ref[:, attn:attn + width] = _silu(proj(0))
    lb = lb_ref[...]
    f = lb + (1.0 - lb) * jax.nn.sigmoid(proj(1))
    o_ref[:, attn + width:attn + 2 * width] = jnp.log2(f)
    o_ref[:, attn + 2 * width:attn + 3 * width] = jnp.log2(1.0 - f)
    o_ref[:, attn + 3 * width:attn + 4 * width] = proj(2)
    o_ref[:, attn + 4 * width:attn + 5 * width] = _silu(proj(3))


def _inproj_even(x, mods, k0, w, lower, tm):
    bsz, t, _ = x.shape
    n_in = w.shape[1]
    width = B_HEADS * B_DIM
    n_out = n_in + width
    return pl.pallas_call(
        _inproj_even_kernel,
        grid=(bsz, t // tm),
        in_specs=[
            pl.BlockSpec((None, tm, D_MODEL), lambda b, i: (b, i, 0)),
            _mod_spec(mods, k0, tm, 2), _mod_spec(mods, k0 + 1, tm, 2),
            pl.BlockSpec((D_MODEL, n_in), lambda b, i: (0, 0), pipeline_mode=pl.Buffered(1)),
            pl.BlockSpec((1, width), lambda b, i: (0, 0)),
        ],
        out_specs=pl.BlockSpec((None, tm, n_out), lambda b, i: (b, i, 0)),
        out_shape=jax.ShapeDtypeStruct((bsz, t, n_out), F32),
        compiler_params=_params(("parallel", "parallel")),
        name="inproj_even",
    )(x, mods, mods, w, lower.reshape(1, width))


def _inproj_gdn_kernel(x_ref, sh_ref, sc_ref, w_ref, cw_ref, o_ref, hist_ref, carry_ref):
    assert C_CONV == 4
    i = pl.program_id(1)
    tm = x_ref.shape[0]

    @pl.when(i == 0)
    def _():
        carry_ref[...] = jnp.zeros_like(carry_ref)

    h = (x_ref[...] * (1.0 + sc_ref[...]) + sh_ref[...]).astype(BF16)
    for sec in range(3):
        cs = slice(sec * C_QK, (sec + 1) * C_QK)
        raw = _dot(h, w_ref[:, cs])
        ext = jnp.concatenate([carry_ref[:, cs], raw], axis=0)
        prev = pltpu.roll(ext, 1, 0)
        near = ext * cw_ref[3:4, cs] + prev * cw_ref[2:3, cs]
        far = ext * cw_ref[1:2, cs] + prev * cw_ref[0:1, cs]
        z = near + pltpu.roll(far, 2, 0)
        act = _silu(z[8:])
        if sec == 2:
            o_ref[:, cs] = act
        else:
            for hd in range(C_HEADS):
                a = act[:, hd * C_DIM:(hd + 1) * C_DIM]
                inv = lax.rsqrt(jnp.sum(a * a, axis=-1, keepdims=True) + NORM_EPS)
                if sec == 0:
                    inv = inv * (C_DIM ** -0.5)
                o_ref[:, sec * C_QK + hd * C_DIM:sec * C_QK + (hd + 1) * C_DIM] = a * inv
        carry_ref[:, cs] = raw[tm - 8:tm]
        hist_ref[:, cs] = raw[tm - 8:tm]
    gs = slice(C_CONV_DIM, C_CONV_DIM + C_QK)
    o_ref[:, gs] = _silu(_dot(h, w_ref[:, gs]))
    o_ref[:, C_CONV_DIM + C_QK:] = _dot(h, w_ref[:, C_CONV_DIM + C_QK:])


def _inproj_gdn(x, mods, k0, w, conv_w, tm):
    bsz, t, _ = x.shape
    n = w.shape[1]
    return pl.pallas_call(
        _inproj_gdn_kernel,
        grid=(bsz, t // tm),
        in_specs=[
            pl.BlockSpec((None, tm, D_MODEL), lambda b, i: (b, i, 0)),
            _mod_spec(mods, k0, tm, 2), _mod_spec(mods, k0 + 1, tm, 2),
            pl.BlockSpec((D_MODEL, n), lambda b, i: (0, 0), pipeline_mode=pl.Buffered(1)),
            pl.BlockSpec((C_CONV, C_CONV_DIM), lambda b, i: (0, 0)),
        ],
        out_specs=[
            pl.BlockSpec((None, tm, n), lambda b, i: (b, i, 0)),
            pl.BlockSpec((None, 8, C_CONV_DIM), lambda b, i: (b, 0, 0)),
        ],
        out_shape=[
            jax.ShapeDtypeStruct((bsz, t, n), F32),
            jax.ShapeDtypeStruct((bsz, 8, C_CONV_DIM), F32),
        ],
        scratch_shapes=[pltpu.VMEM((8, C_CONV_DIM), F32)],
        compiler_params=_params(("parallel", "arbitrary")),
        name="inproj_gdn",
    )(x, mods, mods, w, conv_w)


def _swa_kernel(sink_ref, q_ref, kc_ref, kp_ref, vc_ref, vp_ref, o_ref):
    i = pl.program_id(1)
    k_all = jnp.concatenate([kp_ref[...], kc_ref[...]], axis=0).astype(BF16)
    v_all = jnp.concatenate([vp_ref[...], vc_ref[...]], axis=0).astype(BF16)
    r = lax.broadcasted_iota(jnp.int32, (WINDOW, 2 * WINDOW), 0)
    c = lax.broadcasted_iota(jnp.int32, (WINDOW, 2 * WINDOW), 1)
    dist = r + WINDOW - c
    in_window = (dist >= 0) & (dist < WINDOW)
    valid = [in_window & (c >= jnp.where(i > 0, 0, WINDOW)) if b == 0 else in_window
             for b in range(SWA_STEP_BLOCKS)]
    dist = dist.astype(F32)
    scale = A_HEAD_DIM ** -0.5
    ksl = [slice((h // A_GROUP) * A_HEAD_DIM, (h // A_GROUP + 1) * A_HEAD_DIM) for h in range(A_HEADS)]
    slope = [2.0 ** (-8.0 * (h + 1) / A_HEADS) for h in range(A_HEADS)]
    ones = jnp.ones((2 * WINDOW, A_KV), BF16)
    items = [(b, h) for b in range(SWA_STEP_BLOCKS) for h in range(A_HEADS)]
    pairs = range(len(items))
    qh = [q_ref[b * WINDOW:(b + 1) * WINDOW, h * A_HEAD_DIM:(h + 1) * A_HEAD_DIM].astype(BF16) for b, h in items]
    s = [_dot_nt(qh[j], k_all[b * WINDOW:(b + 2) * WINDOW, ksl[h]]) for j, (b, h) in enumerate(items)]
    s = [jnp.where(valid[b], s[j] * scale - slope[h] * dist, -jnp.inf) for j, (b, h) in enumerate(items)]
    m = [jnp.maximum(jnp.max(s[j], axis=-1, keepdims=True), sink_ref[h]) for j, (b, h) in enumerate(items)]
    p = [jnp.exp(s[j] - m[j]).astype(BF16) for j in pairs]
    o = [_dot(p[j], v_all[b * WINDOW:(b + 2) * WINDOW]) for j, (b, h) in enumerate(items)]
    den = [_dot(p[j], ones) + jnp.exp(sink_ref[h] - m[j]) for j, (b, h) in enumerate(items)]
    for b in range(SWA_STEP_BLOCKS):
        o_ref[b * WINDOW:(b + 1) * WINDOW, :] = jnp.concatenate(
            [(o[j] / den[j])[:, ksl[h]] for j, (bb, h) in enumerate(items) if bb == b], axis=1).astype(o_ref.dtype)


def _swa_prompt(proj, sinks):
    bsz, t, _ = proj.shape
    rows = WINDOW * SWA_STEP_BLOCKS
    kcol = A_Q // LANES
    vcol = kcol + 1

    def cur(col):
        return pl.BlockSpec((None, rows, LANES), lambda b, i: (b, i, col))

    def prev(col):
        return pl.BlockSpec((None, WINDOW, LANES), lambda b, i: (b, jnp.maximum(i * SWA_STEP_BLOCKS - 1, 0), col))

    return pl.pallas_call(
        _swa_kernel,
        grid=(bsz, t // rows),
        in_specs=[
            pl.BlockSpec(memory_space=pltpu.SMEM),
            pl.BlockSpec((None, rows, A_Q), lambda b, i: (b, i, 0)),
            cur(kcol), prev(kcol), cur(vcol), prev(vcol),
        ],
        out_specs=pl.BlockSpec((None, rows, A_Q), lambda b, i: (b, i, 0)),
        out_shape=jax.ShapeDtypeStruct((bsz, t, A_Q), BF16),
        compiler_params=_params(("parallel", "parallel")),
        name="swa_prompt",
    )(sinks, proj, proj, proj, proj, proj)


def _split3(x):
    hi = x.astype(BF16)
    r1 = x - hi.astype(F32)
    mid = r1.astype(BF16)
    lo = (r1 - mid.astype(F32)).astype(BF16)
    return hi, mid, lo


def _hgrn_kernel(q0_ref, q1_ref, lf0_ref, lf1_ref, k0_ref, k1_ref, v0_ref, v1_ref, g0_ref, g1_ref, ng_ref,
                 o_ref, s_ref, st_ref):
    i = pl.program_id(1)

    @pl.when(i == 0)
    def _():
        st_ref[...] = jnp.zeros_like(st_ref)

    n = HGRN_BLOCK
    nsub = n // HGRN_SUB
    heads = range(B_HEADS)
    qb = jnp.concatenate([q0_ref[...], q1_ref[...]], axis=1)
    lf = jnp.concatenate([lf0_ref[...], lf1_ref[...]], axis=1)
    lk = jnp.concatenate([k0_ref[...], k1_ref[...]], axis=1)
    v = jnp.concatenate([v0_ref[...], v1_ref[...]], axis=1)
    r = lax.broadcasted_iota(jnp.int32, (n, n), 0)
    c = lax.broadcasted_iota(jnp.int32, (n, n), 1)
    tri = jnp.where((c <= r) & (c >= r - (r & (HGRN_SUB - 1))), 1.0, 0.0).astype(BF16)
    lf_hi, lf_mid, lf_lo = _split3(lf)
    a_all = _dot(tri, lf_hi) + (_dot(tri, lf_mid) + _dot(tri, lf_lo))
    a3 = a_all.reshape(nsub, HGRN_SUB, B_HEADS * B_DIM)
    a_last3 = a3[:, HGRN_SUB - 1:HGRN_SUB, :]
    qb3 = qb.reshape(a3.shape)
    v3 = v.reshape(a3.shape)
    b3 = a3 - lk.reshape(a3.shape)
    qe3 = qb3 * jnp.exp2(a3)
    kd3 = jnp.exp2(a_last3 - b3)
    dec3 = jnp.exp2(a_last3)
    trow = lax.broadcasted_iota(jnp.int32, (1, HGRN_SUB, 1), 1)
    o_diag = []
    v_t = []
    for h in heads:
        hs = slice(h * B_DIM, (h + 1) * B_DIM)
        a_h, q_h, b_h, v_h = a3[:, :, hs], qb3[:, :, hs], b3[:, :, hs], v3[:, :, hs]
        acc = jnp.zeros((nsub, HGRN_SUB, B_DIM), F32)
        for s in range(HGRN_SUB):
            e = jnp.exp2(jnp.where(trow >= s, a_h - b_h[:, s:s + 1, :], -jnp.inf))
            col = jnp.sum(q_h * e, axis=-1, keepdims=True)
            acc = acc + col * v_h[:, s:s + 1, :]
        o_diag.append(acc.reshape(n, B_DIM))
        v_t.append(v[:, hs].T.astype(BF16))
    state = [st_ref[h] for h in heads]
    o_state = [[] for _ in heads]
    for ch in range(nsub):
        sl = slice(ch * HGRN_SUB, (ch + 1) * HGRN_SUB)
        for h in heads:
            hs = slice(h * B_DIM, (h + 1) * B_DIM)
            o_state[h].append(_dot_nt(qe3[ch, :, hs].astype(BF16), state[h].astype(BF16)))
            state[h] = state[h] * dec3[ch, :, hs] + _dot(v_t[h][:, sl], kd3[ch, :, hs].astype(BF16))
    g = jnp.concatenate([g0_ref[...], g1_ref[...]], axis=1)
    for h in heads:
        hs = slice(h * B_DIM, (h + 1) * B_DIM)
        st_ref[h] = state[h]
        o = jnp.concatenate(o_state[h], axis=0) + o_diag[h]
        y = o * lax.rsqrt(jnp.mean(o * o, axis=-1, keepdims=True) + NORM_EPS)
        o_ref[:, hs] = (y * ng_ref[...] * g[:, hs]).astype(o_ref.dtype)

    @pl.when(i == pl.num_programs(1) - 1)
    def _():
        for h in heads:
            s_ref[h] = state[h].T


def _hgrn_prompt(proj, norm_g):
    bsz, t, _ = proj.shape
    nh = B_HEADS
    width = nh * B_DIM
    half = width // 2
    base = (A_Q + 2 * A_KV) // half

    def col(off):
        return pl.BlockSpec((None, HGRN_BLOCK, half), lambda b, i: (b, i, base + off))

    return pl.pallas_call(
        _hgrn_kernel,
        grid=(bsz, t // HGRN_BLOCK),
        in_specs=[
            col(0), col(1), col(2), col(3), col(4), col(5), col(6), col(7), col(8), col(9),
            pl.BlockSpec((1, B_DIM), lambda b, i: (0, 0)),
        ],
        out_specs=[
            pl.BlockSpec((None, HGRN_BLOCK, width), lambda b, i: (b, i, 0)),
            pl.BlockSpec((None, nh, B_DIM, B_DIM), lambda b, i: (b, 0, 0, 0)),
        ],
        out_shape=[
            jax.ShapeDtypeStruct((bsz, t, width), BF16),
            jax.ShapeDtypeStruct((bsz, nh, B_DIM, B_DIM), F32),
        ],
        scratch_shapes=[pltpu.VMEM((nh, B_DIM, B_DIM), F32)],
        compiler_params=_params(("parallel", "arbitrary")),
        name="hgrn_prompt",
    )(*([proj] * 10), norm_g.reshape(1, B_DIM))


def _gdn_kernel(q_ref, k_ref, v_ref, gate_ref, ab_ref, alog_ref, dtb_ref, ng_ref, o_ref, s_out_ref, s_ref):
    i = pl.program_id(1)
    n = GDN_CHUNK

    @pl.when(i == 0)
    def _():
        s_ref[...] = jnp.zeros_like(s_ref)

    r = lax.broadcasted_iota(jnp.int32, (n, n), 0)
    c = lax.broadcasted_iota(jnp.int32, (n, n), 1)
    causal = c <= r
    strict = c < r
    tri = jnp.where(causal, 1.0, 0.0).astype(F32)
    eye = jnp.where(r == c, 1.0, 0.0).astype(F32)
    base_mask = (r >> GDN_BASE_LOG) == (c >> GDN_BASE_LOG)
    level_masks = [((r >> (k + 1)) == (c >> (k + 1))) & ((r >> k) != (c >> k))
                   for k in range(GDN_BASE_LOG, GDN_CHUNK_LOG)]
    chunks = range(GDN_STEP_CHUNKS)
    heads = range(C_HEADS)
    g_cum, g_cum_t, beta_all = [], [], []
    for ch in chunks:
        ab = ab_ref[ch * n:(ch + 1) * n, :]
        log_alpha = -jnp.exp(alog_ref[...]) * _softplus(ab + dtb_ref[...])
        beta_all.append(jax.nn.sigmoid(ab))
        g_cum.append(_dot(tri, log_alpha, HIGHEST))
        g_cum_t.append(g_cum[ch].T)

    items = [(ch, h) for ch in chunks for h in heads]
    pairs = range(len(items))

    def tile(ref, ch, h):
        return ref[ch * n:(ch + 1) * n, h * C_DIM:(h + 1) * C_DIM]

    qn = [tile(q_ref, ch, h) for ch, h in items]
    kn = [tile(k_ref, ch, h) for ch, h in items]
    vh = [tile(v_ref, ch, h) for ch, h in items]
    kn_b = [kn[j].astype(BF16) for j in pairs]
    g_col = [g_cum[ch][:, h:h + 1] for ch, h in items]
    beta = [beta_all[ch][:, C_HEADS + h:C_HEADS + h + 1] for ch, h in items]
    dec_incl = [jnp.exp(jnp.where(causal, g_col[j] - g_cum_t[ch][h:h + 1, :], -jnp.inf))
                for j, (ch, h) in enumerate(items)]
    e_g = [jnp.exp(g_col[j]) for j in pairs]
    kq_kt = [_dot_nt(jnp.concatenate([kn_b[j], qn[j].astype(BF16)], axis=0), kn_b[j]) for j in pairs]
    kkt = [kq_kt[j][:n] for j in pairs]
    qk = [kq_kt[j][n:] for j in pairs]
    l_mat = [beta[j] * kkt[j] * jnp.where(strict, dec_incl[j], 0.0) for j in pairs]
    a_pow = [jnp.where(base_mask, -l_mat[j], 0.0) for j in pairs]
    t_inv = [eye + a_pow[j] for j in pairs]
    for _ in range(GDN_BASE_LOG - 1):
        a_pow = [_dot_inv(a_pow[j], a_pow[j]) for j in pairs]
        t_inv = [t_inv[j] + _dot_inv(t_inv[j], a_pow[j]) for j in pairs]
    for lm in level_masks:
        x = [_dot_inv(jnp.where(lm, l_mat[j], 0.0), t_inv[j]) for j in pairs]
        t_inv = [t_inv[j] - _dot_inv(t_inv[j], x[j]) for j in pairs]
    uw = [_dot_inv(t_inv[j], jnp.concatenate([beta[j] * vh[j], (beta[j] * e_g[j]) * kn[j]], axis=1))
          for j in pairs]
    wq_b = [jnp.concatenate([uw[j][:, C_DIM:], qn[j] * e_g[j]], axis=0).astype(BF16) for j in pairs]
    qk_b = [(qk[j] * dec_incl[j]).astype(BF16) for j in pairs]
    g_last = [g_col[j][n - 1:n] for j in pairs]
    kd_t = [(kn[j] * jnp.exp(g_last[j] - g_col[j])).T.astype(BF16) for j in pairs]

    state = [s_ref[h] for h in heads]
    for ch in chunks:
        js = [ch * C_HEADS + h for h in heads]
        s_b = [state[h].astype(BF16) for h in heads]
        wq_s = [_dot(wq_b[js[h]], s_b[h]) for h in heads]
        delta_b = [(uw[js[h]][:, :C_DIM] - wq_s[h][:n]).astype(BF16) for h in heads]
        o = [wq_s[h][n:] + _dot(qk_b[js[h]], delta_b[h]) for h in heads]
        state = [jnp.exp(g_last[js[h]]) * state[h] + _dot(kd_t[js[h]], delta_b[h]) for h in heads]
        for h in heads:
            hs = slice(h * C_DIM, (h + 1) * C_DIM)
            y = o[h] * lax.rsqrt(jnp.mean(o[h] * o[h], axis=-1, keepdims=True) + NORM_EPS)
            o_ref[ch * n:(ch + 1) * n, hs] = (y * ng_ref[...] * tile(gate_ref, ch, h)).astype(o_ref.dtype)
    for h in heads:
        s_ref[h] = state[h]

    @pl.when(i == pl.num_programs(1) - 1)
    def _():
        s_out_ref[...] = s_ref[...]


def _gdn_prompt(proj, alog_row, dtb_row, norm_g):
    bsz, t, _ = proj.shape
    n = GDN_CHUNK * GDN_STEP_CHUNKS

    def col(cidx):
        return pl.BlockSpec((None, n, C_QK), lambda b, i: (b, i, cidx))

    one = pl.BlockSpec((1, LANES), lambda b, i: (0, 0))
    return pl.pallas_call(
        _gdn_kernel,
        grid=(bsz, t // n),
        in_specs=[
            col(0), col(1), col(2), col(3),
            pl.BlockSpec((None, n, LANES), lambda b, i: (b, i, (C_CONV_DIM + C_QK) // LANES)),
            one, one, one,
        ],
        out_specs=[
            pl.BlockSpec((None, n, C_QK), lambda b, i: (b, i, 0)),
            pl.BlockSpec((None, C_HEADS, C_DIM, C_DIM), lambda b, i: (b, 0, 0, 0)),
        ],
        out_shape=[
            jax.ShapeDtypeStruct((bsz, t, C_QK), BF16),
            jax.ShapeDtypeStruct((bsz, C_HEADS, C_DIM, C_DIM), F32),
        ],
        scratch_shapes=[pltpu.VMEM((C_HEADS, C_DIM, C_DIM), F32)],
        compiler_params=_params(("parallel", "arbitrary")),
        name="gdn_prompt",
    )(proj, proj, proj, proj, proj, alog_row, dtb_row, norm_g.reshape(1, C_DIM))


def _swa_dec_kernel(sink_ref, q_ref, kn_ref, vn_ref, ck_ref, cv_ref, o_ref, ok_ref, ov_ref):
    batch = range(q_ref.shape[0])
    row = lax.broadcasted_iota(jnp.int32, (WINDOW, LANES), 0)
    head = lax.broadcasted_iota(jnp.int32, (A_HEADS, WINDOW), 0)
    dist = (WINDOW - 1 - lax.broadcasted_iota(jnp.int32, (A_HEADS, WINDOW), 1)).astype(F32)
    scale = A_HEAD_DIM ** -0.5
    slope = jnp.zeros((A_HEADS, WINDOW), F32)
    sink = jnp.zeros((A_HEADS, 1), F32)
    for h in range(A_HEADS):
        slope = jnp.where(head == h, 2.0 ** (-8.0 * (h + 1) / A_HEADS), slope)
        sink = jnp.where(head[:, 0:1] == h, sink_ref[h], sink)
    bias = slope * dist
    first_kv = head[:, 0:A_HEAD_DIM] < A_GROUP
    keys = [jnp.where(row == WINDOW - 1, kn_ref[b], pltpu.roll(ck_ref[b], WINDOW - 1, 0)) for b in batch]
    vals = [jnp.where(row == WINDOW - 1, vn_ref[b], pltpu.roll(cv_ref[b], WINDOW - 1, 0)) for b in batch]
    for b in batch:
        ok_ref[b] = keys[b]
        ov_ref[b] = vals[b]
    q8 = [jnp.concatenate([jnp.where(first_kv, q_ref[b], 0.0), jnp.where(first_kv, 0.0, q_ref[b])],
                          axis=1).astype(BF16) for b in batch]
    s = [_dot_nt(q8[b], keys[b].astype(BF16)) * scale - bias for b in batch]
    m = [jnp.maximum(jnp.max(s[b], axis=-1, keepdims=True), sink) for b in batch]
    p = [jnp.exp(s[b] - m[b]) for b in batch]
    den = [jnp.sum(p[b], axis=-1, keepdims=True) + jnp.exp(sink - m[b]) for b in batch]
    o = [_dot(p[b].astype(BF16), vals[b].astype(BF16)) for b in batch]
    for b in batch:
        o_ref[b] = jnp.where(first_kv, o[b][:, :A_HEAD_DIM], o[b][:, A_HEAD_DIM:]) / den[b]


def _swa_decode(proj2, cache_k, cache_v, sinks):
    nbatch = proj2.shape[0]
    bb = 8
    q3 = proj2[:, :A_Q].reshape(nbatch, A_HEADS, A_HEAD_DIM)
    kn = proj2[:, A_Q:A_Q + A_KV].reshape(nbatch, 1, A_KV)
    vn = proj2[:, A_Q + A_KV:A_Q + 2 * A_KV].reshape(nbatch, 1, A_KV)
    ck = cache_k.reshape(nbatch, WINDOW, A_KV)
    cv = cache_v.reshape(nbatch, WINDOW, A_KV)
    qspec = pl.BlockSpec((bb, A_HEADS, A_HEAD_DIM), lambda j: (j, 0, 0))
    nspec = pl.BlockSpec((bb, 1, A_KV), lambda j: (j, 0, 0))
    cspec = pl.BlockSpec((bb, WINDOW, A_KV), lambda j: (j, 0, 0))
    o, nk, nv = pl.pallas_call(
        _swa_dec_kernel,
        grid=(nbatch // bb,),
        in_specs=[pl.BlockSpec(memory_space=pltpu.SMEM), qspec, nspec, nspec, cspec, cspec],
        out_specs=[qspec, cspec, cspec],
        out_shape=[
            jax.ShapeDtypeStruct((nbatch, A_HEADS, A_HEAD_DIM), F32),
            jax.ShapeDtypeStruct((nbatch, WINDOW, A_KV), F32),
            jax.ShapeDtypeStruct((nbatch, WINDOW, A_KV), F32),
        ],
        compiler_params=_params(("parallel",)),
        name="swa_decode",
    )(sinks, q3, kn, vn, ck, cv)
    shape5 = (nbatch, WINDOW, A_KV_HEADS, A_HEAD_DIM)
    return o.reshape(nbatch, A_Q), nk.reshape(shape5), nv.reshape(shape5)


def _cols(rows, heads):
    nbatch = rows.shape[0]
    x = rows.reshape(nbatch // DEC_BLOCK, DEC_BLOCK, heads, LANES)
    return x.transpose(2, 0, 3, 1)


def _hgrn_dec_kernel(qt_ref, ft_ref, lb_ref, v_ref, g_ref, ng_ref, s_ref, o_ref, so_ref, orow_ref):
    lb = lb_ref[...]
    qc = _silu(qt_ref[...])
    f = lb + (1.0 - lb) * jax.nn.sigmoid(ft_ref[...])
    kk = 1.0 - f
    v = v_ref[...]
    for bb in range(DEC_BLOCK):
        s_new = f[:, bb:bb + 1] * s_ref[bb] + kk[:, bb:bb + 1] * v[bb:bb + 1, :]
        so_ref[bb] = s_new
        orow_ref[bb:bb + 1, :] = jnp.sum(qc[:, bb:bb + 1] * s_new, axis=0, keepdims=True)
    o_ref[...] = _rms_gate(orow_ref[...], ng_ref[...], g_ref[...])


def _hgrn_decode(proj2, state, lower, norm_g):
    nbatch = proj2.shape[0]
    nh = B_HEADS
    base = A_Q + 2 * A_KV
    width = nh * B_DIM
    qt = _cols(proj2[:, base:base + width], nh)
    ft = _cols(proj2[:, base + width:base + 2 * width], nh)
    colspec = pl.BlockSpec((None, None, B_DIM, DEC_BLOCK), lambda h, j: (h, j, 0, 0))
    sspec = pl.BlockSpec((DEC_BLOCK, None, B_DIM, B_DIM), lambda h, j: (j, h, 0, 0))
    cb = base // LANES
    o, s_new = pl.pallas_call(
        _hgrn_dec_kernel,
        grid=(nh, nbatch // DEC_BLOCK),
        in_specs=[
            colspec, colspec,
            pl.BlockSpec((None, B_DIM, 1), lambda h, j: (h, 0, 0)),
            pl.BlockSpec((DEC_BLOCK, B_DIM), lambda h, j: (j, cb + 2 * nh + h)),
            pl.BlockSpec((DEC_BLOCK, B_DIM), lambda h, j: (j, cb + 3 * nh + h)),
            pl.BlockSpec((1, B_DIM), lambda h, j: (0, 0)),
            sspec,
        ],
        out_specs=[pl.BlockSpec((DEC_BLOCK, B_DIM), lambda h, j: (j, h)), sspec],
        out_shape=[
            jax.ShapeDtypeStruct((nbatch, width), F32),
            jax.ShapeDtypeStruct(state.shape, F32),
        ],
        scratch_shapes=[pltpu.VMEM((DEC_BLOCK, B_DIM), F32)],
        compiler_params=_params(("parallel", "parallel")),
        name="hgrn_decode",
    )(qt, ft, lower.reshape(nh, B_DIM, 1), proj2, proj2, norm_g.reshape(1, B_DIM), state)
    return o, s_new


def _gdn_prep_kernel(p_ref, h0_ref, h1_ref, h2_ref, cw_ref, alog_ref, dtb_ref, q_ref, k_ref, v_ref, ab_ref):
    acc = (h0_ref[...] * cw_ref[0:1, :] + h1_ref[...] * cw_ref[1:2, :] + h2_ref[...] * cw_ref[2:3, :]
           + p_ref[:, 0:C_CONV_DIM] * cw_ref[3:4, :])
    qkv = _silu(acc)
    for h in range(C_HEADS):
        hs = slice(h * C_DIM, (h + 1) * C_DIM)
        qh = qkv[:, h * C_DIM:(h + 1) * C_DIM]
        kh = qkv[:, C_QK + h * C_DIM:C_QK + (h + 1) * C_DIM]
        q_ref[:, hs] = qh * lax.rsqrt(jnp.sum(qh * qh, axis=-1, keepdims=True) + NORM_EPS) * (C_DIM ** -0.5)
        k_ref[:, hs] = kh * lax.rsqrt(jnp.sum(kh * kh, axis=-1, keepdims=True) + NORM_EPS)
    v_ref[...] = qkv[:, 2 * C_QK:3 * C_QK]
    ab = p_ref[:, C_CONV_DIM + C_QK:C_CONV_DIM + C_QK + LANES]
    alpha = jnp.exp(-jnp.exp(alog_ref[...]) * _softplus(ab + dtb_ref[...]))
    beta = jax.nn.sigmoid(ab)
    lane = lax.broadcasted_iota(jnp.int32, ab.shape, 1)
    ab_ref[...] = jnp.where(lane < C_HEADS, alpha, beta)


def _gdn_dec_kernel(qt_ref, kt_ref, v_ref, gate_ref, al_ref, be_ref, ng_ref, s_ref, o_ref, so_ref, orow_ref):
    qt = qt_ref[...]
    kt = kt_ref[...]
    v = v_ref[...]
    al = al_ref[...]
    be = be_ref[...]
    for bb in range(DEC_BLOCK):
        s = s_ref[bb]
        kc = kt[:, bb:bb + 1]
        a = al[bb:bb + 1, :]
        ks = jnp.sum(kc * s, axis=0, keepdims=True)
        delta = be[bb:bb + 1, :] * (v[bb:bb + 1, :] - a * ks)
        s_new = a * s + kc * delta
        so_ref[bb] = s_new
        orow_ref[bb:bb + 1, :] = jnp.sum(qt[:, bb:bb + 1] * s_new, axis=0, keepdims=True)
    o_ref[...] = _rms_gate(orow_ref[...], ng_ref[...], gate_ref[...])


def _gdn_decode(proj2, hist, state, conv_w, alog_row, dtb_row, norm_g):
    nbatch = proj2.shape[0]
    nh = C_HEADS
    full = lambda shape: pl.BlockSpec(shape, lambda: tuple(0 for _ in shape))
    hspec = full((nbatch, C_CONV_DIM))
    qn, kn, vc, ab = pl.pallas_call(
        _gdn_prep_kernel,
        grid=(),
        in_specs=[
            full((nbatch, ODD_IN_PAD)), hspec, hspec, hspec,
            full((C_CONV, C_CONV_DIM)), full((1, LANES)), full((1, LANES)),
        ],
        out_specs=[full((nbatch, C_QK)), full((nbatch, C_QK)), full((nbatch, C_QK)), full((nbatch, LANES))],
        out_shape=[
            jax.ShapeDtypeStruct((nbatch, C_QK), F32),
            jax.ShapeDtypeStruct((nbatch, C_QK), F32),
            jax.ShapeDtypeStruct((nbatch, C_QK), F32),
            jax.ShapeDtypeStruct((nbatch, LANES), F32),
        ],
        compiler_params=pltpu.CompilerParams(vmem_limit_bytes=VMEM_LIMIT),
        name="gdn_decode_prep",
    )(proj2, hist[:, 0], hist[:, 1], hist[:, 2], conv_w, alog_row, dtb_row)
    qt = _cols(qn, nh)
    kt = _cols(kn, nh)
    al = jnp.broadcast_to(ab[:, :nh].T[:, :, None], (nh, nbatch, LANES))
    be = jnp.broadcast_to(ab[:, nh:2 * nh].T[:, :, None], (nh, nbatch, LANES))
    colspec = pl.BlockSpec((None, None, C_DIM, DEC_BLOCK), lambda h, j: (h, j, 0, 0))
    sspec = pl.BlockSpec((DEC_BLOCK, None, C_DIM, C_DIM), lambda h, j: (j, h, 0, 0))
    rspec = pl.BlockSpec((None, DEC_BLOCK, LANES), lambda h, j: (h, j, 0))
    gcol = C_CONV_DIM // LANES
    o, s_new = pl.pallas_call(
        _gdn_dec_kernel,
        grid=(nh, nbatch // DEC_BLOCK),
        in_specs=[
            colspec, colspec,
            pl.BlockSpec((DEC_BLOCK, C_DIM), lambda h, j: (j, h)),
            pl.BlockSpec((DEC_BLOCK, C_DIM), lambda h, j: (j, gcol + h)),
            rspec, rspec,
            pl.BlockSpec((1, C_DIM), lambda h, j: (0, 0)),
            sspec,
        ],
        out_specs=[pl.BlockSpec((DEC_BLOCK, C_DIM), lambda h, j: (j, h)), sspec],
        out_shape=[
            jax.ShapeDtypeStruct((nbatch, C_QK), F32),
            jax.ShapeDtypeStruct(state.shape, F32),
        ],
        scratch_shapes=[pltpu.VMEM((DEC_BLOCK, C_DIM), F32)],
        compiler_params=_params(("parallel", "parallel")),
        name="gdn_decode",
    )(qt, kt, vc, proj2, al, be, norm_g.reshape(1, C_DIM), state)
    return o, s_new


def _trunk(x, mods, caches, p, tm):
    decode = caches is not None
    m = mods[0]
    ffn = functools.partial(_ffn, w_up=p["w_up"], w_down=p["w_down"], ln_g=p["ln_g"], ln_b=p["ln_b"], tm=tm)
    x = ffn(x, m, 0, layer=0, sub=0)
    if decode:
        proj = _inproj(x, m, 3, p["even_in"], tm)
        o_a, new_k, new_v = _swa_decode(proj[0], caches[0][0], caches[1][0], p["sinks"])
        o_b, s_hgrn = _hgrn_decode(proj[0], caches[2][0], p["lower"], p["hgrn_norm_g"])
        o_a, o_b = o_a[None], o_b[None]
    else:
        bsz = x.shape[0]
        proj = _inproj_even(x, m, 3, p["even_in"], p["lower"], tm)
        o_a = _swa_prompt(proj, p["sinks"])
        new_k = proj[:, -WINDOW:, A_Q:A_Q + A_KV].reshape(bsz, WINDOW, A_KV_HEADS, A_HEAD_DIM)
        new_v = proj[:, -WINDOW:, A_Q + A_KV:A_Q + 2 * A_KV].reshape(bsz, WINDOW, A_KV_HEADS, A_HEAD_DIM)
        o_b, s_hgrn = _hgrn_prompt(proj, p["hgrn_norm_g"])
    x = ffn(x, m, 6, layer=0, sub=1, mixer=(o_a, 0, o_b, 0, p["even_out"], 5))
    m = mods[1]
    x = ffn(x, m, 0, layer=1, sub=0)
    if decode:
        proj = _inproj(x, m, 3, p["odd_in"], tm)
        hist = caches[4][0]
        o_c, s_gdn = _gdn_decode(proj[0], hist, caches[3][0], p["conv_w"], p["alog_row"], p["dtb_row"],
                                 p["gdn_norm_g"])
        o_c = o_c[None]
        new_hist = jnp.concatenate([hist[:, 1:], proj[0][:, None, :C_CONV_DIM]], axis=1)
    else:
        proj, last_rows = _inproj_gdn(x, m, 3, p["odd_in"], p["conv_w"], tm)
        o_c, s_gdn = _gdn_prompt(proj, p["alog_row"], p["dtb_row"], p["gdn_norm_g"])
        new_hist = last_rows[:, -(C_CONV - 1):]
    x = ffn(x, m, 6, layer=1, sub=1, mixer=(o_c, 0, o_c, 1, p["odd_out"], 5))
    return x, new_k[None], new_v[None], s_hgrn[None], s_gdn[None], new_hist[None]


def kernel(x_prompt, x_sample, cache_swa_k, cache_swa_v, state_hgrn, state_gdn, state_gdn_conv, c_prompt, c_sample, ada_w, ada_b, ln_g, ln_b, ffn_w_up, ffn_w_down, even_w_in, even_w_out, swa_sinks, hgrn_norm_g, hgrn_lb_logits, odd_w_in, odd_w_out, gdn_conv_w, gdn_a_log, gdn_dt_bias, gdn_norm_g):
    n_prompt = c_prompt.shape[0]
    n_sample = c_sample.shape[0]
    pad_rows = (-(n_prompt + n_sample)) % 8
    c_all = jnp.concatenate([c_prompt, c_sample, jnp.zeros((pad_rows, D_MODEL), F32)], axis=0)
    mods = _ada_mods(c_all, ada_w, ada_b)
    mods_p = mods[:, :n_prompt].reshape(DEPTH, n_prompt, 1, N_MOD * D_MODEL)
    mods_s = mods[:, n_prompt:n_prompt + n_sample].reshape(DEPTH, 1, n_sample, N_MOD * D_MODEL)

    probs = jax.nn.softmax(hgrn_lb_logits.astype(F32), axis=0)
    lower = (jnp.cumsum(probs, axis=0)[1:] - probs[0])[0]

    def lane_row(v):
        return jnp.pad(v.astype(F32), (0, LANES - v.shape[0])).reshape(1, LANES)

    p = dict(
        w_up=ffn_w_up.astype(BF16), w_down=ffn_w_down.astype(BF16),
        ln_g=ln_g, ln_b=ln_b,
        even_in=even_w_in[0].astype(BF16), even_out=even_w_out[0].astype(BF16),
        odd_in=jnp.concatenate([odd_w_in[0].astype(BF16), jnp.zeros((D_MODEL, ODD_IN_PAD - ODD_IN), BF16)], axis=1),
        odd_out=odd_w_out[0].astype(BF16),
        sinks=swa_sinks[0], lower=lower, hgrn_norm_g=hgrn_norm_g[0],
        conv_w=gdn_conv_w[0], alog_row=lane_row(gdn_a_log[0]), dtb_row=lane_row(gdn_dt_bias[0]),
        gdn_norm_g=gdn_norm_g[0],
    )
    y_p, p_k, p_v, p_hgrn, p_gdn, p_conv = _trunk(x_prompt, mods_p, None, p, 512)
    caches = (cache_swa_k, cache_swa_v, state_hgrn, state_gdn, state_gdn_conv)
    x_s = x_sample.reshape(1, n_sample, D_MODEL)
    y_s, s_k, s_v, s_hgrn, s_gdn, s_conv = _trunk(x_s, mods_s, caches, p, n_sample)
    y_s = y_s.reshape(n_sample, 1, D_MODEL)
    return (y_p, y_s, p_k, p_v, p_hgrn, p_gdn, p_conv, s_k, s_v, s_hgrn, s_gdn, s_conv)
```

```python
import functools

import jax
import jax.numpy as jnp
from jax import lax
from jax.experimental import pallas as pl
from jax.experimental.pallas import tpu as pltpu

F32 = jnp.float32
BF16 = jnp.bfloat16

D_MODEL = 1024
DEPTH = 2
WINDOW = 128
A_HEADS = 8
A_KV_HEADS = 2
A_GROUP = A_HEADS // A_KV_HEADS
A_HEAD_DIM = 64
A_Q = A_HEADS * A_HEAD_DIM
A_KV = A_KV_HEADS * A_HEAD_DIM
B_HEADS = 4
B_DIM = 128
EVEN_IN = A_Q + 2 * A_KV + 4 * B_HEADS * B_DIM
C_HEADS = 8
C_DIM = 128
C_QK = C_HEADS * C_DIM
C_CONV = 4
C_CONV_DIM = 3 * C_QK
ODD_IN = C_CONV_DIM + C_QK + 2 * C_HEADS
D_FF = 2816
N_MOD = 9
DN_ALPHA = (2 * DEPTH) ** 0.25
LN_EPS = 1e-5
NORM_EPS = 1e-6
LANES = 128

FF_CHUNK = 256
FFN_TM = 1024
FFN_VMEM_LIMIT = 56 << 20
PROJ_TILE = EVEN_IN // 2
ODD_IN_PAD = -(-ODD_IN // PROJ_TILE) * PROJ_TILE
SWA_STEP_BLOCKS = 2
HGRN_BLOCK = 256
HGRN_SUB = 8
GDN_CHUNK_LOG = 7
GDN_CHUNK = 1 << GDN_CHUNK_LOG
GDN_BASE_LOG = 3
GDN_STEP_CHUNKS = 2
DEC_BLOCK = 64
VMEM_LIMIT = 48 << 20


def _params(sem, vmem=VMEM_LIMIT):
    return pltpu.CompilerParams(dimension_semantics=sem, vmem_limit_bytes=vmem)


def _silu(x):
    return x * jax.nn.sigmoid(x)


def _softplus(x):
    return jnp.maximum(x, 0.0) + jnp.log(1.0 + jnp.exp(-jnp.abs(x)))


def _layer_norm(y, g, b):
    mu = jnp.mean(y, axis=-1, keepdims=True)
    d = y - mu
    var = jnp.mean(d * d, axis=-1, keepdims=True)
    return d * lax.rsqrt(var + LN_EPS) * g + b


def _rms_gate(o, norm_g, gate):
    y = o * lax.rsqrt(jnp.mean(o * o, axis=-1, keepdims=True) + NORM_EPS)
    return y * norm_g * _silu(gate)


def _dot(a, b, precision=None):
    return jnp.dot(a, b, preferred_element_type=F32, precision=precision)


def _dot_inv(a, b):
    return _dot(a.astype(BF16), b.astype(BF16))


def _dot_nt(a, b, precision=None):
    return lax.dot_general(a, b, (((1,), (1,)), ((), ())), preferred_element_type=F32, precision=precision)


def _ada_kernel(c_ref, w_ref, b_ref, o_ref):
    cs = _silu(c_ref[...]).astype(BF16)
    o_ref[...] = _dot(cs, w_ref[...].astype(BF16)) + b_ref[...]


def _ada_mods(c_all, ada_w, ada_b):
    m = c_all.shape[0]
    n = N_MOD * D_MODEL
    tn = 1152
    return pl.pallas_call(
        _ada_kernel,
        grid=(DEPTH, n // tn),
        in_specs=[
            pl.BlockSpec((m, D_MODEL), lambda l, j: (0, 0)),
            pl.BlockSpec((None, D_MODEL, tn), lambda l, j: (l, 0, j)),
            pl.BlockSpec((None, 1, tn), lambda l, j: (l, 0, j)),
        ],
        out_specs=pl.BlockSpec((None, m, tn), lambda l, j: (l, 0, j)),
        out_shape=jax.ShapeDtypeStruct((DEPTH, m, n), F32),
        compiler_params=_params(("parallel", "parallel")),
        name="ada_mods",
    )(c_all, ada_w, ada_b.reshape(DEPTH, 1, n))


def _mod_spec(mods, k, tm, grid_rank):
    per_token = mods.shape[1] != 1
    rows = tm if per_token else 1
    if grid_rank == 3:
        return pl.BlockSpec((None, rows, D_MODEL), lambda b, i, j: (b, i if per_token else 0, k))
    return pl.BlockSpec((None, rows, D_MODEL), lambda b, i: (b, i if per_token else 0, k))


def _ffn_body(x, sh_ref, sc_ref, g_ref, wu_ref, wd_ref, lg_ref, lb_ref, o_ref):
    h = (x * (1.0 + sc_ref[...]) + sh_ref[...]).astype(BF16)
    for c in range(D_FF // FF_CHUNK):
        lo = c * FF_CHUNK
        gate = _dot(h, wu_ref[:, lo:lo + FF_CHUNK])
        up = _dot(h, wu_ref[:, D_FF + lo:D_FF + lo + FF_CHUNK])
        act = (_silu(gate) * up).astype(BF16)
        part = _dot(act, wd_ref[lo:lo + FF_CHUNK, :])
        if c == 0:
            o_ref[...] = part
        else:
            o_ref[...] += part
    y = DN_ALPHA * x + (0.5 * g_ref[...]) * o_ref[...]
    o_ref[...] = _layer_norm(y, lg_ref[...], lb_ref[...])


def _ffn_kernel(x_ref, sh_ref, sc_ref, g_ref, wu_ref, wd_ref, lg_ref, lb_ref, o_ref):
    _ffn_body(x_ref[...], sh_ref, sc_ref, g_ref, wu_ref, wd_ref, lg_ref, lb_ref, o_ref)


def _mix_ffn_kernel(x_ref, o1_ref, o2_ref, gm_ref, w1_ref, w2_ref, lgm_ref, lbm_ref,
                    sh_ref, sc_ref, g_ref, wu_ref, wd_ref, lg_ref, lb_ref, o_ref):
    mix = _dot(o1_ref[...].astype(BF16), w1_ref[...]) + _dot(o2_ref[...].astype(BF16), w2_ref[...])
    x1 = _layer_norm(DN_ALPHA * x_ref[...] + gm_ref[...] * mix, lgm_ref[...], lbm_ref[...])
    _ffn_body(x1, sh_ref, sc_ref, g_ref, wu_ref, wd_ref, lg_ref, lb_ref, o_ref)


def _ffn(x, mods, k0, w_up, w_down, ln_g, ln_b, layer, sub, tm, mixer=None):
    bsz, t, _ = x.shape
    tm = min(t, max(tm, FFN_TM))
    half = D_MODEL // 2
    row = pl.BlockSpec((None, tm, D_MODEL), lambda b, i: (b, i, 0))
    vec = pl.BlockSpec((1, D_MODEL), lambda b, i: (0, 0))
    resident = pl.Buffered(1)
    ffn_specs = [
        _mod_spec(mods, k0, tm, 2), _mod_spec(mods, k0 + 1, tm, 2), _mod_spec(mods, k0 + 2, tm, 2),
        pl.BlockSpec((None, None, D_MODEL, 2 * D_FF), lambda b, i: (layer, sub, 0, 0), pipeline_mode=resident),
        pl.BlockSpec((None, None, D_FF, D_MODEL), lambda b, i: (layer, sub, 0, 0), pipeline_mode=resident),
        vec, vec,
    ]
    ffn_args = (mods, mods, mods, w_up, w_down,
                ln_g[layer, 2 * sub].reshape(1, D_MODEL), ln_b[layer, 2 * sub].reshape(1, D_MODEL))
    if mixer is None:
        body, specs, args = _ffn_kernel, [row] + ffn_specs, (x,) + ffn_args
    else:
        o1, c1, o2, c2, w_out, gate_k = mixer
        body = _mix_ffn_kernel
        specs = [
            row,
            pl.BlockSpec((None, tm, half), lambda b, i: (b, i, c1)),
            pl.BlockSpec((None, tm, half), lambda b, i: (b, i, c2)),
            _mod_spec(mods, gate_k, tm, 2),
            pl.BlockSpec((half, D_MODEL), lambda b, i: (0, 0), pipeline_mode=resident),
            pl.BlockSpec((half, D_MODEL), lambda b, i: (1, 0), pipeline_mode=resident),
            vec, vec,
        ] + ffn_specs
        args = (x, o1, o2, mods, w_out, w_out,
                ln_g[layer, 1].reshape(1, D_MODEL), ln_b[layer, 1].reshape(1, D_MODEL)) + ffn_args
    return pl.pallas_call(
        body,
        grid=(bsz, t // tm),
        in_specs=specs,
        out_specs=row,
        out_shape=jax.ShapeDtypeStruct(x.shape, F32),
        compiler_params=_params(("parallel", "parallel"), FFN_VMEM_LIMIT),
        name="ffn" if mixer is None else "mix_ffn",
    )(*args)


def _inproj_kernel(x_ref, sh_ref, sc_ref, w_ref, o_ref):
    h = (x_ref[...] * (1.0 + sc_ref[...]) + sh_ref[...]).astype(BF16)
    for lo in range(0, w_ref.shape[1], PROJ_TILE):
        o_ref[:, lo:lo + PROJ_TILE] = _dot(h, w_ref[:, lo:lo + PROJ_TILE])


def _inproj(x, mods, k0, w, tm):
    bsz, t, _ = x.shape
    n = w.shape[1]
    return pl.pallas_call(
        _inproj_kernel,
        grid=(bsz, t // tm),
        in_specs=[
            pl.BlockSpec((None, tm, D_MODEL), lambda b, i: (b, i, 0)),
            _mod_spec(mods, k0, tm, 2), _mod_spec(mods, k0 + 1, tm, 2),
            pl.BlockSpec((D_MODEL, n), lambda b, i: (0, 0), pipeline_mode=pl.Buffered(1)),
        ],
        out_specs=pl.BlockSpec((None, tm, n), lambda b, i: (b, i, 0)),
        out_shape=jax.ShapeDtypeStruct((bsz, t, n), F32),
        compiler_params=_params(("parallel", "parallel")),
        name="inproj",
    )(x, mods, mods, w)


def _inproj_even_kernel(x_ref, sh_ref, sc_ref, w_ref, lb_ref, o_ref):
    h = (x_ref[...] * (1.0 + sc_ref[...]) + sh_ref[...]).astype(BF16)
    attn = A_Q + 2 * A_KV
    width = B_HEADS * B_DIM

    def proj(group):
        return _dot(h, w_ref[:, attn + group * width:attn + (group + 1) * width])

    o_ref[:, 0:attn] = _dot(h, w_ref[:, 0:attn])
    o_ref[:, attn:attn + width] = _silu(proj(0))
    lb = lb_ref[...]
    f = lb + (1.0 - lb) * jax.nn.sigmoid(proj(1))
    o_ref[:, attn + width:attn + 2 * width] = jnp.log2(f)
    o_ref[:, attn + 2 * width:attn + 3 * width] = jnp.log2(1.0 - f)
    o_ref[:, attn + 3 * width:attn + 4 * width] = proj(2)
    o_ref[:, attn + 4 * width:attn + 5 * width] = _silu(proj(3))


def _inproj_even(x, mods, k0, w, lower, tm):
    bsz, t, _ = x.shape
    n_in = w.shape[1]
    width = B_HEADS * B_DIM
    n_out = n_in + width
    return pl.pallas_call(
        _inproj_even_kernel,
        grid=(bsz, t // tm),
        in_specs=[
            pl.BlockSpec((None, tm, D_MODEL), lambda b, i: (b, i, 0)),
            _mod_spec(mods, k0, tm, 2), _mod_spec(mods, k0 + 1, tm, 2),
            pl.BlockSpec((D_MODEL, n_in), lambda b, i: (0, 0), pipeline_mode=pl.Buffered(1)),
            pl.BlockSpec((1, width), lambda b, i: (0, 0)),
        ],
        out_specs=pl.BlockSpec((None, tm, n_out), lambda b, i: (b, i, 0)),
        out_shape=jax.ShapeDtypeStruct((bsz, t, n_out), F32),
        compiler_params=_params(("parallel", "parallel")),
        name="inproj_even",
    )(x, mods, mods, w, lower.reshape(1, width))


def _inproj_gdn_kernel(x_ref, sh_ref, sc_ref, w_ref, cw_ref, o_ref, hist_ref, carry_ref):
    assert C_CONV == 4
    i = pl.program_id(1)
    tm = x_ref.shape[0]

    @pl.when(i == 0)
    def _():
        carry_ref[...] = jnp.zeros_like(carry_ref)

    h = (x_ref[...] * (1.0 + sc_ref[...]) + sh_ref[...]).astype(BF16)
    for sec in range(3):
        cs = slice(sec * C_QK, (sec + 1) * C_QK)
        raw = _dot(h, w_ref[:, cs])
        ext = jnp.concatenate([carry_ref[:, cs], raw], axis=0)
        prev = pltpu.roll(ext, 1, 0)
        near = ext * cw_ref[3:4, cs] + prev * cw_ref[2:3, cs]
        far = ext * cw_ref[1:2, cs] + prev * cw_ref[0:1, cs]
        z = near + pltpu.roll(far, 2, 0)
        act = _silu(z[8:])
        if sec == 2:
            o_ref[:, cs] = act
        else:
            for hd in range(C_HEADS):
                a = act[:, hd * C_DIM:(hd + 1) * C_DIM]
                inv = lax.rsqrt(jnp.sum(a * a, axis=-1, keepdims=True) + NORM_EPS)
                if sec == 0:
                    inv = inv * (C_DIM ** -0.5)
                o_ref[:, sec * C_QK + hd * C_DIM:sec * C_QK + (hd + 1) * C_DIM] = a * inv
        carry_ref[:, cs] = raw[tm - 8:tm]
        hist_ref[:, cs] = raw[tm - 8:tm]
    o_ref[:, C_CONV_DIM:] = _dot(h, w_ref[:, C_CONV_DIM:])


def _inproj_gdn(x, mods, k0, w, conv_w, tm):
    bsz, t, _ = x.shape
    n = w.shape[1]
    return pl.pallas_call(
        _inproj_gdn_kernel,
        grid=(bsz, t // tm),
        in_specs=[
            pl.BlockSpec((None, tm, D_MODEL), lambda b, i: (b, i, 0)),
            _mod_spec(mods, k0, tm, 2), _mod_spec(mods, k0 + 1, tm, 2),
            pl.BlockSpec((D_MODEL, n), lambda b, i: (0, 0), pipeline_mode=pl.Buffered(1)),
            pl.BlockSpec((C_CONV, C_CONV_DIM), lambda b, i: (0, 0)),
        ],
        out_specs=[
            pl.BlockSpec((None, tm, n), lambda b, i: (b, i, 0)),
            pl.BlockSpec((None, 8, C_CONV_DIM), lambda b, i: (b, 0, 0)),
        ],
        out_shape=[
            jax.ShapeDtypeStruct((bsz, t, n), F32),
            jax.ShapeDtypeStruct((bsz, 8, C_CONV_DIM), F32),
        ],
        scratch_shapes=[pltpu.VMEM((8, C_CONV_DIM), F32)],
        compiler_params=_params(("parallel", "arbitrary")),
        name="inproj_gdn",
    )(x, mods, mods, w, conv_w)


def _swa_kernel(sink_ref, q_ref, kc_ref, kp_ref, vc_ref, vp_ref, o_ref):
    i = pl.program_id(1)
    k_all = jnp.concatenate([kp_ref[...], kc_ref[...]], axis=0).astype(BF16)
    v_all = jnp.concatenate([vp_ref[...], vc_ref[...]], axis=0).astype(BF16)
    r = lax.broadcasted_iota(jnp.int32, (WINDOW, 2 * WINDOW), 0)
    c = lax.broadcasted_iota(jnp.int32, (WINDOW, 2 * WINDOW), 1)
    dist = r + WINDOW - c
    in_window = (dist >= 0) & (dist < WINDOW)
    valid = [in_window & (c >= jnp.where(i > 0, 0, WINDOW)) if b == 0 else in_window
             for b in range(SWA_STEP_BLOCKS)]
    dist = dist.astype(F32)
    scale = A_HEAD_DIM ** -0.5
    ksl = [slice((h // A_GROUP) * A_HEAD_DIM, (h // A_GROUP + 1) * A_HEAD_DIM) for h in range(A_HEADS)]
    slope = [2.0 ** (-8.0 * (h + 1) / A_HEADS) for h in range(A_HEADS)]
    ones = jnp.ones((2 * WINDOW, A_KV), BF16)
    items = [(b, h) for b in range(SWA_STEP_BLOCKS) for h in range(A_HEADS)]
    pairs = range(len(items))
    qh = [q_ref[b * WINDOW:(b + 1) * WINDOW, h * A_HEAD_DIM:(h + 1) * A_HEAD_DIM].astype(BF16) for b, h in items]
    s = [_dot_nt(qh[j], k_all[b * WINDOW:(b + 2) * WINDOW, ksl[h]]) for j, (b, h) in enumerate(items)]
    s = [jnp.where(valid[b], s[j] * scale - slope[h] * dist, -jnp.inf) for j, (b, h) in enumerate(items)]
    m = [jnp.maximum(jnp.max(s[j], axis=-1, keepdims=True), sink_ref[h]) for j, (b, h) in enumerate(items)]
    p = [jnp.exp(s[j] - m[j]).astype(BF16) for j in pairs]
    o = [_dot(p[j], v_all[b * WINDOW:(b + 2) * WINDOW]) for j, (b, h) in enumerate(items)]
    den = [_dot(p[j], ones) + jnp.exp(sink_ref[h] - m[j]) for j, (b, h) in enumerate(items)]
    for b in range(SWA_STEP_BLOCKS):
        o_ref[b * WINDOW:(b + 1) * WINDOW, :] = jnp.concatenate(
            [(o[j] / den[j])[:, ksl[h]] for j, (bb, h) in enumerate(items) if bb == b], axis=1).astype(o_ref.dtype)


def _swa_prompt(proj, sinks):
    bsz, t, _ = proj.shape
    rows = WINDOW * SWA_STEP_BLOCKS
    kcol = A_Q // LANES
    vcol = kcol + 1

    def cur(col):
        return pl.BlockSpec((None, rows, LANES), lambda b, i: (b, i, col))

    def prev(col):
        return pl.BlockSpec((None, WINDOW, LANES), lambda b, i: (b, jnp.maximum(i * SWA_STEP_BLOCKS - 1, 0), col))

    return pl.pallas_call(
        _swa_kernel,
        grid=(bsz, t // rows),
        in_specs=[
            pl.BlockSpec(memory_space=pltpu.SMEM),
            pl.BlockSpec((None, rows, A_Q), lambda b, i: (b, i, 0)),
            cur(kcol), prev(kcol), cur(vcol), prev(vcol),
        ],
        out_specs=pl.BlockSpec((None, rows, A_Q), lambda b, i: (b, i, 0)),
        out_shape=jax.ShapeDtypeStruct((bsz, t, A_Q), BF16),
        compiler_params=_params(("parallel", "parallel")),
        name="swa_prompt",
    )(sinks, proj, proj, proj, proj, proj)


def _split3(x):
    hi = x.astype(BF16)
    r1 = x - hi.astype(F32)
    mid = r1.astype(BF16)
    lo = (r1 - mid.astype(F32)).astype(BF16)
    return hi, mid, lo


def _hgrn_kernel(q0_ref, q1_ref, lf0_ref, lf1_ref, k0_ref, k1_ref, v0_ref, v1_ref, g0_ref, g1_ref, ng_ref,
                 o_ref, s_ref, st_ref):
    i = pl.program_id(1)

    @pl.when(i == 0)
    def _():
        st_ref[...] = jnp.zeros_like(st_ref)

    n = HGRN_BLOCK
    nsub = n // HGRN_SUB
    heads = range(B_HEADS)
    qb = jnp.concatenate([q0_ref[...], q1_ref[...]], axis=1)
    lf = jnp.concatenate([lf0_ref[...], lf1_ref[...]], axis=1)
    lk = jnp.concatenate([k0_ref[...], k1_ref[...]], axis=1)
    v = jnp.concatenate([v0_ref[...], v1_ref[...]], axis=1)
    r = lax.broadcasted_iota(jnp.int32, (n, n), 0)
    c = lax.broadcasted_iota(jnp.int32, (n, n), 1)
    tri = jnp.where((c <= r) & (c >= r - (r & (HGRN_SUB - 1))), 1.0, 0.0).astype(BF16)
    lf_hi, lf_mid, lf_lo = _split3(lf)
    a_all = _dot(tri, lf_hi) + (_dot(tri, lf_mid) + _dot(tri, lf_lo))
    a3 = a_all.reshape(nsub, HGRN_SUB, B_HEADS * B_DIM)
    a_last3 = a3[:, HGRN_SUB - 1:HGRN_SUB, :]
    qb3 = qb.reshape(a3.shape)
    v3 = v.reshape(a3.shape)
    b3 = a3 - lk.reshape(a3.shape)
    qe3 = qb3 * jnp.exp2(a3)
    kd3 = jnp.exp2(a_last3 - b3)
    dec3 = jnp.exp2(a_last3)
    trow = lax.broadcasted_iota(jnp.int32, (1, HGRN_SUB, 1), 1)
    o_diag = []
    v_t = []
    for h in heads:
        hs = slice(h * B_DIM, (h + 1) * B_DIM)
        a_h, q_h, b_h, v_h = a3[:, :, hs], qb3[:, :, hs], b3[:, :, hs], v3[:, :, hs]
        acc = jnp.zeros((nsub, HGRN_SUB, B_DIM), F32)
        for s in range(HGRN_SUB):
            e = jnp.exp2(jnp.where(trow >= s, a_h - b_h[:, s:s + 1, :], -jnp.inf))
            col = jnp.sum(q_h * e, axis=-1, keepdims=True)
            acc = acc + col * v_h[:, s:s + 1, :]
        o_diag.append(acc.reshape(n, B_DIM))
        v_t.append(v[:, hs].T.astype(BF16))
    state = [st_ref[h] for h in heads]
    o_state = [[] for _ in heads]
    for ch in range(nsub):
        sl = slice(ch * HGRN_SUB, (ch + 1) * HGRN_SUB)
        for h in heads:
            hs = slice(h * B_DIM, (h + 1) * B_DIM)
            o_state[h].append(_dot_nt(qe3[ch, :, hs].astype(BF16), state[h].astype(BF16)))
            state[h] = state[h] * dec3[ch, :, hs] + _dot(v_t[h][:, sl], kd3[ch, :, hs].astype(BF16))
    g = jnp.concatenate([g0_ref[...], g1_ref[...]], axis=1)
    for h in heads:
        hs = slice(h * B_DIM, (h + 1) * B_DIM)
        st_ref[h] = state[h]
        o = jnp.concatenate(o_state[h], axis=0) + o_diag[h]
        y = o * lax.rsqrt(jnp.mean(o * o, axis=-1, keepdims=True) + NORM_EPS)
        o_ref[:, hs] = (y * ng_ref[...] * g[:, hs]).astype(o_ref.dtype)

    @pl.when(i == pl.num_programs(1) - 1)
    def _():
        for h in heads:
            s_ref[h] = state[h].T


def _hgrn_prompt(proj, norm_g):
    bsz, t, _ = proj.shape
    nh = B_HEADS
    width = nh * B_DIM
    half = width // 2
    base = (A_Q + 2 * A_KV) // half

    def col(off):
        return pl.BlockSpec((None, HGRN_BLOCK, half), lambda b, i: (b, i, base + off))

    return pl.pallas_call(
        _hgrn_kernel,
        grid=(bsz, t // HGRN_BLOCK),
        in_specs=[
            col(0), col(1), col(2), col(3), col(4), col(5), col(6), col(7), col(8), col(9),
            pl.BlockSpec((1, B_DIM), lambda b, i: (0, 0)),
        ],
        out_specs=[
            pl.BlockSpec((None, HGRN_BLOCK, width), lambda b, i: (b, i, 0)),
            pl.BlockSpec((None, nh, B_DIM, B_DIM), lambda b, i: (b, 0, 0, 0)),
        ],
        out_shape=[
            jax.ShapeDtypeStruct((bsz, t, width), BF16),
            jax.ShapeDtypeStruct((bsz, nh, B_DIM, B_DIM), F32),
        ],
        scratch_shapes=[pltpu.VMEM((nh, B_DIM, B_DIM), F32)],
        compiler_params=_params(("parallel", "arbitrary")),
        name="hgrn_prompt",
    )(*([proj] * 10), norm_g.reshape(1, B_DIM))


def _gdn_kernel(q_ref, k_ref, v_ref, gate_ref, ab_ref, alog_ref, dtb_ref, ng_ref, o_ref, s_out_ref, s_ref):
    i = pl.program_id(1)
    n = GDN_CHUNK

    @pl.when(i == 0)
    def _():
        s_ref[...] = jnp.zeros_like(s_ref)

    r = lax.broadcasted_iota(jnp.int32, (n, n), 0)
    c = lax.broadcasted_iota(jnp.int32, (n, n), 1)
    causal = c <= r
    strict = c < r
    tri = jnp.where(causal, 1.0, 0.0).astype(BF16)
    eye = jnp.where(r == c, 1.0, 0.0).astype(F32)
    base_mask = (r >> GDN_BASE_LOG) == (c >> GDN_BASE_LOG)
    level_masks = [((r >> (k + 1)) == (c >> (k + 1))) & ((r >> k) != (c >> k))
                   for k in range(GDN_BASE_LOG, GDN_CHUNK_LOG)]
    chunks = range(GDN_STEP_CHUNKS)
    heads = range(C_HEADS)
    g_cum, g_cum_t, beta_all = [], [], []
    for ch in chunks:
        ab = ab_ref[ch * n:(ch + 1) * n, :]
        log_alpha = -jnp.exp(alog_ref[...]) * _softplus(ab + dtb_ref[...])
        beta_all.append(jax.nn.sigmoid(ab))
        la_hi, la_mid, la_lo = _split3(log_alpha)
        g_cum.append(_dot(tri, la_hi) + (_dot(tri, la_mid) + _dot(tri, la_lo)))
        g_cum_t.append(g_cum[ch].T)

    items = [(ch, h) for ch in chunks for h in heads]
    pairs = range(len(items))

    def tile(ref, ch, h):
        return ref[ch * n:(ch + 1) * n, h * C_DIM:(h + 1) * C_DIM]

    qn = [tile(q_ref, ch, h) for ch, h in items]
    kn = [tile(k_ref, ch, h) for ch, h in items]
    vh = [tile(v_ref, ch, h) for ch, h in items]
    kn_b = [kn[j].astype(BF16) for j in pairs]
    g_col = [g_cum[ch][:, h:h + 1] for ch, h in items]
    beta = [beta_all[ch][:, C_HEADS + h:C_HEADS + h + 1] for ch, h in items]
    dec_incl = [jnp.exp(jnp.where(causal, g_col[j] - g_cum_t[ch][h:h + 1, :], -jnp.inf))
                for j, (ch, h) in enumerate(items)]
    e_g = [jnp.exp(g_col[j]) for j in pairs]
    kq_kt = [_dot_nt(jnp.concatenate([kn_b[j], qn[j].astype(BF16)], axis=0), kn_b[j]) for j in pairs]
    kkt = [kq_kt[j][:n] for j in pairs]
    qk = [kq_kt[j][n:] for j in pairs]
    l_mat = [beta[j] * kkt[j] * jnp.where(strict, dec_incl[j], 0.0) for j in pairs]
    a_pow = [jnp.where(base_mask, -l_mat[j], 0.0) for j in pairs]
    t_inv = [eye + a_pow[j] for j in pairs]
    for _ in range(GDN_BASE_LOG - 1):
        a_pow = [_dot_inv(a_pow[j], a_pow[j]) for j in pairs]
        t_inv = [t_inv[j] + _dot_inv(t_inv[j], a_pow[j]) for j in pairs]
    for lm in level_masks:
        x = [_dot_inv(jnp.where(lm, l_mat[j], 0.0), t_inv[j]) for j in pairs]
        t_inv = [t_inv[j] - _dot_inv(t_inv[j], x[j]) for j in pairs]
    uw = [_dot_inv(t_inv[j], jnp.concatenate([beta[j] * vh[j], (beta[j] * e_g[j]) * kn[j]], axis=1))
          for j in pairs]
    wq_b = [jnp.concatenate([uw[j][:, C_DIM:], qn[j] * e_g[j]], axis=0).astype(BF16) for j in pairs]
    qk_b = [(qk[j] * dec_incl[j]).astype(BF16) for j in pairs]
    g_last = [g_col[j][n - 1:n] for j in pairs]
    kd_t = [(kn[j] * jnp.exp(g_last[j] - g_col[j])).T.astype(BF16) for j in pairs]

    state = [s_ref[h] for h in heads]
    for ch in chunks:
        js = [ch * C_HEADS + h for h in heads]
        s_b = [state[h].astype(BF16) for h in heads]
        wq_s = [_dot(wq_b[js[h]], s_b[h]) for h in heads]
        delta_b = [(uw[js[h]][:, :C_DIM] - wq_s[h][:n]).astype(BF16) for h in heads]
        o = [wq_s[h][n:] + _dot(qk_b[js[h]], delta_b[h]) for h in heads]
        state = [jnp.exp(g_last[js[h]]) * state[h] + _dot(kd_t[js[h]], delta_b[h]) for h in heads]
        for h in heads:
            hs = slice(h * C_DIM, (h + 1) * C_DIM)
            y = o[h] * lax.rsqrt(jnp.mean(o[h] * o[h], axis=-1, keepdims=True) + NORM_EPS)
            o_ref[ch * n:(ch + 1) * n, hs] = (y * ng_ref[...] * _silu(tile(gate_ref, ch, h))).astype(o_ref.dtype)
    for h in heads:
        s_ref[h] = state[h]

    @pl.when(i == pl.num_programs(1) - 1)
    def _():
        s_out_ref[...] = s_ref[...]


def _gdn_prompt(proj, alog_row, dtb_row, norm_g):
    bsz, t, _ = proj.shape
    n = GDN_CHUNK * GDN_STEP_CHUNKS

    def col(cidx):
        return pl.BlockSpec((None, n, C_QK), lambda b, i: (b, i, cidx))

    one = pl.BlockSpec((1, LANES), lambda b, i: (0, 0))
    return pl.pallas_call(
        _gdn_kernel,
        grid=(bsz, t // n),
        in_specs=[
            col(0), col(1), col(2), col(3),
            pl.BlockSpec((None, n, LANES), lambda b, i: (b, i, (C_CONV_DIM + C_QK) // LANES)),
            one, one, one,
        ],
        out_specs=[
            pl.BlockSpec((None, n, C_QK), lambda b, i: (b, i, 0)),
            pl.BlockSpec((None, C_HEADS, C_DIM, C_DIM), lambda b, i: (b, 0, 0, 0)),
        ],
        out_shape=[
            jax.ShapeDtypeStruct((bsz, t, C_QK), BF16),
            jax.ShapeDtypeStruct((bsz, C_HEADS, C_DIM, C_DIM), F32),
        ],
        scratch_shapes=[pltpu.VMEM((C_HEADS, C_DIM, C_DIM), F32)],
        compiler_params=_params(("parallel", "arbitrary")),
        name="gdn_prompt",
    )(proj, proj, proj, proj, proj, alog_row, dtb_row, norm_g.reshape(1, C_DIM))


def _swa_dec_kernel(sink_ref, q_ref, kn_ref, vn_ref, ck_ref, cv_ref, o_ref, ok_ref, ov_ref):
    batch = range(q_ref.shape[0])
    row = lax.broadcasted_iota(jnp.int32, (WINDOW, LANES), 0)
    head = lax.broadcasted_iota(jnp.int32, (A_HEADS, WINDOW), 0)
    dist = (WINDOW - 1 - lax.broadcasted_iota(jnp.int32, (A_HEADS, WINDOW), 1)).astype(F32)
    scale = A_HEAD_DIM ** -0.5
    slope = jnp.zeros((A_HEADS, WINDOW), F32)
    sink = jnp.zeros((A_HEADS, 1), F32)
    for h in range(A_HEADS):
        slope = jnp.where(head == h, 2.0 ** (-8.0 * (h + 1) / A_HEADS), slope)
        sink = jnp.where(head[:, 0:1] == h, sink_ref[h], sink)
    bias = slope * dist
    first_kv = head[:, 0:A_HEAD_DIM] < A_GROUP
    keys = [jnp.where(row == WINDOW - 1, kn_ref[b], pltpu.roll(ck_ref[b], WINDOW - 1, 0)) for b in batch]
    vals = [jnp.where(row == WINDOW - 1, vn_ref[b], pltpu.roll(cv_ref[b], WINDOW - 1, 0)) for b in batch]
    for b in batch:
        ok_ref[b] = keys[b]
        ov_ref[b] = vals[b]
    q8 = [jnp.concatenate([jnp.where(first_kv, q_ref[b], 0.0), jnp.where(first_kv, 0.0, q_ref[b])],
                          axis=1).astype(BF16) for b in batch]
    s = [_dot_nt(q8[b], keys[b].astype(BF16)) * scale - bias for b in batch]
    m = [jnp.maximum(jnp.max(s[b], axis=-1, keepdims=True), sink) for b in batch]
    p = [jnp.exp(s[b] - m[b]) for b in batch]
    den = [jnp.sum(p[b], axis=-1, keepdims=True) + jnp.exp(sink - m[b]) for b in batch]
    o = [_dot(p[b].astype(BF16), vals[b].astype(BF16)) for b in batch]
    for b in batch:
        o_ref[b] = jnp.where(first_kv, o[b][:, :A_HEAD_DIM], o[b][:, A_HEAD_DIM:]) / den[b]


def _swa_decode(proj2, cache_k, cache_v, sinks):
    nbatch = proj2.shape[0]
    bb = 8
    q3 = proj2[:, :A_Q].reshape(nbatch, A_HEADS, A_HEAD_DIM)
    kn = proj2[:, A_Q:A_Q + A_KV].reshape(nbatch, 1, A_KV)
    vn = proj2[:, A_Q + A_KV:A_Q + 2 * A_KV].reshape(nbatch, 1, A_KV)
    ck = cache_k.reshape(nbatch, WINDOW, A_KV)
    cv = cache_v.reshape(nbatch, WINDOW, A_KV)
    qspec = pl.BlockSpec((bb, A_HEADS, A_HEAD_DIM), lambda j: (j, 0, 0))
    nspec = pl.BlockSpec((bb, 1, A_KV), lambda j: (j, 0, 0))
    cspec = pl.BlockSpec((bb, WINDOW, A_KV), lambda j: (j, 0, 0))
    o, nk, nv = pl.pallas_call(
        _swa_dec_kernel,
        grid=(nbatch // bb,),
        in_specs=[pl.BlockSpec(memory_space=pltpu.SMEM), qspec, nspec, nspec, cspec, cspec],
        out_specs=[qspec, cspec, cspec],
        out_shape=[
            jax.ShapeDtypeStruct((nbatch, A_HEADS, A_HEAD_DIM), F32),
            jax.ShapeDtypeStruct((nbatch, WINDOW, A_KV), F32),
            jax.ShapeDtypeStruct((nbatch, WINDOW, A_KV), F32),
        ],
        compiler_params=_params(("parallel",)),
        name="swa_decode",
    )(sinks, q3, kn, vn, ck, cv)
    shape5 = (nbatch, WINDOW, A_KV_HEADS, A_HEAD_DIM)
    return o.reshape(nbatch, A_Q), nk.reshape(shape5), nv.reshape(shape5)


def _cols(rows, heads):
    nbatch = rows.shape[0]
    x = rows.reshape(nbatch // DEC_BLOCK, DEC_BLOCK, heads, LANES)
    return x.transpose(2, 0, 3, 1)


def _hgrn_dec_kernel(qt_ref, ft_ref, lb_ref, v_ref, g_ref, ng_ref, s_ref, o_ref, so_ref, orow_ref):
    lb = lb_ref[...]
    qc = _silu(qt_ref[...])
    f = lb + (1.0 - lb) * jax.nn.sigmoid(ft_ref[...])
    kk = 1.0 - f
    v = v_ref[...]
    for bb in range(DEC_BLOCK):
        s_new = f[:, bb:bb + 1] * s_ref[bb] + kk[:, bb:bb + 1] * v[bb:bb + 1, :]
        so_ref[bb] = s_new
        orow_ref[bb:bb + 1, :] = jnp.sum(qc[:, bb:bb + 1] * s_new, axis=0, keepdims=True)
    o_ref[...] = _rms_gate(orow_ref[...], ng_ref[...], g_ref[...])


def _hgrn_decode(proj2, state, lower, norm_g):
    nbatch = proj2.shape[0]
    nh = B_HEADS
    base = A_Q + 2 * A_KV
    width = nh * B_DIM
    qt = _cols(proj2[:, base:base + width], nh)
    ft = _cols(proj2[:, base + width:base + 2 * width], nh)
    colspec = pl.BlockSpec((None, None, B_DIM, DEC_BLOCK), lambda h, j: (h, j, 0, 0))
    sspec = pl.BlockSpec((DEC_BLOCK, None, B_DIM, B_DIM), lambda h, j: (j, h, 0, 0))
    cb = base // LANES
    o, s_new = pl.pallas_call(
        _hgrn_dec_kernel,
        grid=(nh, nbatch // DEC_BLOCK),
        in_specs=[
            colspec, colspec,
            pl.BlockSpec((None, B_DIM, 1), lambda h, j: (h, 0, 0)),
            pl.BlockSpec((DEC_BLOCK, B_DIM), lambda h, j: (j, cb + 2 * nh + h)),
            pl.BlockSpec((DEC_BLOCK, B_DIM), lambda h, j: (j, cb + 3 * nh + h)),
            pl.BlockSpec((1, B_DIM), lambda h, j: (0, 0)),
            sspec,
        ],
        out_specs=[pl.BlockSpec((DEC_BLOCK, B_DIM), lambda h, j: (j, h)), sspec],
        out_shape=[
            jax.ShapeDtypeStruct((nbatch, width), F32),
            jax.ShapeDtypeStruct(state.shape, F32),
        ],
        scratch_shapes=[pltpu.VMEM((DEC_BLOCK, B_DIM), F32)],
        compiler_params=_params(("parallel", "parallel")),
        name="hgrn_decode",
    )(qt, ft, lower.reshape(nh, B_DIM, 1), proj2, proj2, norm_g.reshape(1, B_DIM), state)
    return o, s_new


def _gdn_prep_kernel(p_ref, h0_ref, h1_ref, h2_ref, cw_ref, alog_ref, dtb_ref, q_ref, k_ref, v_ref, ab_ref):
    acc = (h0_ref[...] * cw_ref[0:1, :] + h1_ref[...] * cw_ref[1:2, :] + h2_ref[...] * cw_ref[2:3, :]
           + p_ref[:, 0:C_CONV_DIM] * cw_ref[3:4, :])
    qkv = _silu(acc)
    for h in range(C_HEADS):
        hs = slice(h * C_DIM, (h + 1) * C_DIM)
        qh = qkv[:, h * C_DIM:(h + 1) * C_DIM]
        kh = qkv[:, C_QK + h * C_DIM:C_QK + (h + 1) * C_DIM]
        q_ref[:, hs] = qh * lax.rsqrt(jnp.sum(qh * qh, axis=-1, keepdims=True) + NORM_EPS) * (C_DIM ** -0.5)
        k_ref[:, hs] = kh * lax.rsqrt(jnp.sum(kh * kh, axis=-1, keepdims=True) + NORM_EPS)
    v_ref[...] = qkv[:, 2 * C_QK:3 * C_QK]
    ab = p_ref[:, C_CONV_DIM + C_QK:C_CONV_DIM + C_QK + LANES]
    alpha = jnp.exp(-jnp.exp(alog_ref[...]) * _softplus(ab + dtb_ref[...]))
    beta = jax.nn.sigmoid(ab)
    lane = lax.broadcasted_iota(jnp.int32, ab.shape, 1)
    ab_ref[...] = jnp.where(lane < C_HEADS, alpha, beta)


def _gdn_dec_kernel(qt_ref, kt_ref, v_ref, gate_ref, al_ref, be_ref, ng_ref, s_ref, o_ref, so_ref, orow_ref):
    qt = qt_ref[...]
    kt = kt_ref[...]
    v = v_ref[...]
    al = al_ref[...]
    be = be_ref[...]
    for bb in range(DEC_BLOCK):
        s = s_ref[bb]
        kc = kt[:, bb:bb + 1]
        a = al[bb:bb + 1, :]
        ks = jnp.sum(kc * s, axis=0, keepdims=True)
        delta = be[bb:bb + 1, :] * (v[bb:bb + 1, :] - a * ks)
        s_new = a * s + kc * delta
        so_ref[bb] = s_new
        orow_ref[bb:bb + 1, :] = jnp.sum(qt[:, bb:bb + 1] * s_new, axis=0, keepdims=True)
    o_ref[...] = _rms_gate(orow_ref[...], ng_ref[...], gate_ref[...])


def _gdn_decode(proj2, hist, state, conv_w, alog_row, dtb_row, norm_g):
    nbatch = proj2.shape[0]
    nh = C_HEADS
    full = lambda shape: pl.BlockSpec(shape, lambda: tuple(0 for _ in shape))
    hspec = full((nbatch, C_CONV_DIM))
    qn, kn, vc, ab = pl.pallas_call(
        _gdn_prep_kernel,
        grid=(),
        in_specs=[
            full((nbatch, ODD_IN_PAD)), hspec, hspec, hspec,
            full((C_CONV, C_CONV_DIM)), full((1, LANES)), full((1, LANES)),
        ],
        out_specs=[full((nbatch, C_QK)), full((nbatch, C_QK)), full((nbatch, C_QK)), full((nbatch, LANES))],
        out_shape=[
            jax.ShapeDtypeStruct((nbatch, C_QK), F32),
            jax.ShapeDtypeStruct((nbatch, C_QK), F32),
            jax.ShapeDtypeStruct((nbatch, C_QK), F32),
            jax.ShapeDtypeStruct((nbatch, LANES), F32),
        ],
        compiler_params=pltpu.CompilerParams(vmem_limit_bytes=VMEM_LIMIT),
        name="gdn_decode_prep",
    )(proj2, hist[:, 0], hist[:, 1], hist[:, 2], conv_w, alog_row, dtb_row)
    qt = _cols(qn, nh)
    kt = _cols(kn, nh)
    al = jnp.broadcast_to(ab[:, :nh].T[:, :, None], (nh, nbatch, LANES))
    be = jnp.broadcast_to(ab[:, nh:2 * nh].T[:, :, None], (nh, nbatch, LANES))
    colspec = pl.BlockSpec((None, None, C_DIM, DEC_BLOCK), lambda h, j: (h, j, 0, 0))
    sspec = pl.BlockSpec((DEC_BLOCK, None, C_DIM, C_DIM), lambda h, j: (j, h, 0, 0))
    rspec = pl.BlockSpec((None, DEC_BLOCK, LANES), lambda h, j: (h, j, 0))
    gcol = C_CONV_DIM // LANES
    o, s_new = pl.pallas_call(
        _gdn_dec_kernel,
        grid=(nh, nbatch // DEC_BLOCK),
        in_specs=[
            colspec, colspec,
            pl.BlockSpec((DEC_BLOCK, C_DIM), lambda h, j: (j, h)),
            pl.BlockSpec((DEC_BLOCK, C_DIM), lambda h, j: (j, gcol + h)),
            rspec, rspec,
            pl.BlockSpec((1, C_DIM), lambda h, j: (0, 0)),
            sspec,
        ],
        out_specs=[pl.BlockSpec((DEC_BLOCK, C_DIM), lambda h, j: (j, h)), sspec],
        out_shape=[
            jax.ShapeDtypeStruct((nbatch, C_QK), F32),
            jax.ShapeDtypeStruct(state.shape, F32),
        ],
        scratch_shapes=[pltpu.VMEM((DEC_BLOCK, C_DIM), F32)],
        compiler_params=_params(("parallel", "parallel")),
        name="gdn_decode",
    )(qt, kt, vc, proj2, al, be, norm_g.reshape(1, C_DIM), state)
    return o, s_new


def _trunk(x, mods, caches, p, tm):
    decode = caches is not None
    m = mods[0]
    ffn = functools.partial(_ffn, w_up=p["w_up"], w_down=p["w_down"], ln_g=p["ln_g"], ln_b=p["ln_b"], tm=tm)
    x = ffn(x, m, 0, layer=0, sub=0)
    if decode:
        proj = _inproj(x, m, 3, p["even_in"], tm)
        o_a, new_k, new_v = _swa_decode(proj[0], caches[0][0], caches[1][0], p["sinks"])
        o_b, s_hgrn = _hgrn_decode(proj[0], caches[2][0], p["lower"], p["hgrn_norm_g"])
        o_a, o_b = o_a[None], o_b[None]
    else:
        bsz = x.shape[0]
        proj = _inproj_even(x, m, 3, p["even_in"], p["lower"], tm)
        o_a = _swa_prompt(proj, p["sinks"])
        new_k = proj[:, -WINDOW:, A_Q:A_Q + A_KV].reshape(bsz, WINDOW, A_KV_HEADS, A_HEAD_DIM)
        new_v = proj[:, -WINDOW:, A_Q + A_KV:A_Q + 2 * A_KV].reshape(bsz, WINDOW, A_KV_HEADS, A_HEAD_DIM)
        o_b, s_hgrn = _hgrn_prompt(proj, p["hgrn_norm_g"])
    x = ffn(x, m, 6, layer=0, sub=1, mixer=(o_a, 0, o_b, 0, p["even_out"], 5))
    m = mods[1]
    x = ffn(x, m, 0, layer=1, sub=0)
    if decode:
        proj = _inproj(x, m, 3, p["odd_in"], tm)
        hist = caches[4][0]
        o_c, s_gdn = _gdn_decode(proj[0], hist, caches[3][0], p["conv_w"], p["alog_row"], p["dtb_row"],
                                 p["gdn_norm_g"])
        o_c = o_c[None]
        new_hist = jnp.concatenate([hist[:, 1:], proj[0][:, None, :C_CONV_DIM]], axis=1)
    else:
        proj, last_rows = _inproj_gdn(x, m, 3, p["odd_in"], p["conv_w"], tm)
        o_c, s_gdn = _gdn_prompt(proj, p["alog_row"], p["dtb_row"], p["gdn_norm_g"])
        new_hist = last_rows[:, -(C_CONV - 1):]
    x = ffn(x, m, 6, layer=1, sub=1, mixer=(o_c, 0, o_c, 1, p["odd_out"], 5))
    return x, new_k[None], new_v[None], s_hgrn[None], s_gdn[None], new_hist[None]


def kernel(x_prompt, x_sample, cache_swa_k, cache_swa_v, state_hgrn, state_gdn, state_gdn_conv, c_prompt, c_sample, ada_w, ada_b, ln_g, ln_b, ffn_w_up, ffn_w_down, even_w_in, even_w_out, swa_sinks, hgrn_norm_g, hgrn_lb_logits, odd_w_in, odd_w_out, gdn_conv_w, gdn_a_log, gdn_dt_bias, gdn_norm_g):
    n_prompt = c_prompt.shape[0]
    n_sample = c_sample.shape[0]
    pad_rows = (-(n_prompt + n_sample)) % 8
    c_all = jnp.concatenate([c_prompt, c_sample, jnp.zeros((pad_rows, D_MODEL), F32)], axis=0)
    mods = _ada_mods(c_all, ada_w, ada_b)
    mods_p = mods[:, :n_prompt].reshape(DEPTH, n_prompt, 1, N_MOD * D_MODEL)
    mods_s = mods[:, n_prompt:n_prompt + n_sample].reshape(DEPTH, 1, n_sample, N_MOD * D_MODEL)

    probs = jax.nn.softmax(hgrn_lb_logits.astype(F32), axis=0)
    lower = (jnp.cumsum(probs, axis=0)[1:] - probs[0])[0]

    def lane_row(v):
        return jnp.pad(v.astype(F32), (0, LANES - v.shape[0])).reshape(1, LANES)

    p = dict(
        w_up=ffn_w_up.astype(BF16), w_down=ffn_w_down.astype(BF16),
        ln_g=ln_g, ln_b=ln_b,
        even_in=even_w_in[0].astype(BF16), even_out=even_w_out[0].astype(BF16),
        odd_in=jnp.concatenate([odd_w_in[0].astype(BF16), jnp.zeros((D_MODEL, ODD_IN_PAD - ODD_IN), BF16)], axis=1),
        odd_out=odd_w_out[0].astype(BF16),
        sinks=swa_sinks[0], lower=lower, hgrn_norm_g=hgrn_norm_g[0],
        conv_w=gdn_conv_w[0], alog_row=lane_row(gdn_a_log[0]), dtb_row=lane_row(gdn_dt_bias[0]),
        gdn_norm_g=gdn_norm_g[0],
    )
    y_p, p_k, p_v, p_hgrn, p_gdn, p_conv = _trunk(x_prompt, mods_p, None, p, 512)
    caches = (cache_swa_k, cache_swa_v, state_hgrn, state_gdn, state_gdn_conv)
    x_s = x_sample.reshape(1, n_sample, D_MODEL)
    y_s, s_k, s_v, s_hgrn, s_gdn, s_conv = _trunk(x_s, mods_s, caches, p, n_sample)
    y_s = y_s.reshape(n_sample, 1, D_MODEL)
    return (y_p, y_s, p_k, p_v, p_hgrn, p_gdn, p_conv, s_k, s_v, s_hgrn, s_gdn, s_conv)
```

```python
import functools

import jax
import jax.numpy as jnp
from jax import lax
from jax.experimental import pallas as pl
from jax.experimental.pallas import tpu as pltpu

F32 = jnp.float32
BF16 = jnp.bfloat16

D_MODEL = 1024
DEPTH = 2
WINDOW = 128
A_HEADS = 8
A_KV_HEADS = 2
A_GROUP = A_HEADS // A_KV_HEADS
A_HEAD_DIM = 64
A_Q = A_HEADS * A_HEAD_DIM
A_KV = A_KV_HEADS * A_HEAD_DIM
B_HEADS = 4
B_DIM = 128
EVEN_IN = A_Q + 2 * A_KV + 4 * B_HEADS * B_DIM
C_HEADS = 8
C_DIM = 128
C_QK = C_HEADS * C_DIM
C_CONV = 4
C_CONV_DIM = 3 * C_QK
ODD_IN = C_CONV_DIM + C_QK + 2 * C_HEADS
D_FF = 2816
N_MOD = 9
DN_ALPHA = (2 * DEPTH) ** 0.25
LN_EPS = 1e-5
NORM_EPS = 1e-6
LANES = 128

FF_CHUNK = 256
FFN_TM = 1024
FFN_VMEM_LIMIT = 56 << 20
PROJ_TILE = EVEN_IN // 2
ODD_MAIN = C_CONV_DIM + C_QK
ODD_IN_PAD = ODD_MAIN + LANES
SWA_STEP_BLOCKS = 2
HGRN_BLOCK = 256
HGRN_SUB = 8
GDN_CHUNK_LOG = 7
GDN_CHUNK = 1 << GDN_CHUNK_LOG
GDN_BASE_LOG = 3
GDN_STEP_CHUNKS = 2
DEC_BLOCK = 64
VMEM_LIMIT = 48 << 20


def _params(sem, vmem=VMEM_LIMIT):
    return pltpu.CompilerParams(dimension_semantics=sem, vmem_limit_bytes=vmem)


def _silu(x):
    return x * jax.nn.sigmoid(x)


def _softplus(x):
    return jnp.maximum(x, 0.0) + jnp.log(1.0 + jnp.exp(-jnp.abs(x)))


def _layer_norm(y, g, b):
    mu = jnp.mean(y, axis=-1, keepdims=True)
    d = y - mu
    var = jnp.mean(d * d, axis=-1, keepdims=True)
    return d * lax.rsqrt(var + LN_EPS) * g + b


def _rms_gate(o, norm_g, gate):
    y = o * lax.rsqrt(jnp.mean(o * o, axis=-1, keepdims=True) + NORM_EPS)
    return y * norm_g * _silu(gate)


def _dot(a, b, precision=None):
    return jnp.dot(a, b, preferred_element_type=F32, precision=precision)


def _dot_inv(a, b):
    return _dot(a.astype(BF16), b.astype(BF16))


def _dot_nt(a, b, precision=None):
    return lax.dot_general(a, b, (((1,), (1,)), ((), ())), preferred_element_type=F32, precision=precision)


def _ada_kernel(c_ref, w_ref, b_ref, os_ref, op_ref):
    cs = _silu(c_ref[...]).astype(BF16)
    res = _dot(cs, w_ref[...].astype(BF16)) + b_ref[...]
    n_sample = os_ref.shape[0]
    os_ref[...] = res[:n_sample]
    op_ref[...] = res[n_sample:]


def _ada_mods(c_all, n_sample, ada_w, ada_b):
    m = c_all.shape[0]
    n = N_MOD * D_MODEL
    tn = 1152
    return pl.pallas_call(
        _ada_kernel,
        grid=(DEPTH, n // tn),
        in_specs=[
            pl.BlockSpec((m, D_MODEL), lambda l, j: (0, 0)),
            pl.BlockSpec((None, D_MODEL, tn), lambda l, j: (l, 0, j)),
            pl.BlockSpec((None, 1, tn), lambda l, j: (l, 0, j)),
        ],
        out_specs=[
            pl.BlockSpec((None, n_sample, tn), lambda l, j: (l, 0, j)),
            pl.BlockSpec((None, m - n_sample, tn), lambda l, j: (l, 0, j)),
        ],
        out_shape=[
            jax.ShapeDtypeStruct((DEPTH, n_sample, n), F32),
            jax.ShapeDtypeStruct((DEPTH, m - n_sample, n), F32),
        ],
        compiler_params=_params(("parallel", "parallel")),
        name="ada_mods",
    )(c_all, ada_w, ada_b.reshape(DEPTH, 1, n))


def _mod_spec(mods, k, tm, grid_rank):
    per_token = mods.shape[1] != 1
    rows = tm if per_token else 1
    if grid_rank == 3:
        return pl.BlockSpec((None, rows, D_MODEL), lambda b, i, j: (b, i if per_token else 0, k))
    return pl.BlockSpec((None, rows, D_MODEL), lambda b, i: (b, i if per_token else 0, k))


def _ffn_body(x, sh_ref, sc_ref, g_ref, wu_ref, wd_ref, lg_ref, lb_ref, o_ref):
    h = (x * (1.0 + sc_ref[...]) + sh_ref[...]).astype(BF16)
    for c in range(D_FF // FF_CHUNK):
        lo = c * FF_CHUNK
        gate = _dot(h, wu_ref[:, lo:lo + FF_CHUNK])
        up = _dot(h, wu_ref[:, D_FF + lo:D_FF + lo + FF_CHUNK])
        act = (_silu(gate) * up).astype(BF16)
        part = _dot(act, wd_ref[lo:lo + FF_CHUNK, :])
        if c == 0:
            o_ref[...] = part
        else:
            o_ref[...] += part
    y = DN_ALPHA * x + (0.5 * g_ref[...]) * o_ref[...]
    o_ref[...] = _layer_norm(y, lg_ref[...], lb_ref[...])


def _ffn_kernel(x_ref, sh_ref, sc_ref, g_ref, wu_ref, wd_ref, lg_ref, lb_ref, o_ref):
    _ffn_body(x_ref[...], sh_ref, sc_ref, g_ref, wu_ref, wd_ref, lg_ref, lb_ref, o_ref)


def _mix_ffn_kernel(x_ref, o1_ref, o2_ref, gm_ref, w1_ref, w2_ref, lgm_ref, lbm_ref,
                    sh_ref, sc_ref, g_ref, wu_ref, wd_ref, lg_ref, lb_ref, o_ref):
    mix = _dot(o1_ref[...].astype(BF16), w1_ref[...]) + _dot(o2_ref[...].astype(BF16), w2_ref[...])
    x1 = _layer_norm(DN_ALPHA * x_ref[...] + gm_ref[...] * mix, lgm_ref[...], lbm_ref[...])
    _ffn_body(x1, sh_ref, sc_ref, g_ref, wu_ref, wd_ref, lg_ref, lb_ref, o_ref)


def _ffn(x, mods, k0, w_up, w_down, ln_g, ln_b, layer, sub, tm, mixer=None):
    bsz, t, _ = x.shape
    tm = min(t, max(tm, FFN_TM))
    half = D_MODEL // 2
    row = pl.BlockSpec((None, tm, D_MODEL), lambda b, i: (b, i, 0))
    vec = pl.BlockSpec((1, D_MODEL), lambda b, i: (0, 0))
    resident = pl.Buffered(1)
    ffn_specs = [
        _mod_spec(mods, k0, tm, 2), _mod_spec(mods, k0 + 1, tm, 2), _mod_spec(mods, k0 + 2, tm, 2),
        pl.BlockSpec((None, None, D_MODEL, 2 * D_FF), lambda b, i: (layer, sub, 0, 0), pipeline_mode=resident),
        pl.BlockSpec((None, None, D_FF, D_MODEL), lambda b, i: (layer, sub, 0, 0), pipeline_mode=resident),
        vec, vec,
    ]
    ffn_args = (mods, mods, mods, w_up, w_down,
                ln_g[layer, 2 * sub].reshape(1, D_MODEL), ln_b[layer, 2 * sub].reshape(1, D_MODEL))
    if mixer is None:
        body, specs, args = _ffn_kernel, [row] + ffn_specs, (x,) + ffn_args
    else:
        o1, c1, o2, c2, w_out, gate_k = mixer
        body = _mix_ffn_kernel
        specs = [
            row,
            pl.BlockSpec((None, tm, half), lambda b, i: (b, i, c1)),
            pl.BlockSpec((None, tm, half), lambda b, i: (b, i, c2)),
            _mod_spec(mods, gate_k, tm, 2),
            pl.BlockSpec((half, D_MODEL), lambda b, i: (0, 0), pipeline_mode=resident),
            pl.BlockSpec((half, D_MODEL), lambda b, i: (1, 0), pipeline_mode=resident),
            vec, vec,
        ] + ffn_specs
        args = (x, o1, o2, mods, w_out, w_out,
                ln_g[layer, 1].reshape(1, D_MODEL), ln_b[layer, 1].reshape(1, D_MODEL)) + ffn_args
    return pl.pallas_call(
        body,
        grid=(bsz, t // tm),
        in_specs=specs,
        out_specs=row,
        out_shape=jax.ShapeDtypeStruct(x.shape, F32),
        compiler_params=_params(("parallel", "parallel"), FFN_VMEM_LIMIT),
        name="ffn" if mixer is None else "mix_ffn",
    )(*args)


def _inproj_kernel(x_ref, sh_ref, sc_ref, *refs):
    w_refs, o_ref = refs[:-1], refs[-1]
    h = (x_ref[...] * (1.0 + sc_ref[...]) + sh_ref[...]).astype(BF16)
    col = 0
    for w_ref in w_refs:
        width = w_ref.shape[1]
        step = PROJ_TILE if width % PROJ_TILE == 0 else width
        for lo in range(0, width, step):
            o_ref[:, col + lo:col + lo + step] = _dot(h, w_ref[:, lo:lo + step])
        col += width


def _inproj(x, mods, k0, ws, tm):
    bsz, t, _ = x.shape
    n = sum(w.shape[1] for w in ws)
    return pl.pallas_call(
        _inproj_kernel,
        grid=(bsz, t // tm),
        in_specs=[
            pl.BlockSpec((None, tm, D_MODEL), lambda b, i: (b, i, 0)),
            _mod_spec(mods, k0, tm, 2), _mod_spec(mods, k0 + 1, tm, 2),
        ] + [pl.BlockSpec(w.shape, lambda b, i: (0, 0), pipeline_mode=pl.Buffered(1)) for w in ws],
        out_specs=pl.BlockSpec((None, tm, n), lambda b, i: (b, i, 0)),
        out_shape=jax.ShapeDtypeStruct((bsz, t, n), F32),
        compiler_params=_params(("parallel", "parallel")),
        name="inproj",
    )(x, mods, mods, *ws)


def _inproj_even_kernel(x_ref, sh_ref, sc_ref, w_ref, lb_ref, o_ref):
    h = (x_ref[...] * (1.0 + sc_ref[...]) + sh_ref[...]).astype(BF16)
    attn = A_Q + 2 * A_KV
    width = B_HEADS * B_DIM

    def proj(group):
        return _dot(h, w_ref[:, attn + group * width:attn + (group + 1) * width])

    o_ref[:, 0:attn] = _dot(h, w_ref[:, 0:attn])
    o_ref[:, attn:attn + width] = _silu(proj(0))
    lb = lb_ref[...]
    f = lb + (1.0 - lb) * jax.nn.sigmoid(proj(1))
    o_ref[:, attn + width:attn + 2 * width] = jnp.log2(f)
    o_ref[:, attn + 2 * width:attn + 3 * width] = jnp.log2(1.0 - f)
    o_ref[:, attn + 3 * width:attn + 4 * width] = proj(2)
    o_ref[:, attn + 4 * width:attn + 5 * width] = _silu(proj(3))


def _inproj_even(x, mods, k0, w, lower, tm):
    bsz, t, _ = x.shape
    n_in = w.shape[1]
    width = B_HEADS * B_DIM
    n_out = n_in + width
    return pl.pallas_call(
        _inproj_even_kernel,
        grid=(bsz, t // tm),
        in_specs=[
            pl.BlockSpec((None, tm, D_MODEL), lambda b, i: (b, i, 0)),
            _mod_spec(mods, k0, tm, 2), _mod_spec(mods, k0 + 1, tm, 2),
            pl.BlockSpec((D_MODEL, n_in), lambda b, i: (0, 0), pipeline_mode=pl.Buffered(1)),
            pl.BlockSpec((1, width), lambda b, i: (0, 0)),
        ],
        out_specs=pl.BlockSpec((None, tm, n_out), lambda b, i: (b, i, 0)),
        out_shape=jax.ShapeDtypeStruct((bsz, t, n_out), F32),
        compiler_params=_params(("parallel", "parallel")),
        name="inproj_even",
    )(x, mods, mods, w, lower.reshape(1, width))


def _inproj_gdn_kernel(x_ref, sh_ref, sc_ref, w_ref, wt_ref, cw_ref, o_ref, hist_ref, carry_ref):
    assert C_CONV == 4
    i = pl.program_id(1)
    tm = x_ref.shape[0]

    @pl.when(i == 0)
    def _():
        carry_ref[...] = jnp.zeros_like(carry_ref)

    h = (x_ref[...] * (1.0 + sc_ref[...]) + sh_ref[...]).astype(BF16)
    for sec in range(3):
        cs = slice(sec * C_QK, (sec + 1) * C_QK)
        raw = _dot(h, w_ref[:, cs])
        ext = jnp.concatenate([carry_ref[:, cs], raw], axis=0)
        prev = pltpu.roll(ext, 1, 0)
        near = ext * cw_ref[3:4, cs] + prev * cw_ref[2:3, cs]
        far = ext * cw_ref[1:2, cs] + prev * cw_ref[0:1, cs]
        z = near + pltpu.roll(far, 2, 0)
        act = _silu(z[8:])
        if sec == 2:
            o_ref[:, cs] = act
        else:
            for hd in range(C_HEADS):
                a = act[:, hd * C_DIM:(hd + 1) * C_DIM]
                inv = lax.rsqrt(jnp.sum(a * a, axis=-1, keepdims=True) + NORM_EPS)
                if sec == 0:
                    inv = inv * (C_DIM ** -0.5)
                o_ref[:, sec * C_QK + hd * C_DIM:sec * C_QK + (hd + 1) * C_DIM] = a * inv
        carry_ref[:, cs] = raw[tm - 8:tm]
        hist_ref[:, cs] = raw[tm - 8:tm]
    n_main = w_ref.shape[1]
    o_ref[:, C_CONV_DIM:n_main] = _dot(h, w_ref[:, C_CONV_DIM:])
    o_ref[:, n_main:] = _dot(h, wt_ref[...])


def _inproj_gdn(x, mods, k0, ws, conv_w, tm):
    bsz, t, _ = x.shape
    w, w_tail = ws
    n = w.shape[1] + w_tail.shape[1]
    return pl.pallas_call(
        _inproj_gdn_kernel,
        grid=(bsz, t // tm),
        in_specs=[
            pl.BlockSpec((None, tm, D_MODEL), lambda b, i: (b, i, 0)),
            _mod_spec(mods, k0, tm, 2), _mod_spec(mods, k0 + 1, tm, 2),
            pl.BlockSpec(w.shape, lambda b, i: (0, 0), pipeline_mode=pl.Buffered(1)),
            pl.BlockSpec(w_tail.shape, lambda b, i: (0, 0), pipeline_mode=pl.Buffered(1)),
            pl.BlockSpec((C_CONV, C_CONV_DIM), lambda b, i: (0, 0)),
        ],
        out_specs=[
            pl.BlockSpec((None, tm, n), lambda b, i: (b, i, 0)),
            pl.BlockSpec((None, 8, C_CONV_DIM), lambda b, i: (b, 0, 0)),
        ],
        out_shape=[
            jax.ShapeDtypeStruct((bsz, t, n), F32),
            jax.ShapeDtypeStruct((bsz, 8, C_CONV_DIM), F32),
        ],
        scratch_shapes=[pltpu.VMEM((8, C_CONV_DIM), F32)],
        compiler_params=_params(("parallel", "arbitrary")),
        name="inproj_gdn",
    )(x, mods, mods, w, w_tail, conv_w)


def _swa_kernel(sink_ref, q_ref, kc_ref, kp_ref, vc_ref, vp_ref, o_ref):
    i = pl.program_id(1)
    k_all = jnp.concatenate([kp_ref[...], kc_ref[...]], axis=0).astype(BF16)
    v_all = jnp.concatenate([vp_ref[...], vc_ref[...]], axis=0).astype(BF16)
    r = lax.broadcasted_iota(jnp.int32, (WINDOW, 2 * WINDOW), 0)
    c = lax.broadcasted_iota(jnp.int32, (WINDOW, 2 * WINDOW), 1)
    dist = r + WINDOW - c
    in_window = (dist >= 0) & (dist < WINDOW)
    valid = [in_window & (c >= jnp.where(i > 0, 0, WINDOW)) if b == 0 else in_window
             for b in range(SWA_STEP_BLOCKS)]
    dist = dist.astype(F32)
    scale = A_HEAD_DIM ** -0.5
    ksl = [slice((h // A_GROUP) * A_HEAD_DIM, (h // A_GROUP + 1) * A_HEAD_DIM) for h in range(A_HEADS)]
    slope = [2.0 ** (-8.0 * (h + 1) / A_HEADS) for h in range(A_HEADS)]
    ones = jnp.ones((2 * WINDOW, A_KV), BF16)
    items = [(b, h) for b in range(SWA_STEP_BLOCKS) for h in range(A_HEADS)]
    pairs = range(len(items))
    qh = [q_ref[b * WINDOW:(b + 1) * WINDOW, h * A_HEAD_DIM:(h + 1) * A_HEAD_DIM].astype(BF16) for b, h in items]
    s = [_dot_nt(qh[j], k_all[b * WINDOW:(b + 2) * WINDOW, ksl[h]]) for j, (b, h) in enumerate(items)]
    s = [jnp.where(valid[b], s[j] * scale - slope[h] * dist, -jnp.inf) for j, (b, h) in enumerate(items)]
    m = [jnp.maximum(jnp.max(s[j], axis=-1, keepdims=True), sink_ref[h]) for j, (b, h) in enumerate(items)]
    p = [jnp.exp(s[j] - m[j]).astype(BF16) for j in pairs]
    o = [_dot(p[j], v_all[b * WINDOW:(b + 2) * WINDOW]) for j, (b, h) in enumerate(items)]
    den = [_dot(p[j], ones) + jnp.exp(sink_ref[h] - m[j]) for j, (b, h) in enumerate(items)]
    for b in range(SWA_STEP_BLOCKS):
        o_ref[b * WINDOW:(b + 1) * WINDOW, :] = jnp.concatenate(
            [(o[j] / den[j])[:, ksl[h]] for j, (bb, h) in enumerate(items) if bb == b], axis=1).astype(o_ref.dtype)


def _swa_prompt(proj, sinks):
    bsz, t, _ = proj.shape
    rows = WINDOW * SWA_STEP_BLOCKS
    kcol = A_Q // LANES
    vcol = kcol + 1

    def cur(col):
        return pl.BlockSpec((None, rows, LANES), lambda b, i: (b, i, col))

    def prev(col):
        return pl.BlockSpec((None, WINDOW, LANES), lambda b, i: (b, jnp.maximum(i * SWA_STEP_BLOCKS - 1, 0), col))

    return pl.pallas_call(
        _swa_kernel,
        grid=(bsz, t // rows),
        in_specs=[
            pl.BlockSpec(memory_space=pltpu.SMEM),
            pl.BlockSpec((None, rows, A_Q), lambda b, i: (b, i, 0)),
            cur(kcol), prev(kcol), cur(vcol), prev(vcol),
        ],
        out_specs=pl.BlockSpec((None, rows, A_Q), lambda b, i: (b, i, 0)),
        out_shape=jax.ShapeDtypeStruct((bsz, t, A_Q), BF16),
        compiler_params=_params(("parallel", "parallel")),
        name="swa_prompt",
    )(sinks, proj, proj, proj, proj, proj)


def _split3(x):
    hi = x.astype(BF16)
    r1 = x - hi.astype(F32)
    mid = r1.astype(BF16)
    lo = (r1 - mid.astype(F32)).astype(BF16)
    return hi, mid, lo


def _hgrn_kernel(q0_ref, q1_ref, lf0_ref, lf1_ref, k0_ref, k1_ref, v0_ref, v1_ref, g0_ref, g1_ref, ng_ref,
                 o_ref, s_ref, st_ref):
    i = pl.program_id(1)

    @pl.when(i == 0)
    def _():
        st_ref[...] = jnp.zeros_like(st_ref)

    n = HGRN_BLOCK
    nsub = n // HGRN_SUB
    heads = range(B_HEADS)
    qb = jnp.concatenate([q0_ref[...], q1_ref[...]], axis=1)
    lf = jnp.concatenate([lf0_ref[...], lf1_ref[...]], axis=1)
    lk = jnp.concatenate([k0_ref[...], k1_ref[...]], axis=1)
    v = jnp.concatenate([v0_ref[...], v1_ref[...]], axis=1)
    r = lax.broadcasted_iota(jnp.int32, (n, n), 0)
    c = lax.broadcasted_iota(jnp.int32, (n, n), 1)
    tri = jnp.where((c <= r) & (c >= r - (r & (HGRN_SUB - 1))), 1.0, 0.0).astype(BF16)
    lf_hi, lf_mid, lf_lo = _split3(lf)
    a_all = _dot(tri, lf_hi) + (_dot(tri, lf_mid) + _dot(tri, lf_lo))
    a3 = a_all.reshape(nsub, HGRN_SUB, B_HEADS * B_DIM)
    a_last3 = a3[:, HGRN_SUB - 1:HGRN_SUB, :]
    qb3 = qb.reshape(a3.shape)
    v3 = v.reshape(a3.shape)
    b3 = a3 - lk.reshape(a3.shape)
    qe3 = qb3 * jnp.exp2(a3)
    kd3 = jnp.exp2(a_last3 - b3)
    dec3 = jnp.exp2(a_last3)
    trow = lax.broadcasted_iota(jnp.int32, (1, HGRN_SUB, 1), 1)
    o_diag = []
    v_t = []
    for h in heads:
        hs = slice(h * B_DIM, (h + 1) * B_DIM)
        a_h, q_h, b_h, v_h = a3[:, :, hs], qb3[:, :, hs], b3[:, :, hs], v3[:, :, hs]
        acc = jnp.zeros((nsub, HGRN_SUB, B_DIM), F32)
        for s in range(HGRN_SUB):
            e = jnp.exp2(jnp.where(trow >= s, a_h - b_h[:, s:s + 1, :], -jnp.inf))
            col = jnp.sum(q_h * e, axis=-1, keepdims=True)
            acc = acc + col * v_h[:, s:s + 1, :]
        o_diag.append(acc.reshape(n, B_DIM))
        v_t.append(v[:, hs].T.astype(BF16))
    state = [st_ref[h] for h in heads]
    o_state = [[] for _ in heads]
    for ch in range(nsub):
        sl = slice(ch * HGRN_SUB, (ch + 1) * HGRN_SUB)
        for h in heads:
            hs = slice(h * B_DIM, (h + 1) * B_DIM)
            o_state[h].append(_dot_nt(qe3[ch, :, hs].astype(BF16), state[h].astype(BF16)))
            state[h] = state[h] * dec3[ch, :, hs] + _dot(v_t[h][:, sl], kd3[ch, :, hs].astype(BF16))
    g = jnp.concatenate([g0_ref[...], g1_ref[...]], axis=1)
    for h in heads:
        hs = slice(h * B_DIM, (h + 1) * B_DIM)
        st_ref[h] = state[h]
        o = jnp.concatenate(o_state[h], axis=0) + o_diag[h]
        y = o * lax.rsqrt(jnp.mean(o * o, axis=-1, keepdims=True) + NORM_EPS)
        o_ref[:, hs] = (y * ng_ref[...] * g[:, hs]).astype(o_ref.dtype)

    @pl.when(i == pl.num_programs(1) - 1)
    def _():
        for h in heads:
            s_ref[h] = state[h].T


def _hgrn_prompt(proj, norm_g):
    bsz, t, _ = proj.shape
    nh = B_HEADS
    width = nh * B_DIM
    half = width // 2
    base = (A_Q + 2 * A_KV) // half

    def col(off):
        return pl.BlockSpec((None, HGRN_BLOCK, half), lambda b, i: (b, i, base + off))

    return pl.pallas_call(
        _hgrn_kernel,
        grid=(bsz, t // HGRN_BLOCK),
        in_specs=[
            col(0), col(1), col(2), col(3), col(4), col(5), col(6), col(7), col(8), col(9),
            pl.BlockSpec((1, B_DIM), lambda b, i: (0, 0)),
        ],
        out_specs=[
            pl.BlockSpec((None, HGRN_BLOCK, width), lambda b, i: (b, i, 0)),
            pl.BlockSpec((None, nh, B_DIM, B_DIM), lambda b, i: (b, 0, 0, 0)),
        ],
        out_shape=[
            jax.ShapeDtypeStruct((bsz, t, width), BF16),
            jax.ShapeDtypeStruct((bsz, nh, B_DIM, B_DIM), F32),
        ],
        scratch_shapes=[pltpu.VMEM((nh, B_DIM, B_DIM), F32)],
        compiler_params=_params(("parallel", "arbitrary")),
        name="hgrn_prompt",
    )(*([proj] * 10), norm_g.reshape(1, B_DIM))


def _gdn_kernel(q_ref, k_ref, v_ref, gate_ref, ab_ref, alog_ref, dtb_ref, ng_ref, o_ref, s_out_ref, s_ref):
    i = pl.program_id(1)
    n = GDN_CHUNK

    @pl.when(i == 0)
    def _():
        s_ref[...] = jnp.zeros_like(s_ref)

    r = lax.broadcasted_iota(jnp.int32, (n, n), 0)
    c = lax.broadcasted_iota(jnp.int32, (n, n), 1)
    causal = c <= r
    strict = c < r
    tri = jnp.where(causal, 1.0, 0.0).astype(BF16)
    eye = jnp.where(r == c, 1.0, 0.0).astype(F32)
    base_mask = (r >> GDN_BASE_LOG) == (c >> GDN_BASE_LOG)
    level_masks = [((r >> (k + 1)) == (c >> (k + 1))) & ((r >> k) != (c >> k))
                   for k in range(GDN_BASE_LOG, GDN_CHUNK_LOG)]
    chunks = range(GDN_STEP_CHUNKS)
    heads = range(C_HEADS)
    g_cum, g_cum_t, beta_all = [], [], []
    for ch in chunks:
        ab = ab_ref[ch * n:(ch + 1) * n, :]
        log_alpha = -jnp.exp(alog_ref[...]) * _softplus(ab + dtb_ref[...])
        beta_all.append(jax.nn.sigmoid(ab))
        la_hi, la_mid, la_lo = _split3(log_alpha)
        g_cum.append(_dot(tri, la_hi) + (_dot(tri, la_mid) + _dot(tri, la_lo)))
        g_cum_t.append(g_cum[ch].T)

    items = [(ch, h) for ch in chunks for h in heads]
    pairs = range(len(items))

    def tile(ref, ch, h):
        return ref[ch * n:(ch + 1) * n, h * C_DIM:(h + 1) * C_DIM]

    qn = [tile(q_ref, ch, h) for ch, h in items]
    kn = [tile(k_ref, ch, h) for ch, h in items]
    vh = [tile(v_ref, ch, h) for ch, h in items]
    kn_b = [kn[j].astype(BF16) for j in pairs]
    g_col = [g_cum[ch][:, h:h + 1] for ch, h in items]
    beta = [beta_all[ch][:, C_HEADS + h:C_HEADS + h + 1] for ch, h in items]
    dec_incl = [jnp.exp(jnp.where(causal, g_col[j] - g_cum_t[ch][h:h + 1, :], -jnp.inf))
                for j, (ch, h) in enumerate(items)]
    e_g = [jnp.exp(g_col[j]) for j in pairs]
    kq_kt = [_dot_nt(jnp.concatenate([kn_b[j], qn[j].astype(BF16)], axis=0), kn_b[j]) for j in pairs]
    kkt = [kq_kt[j][:n] for j in pairs]
    qk = [kq_kt[j][n:] for j in pairs]
    l_mat = [beta[j] * kkt[j] * jnp.where(strict, dec_incl[j], 0.0) for j in pairs]
    a_pow = [jnp.where(base_mask, -l_mat[j], 0.0) for j in pairs]
    t_inv = [eye + a_pow[j] for j in pairs]
    for _ in range(GDN_BASE_LOG - 1):
        a_pow = [_dot_inv(a_pow[j], a_pow[j]) for j in pairs]
        t_inv = [t_inv[j] + _dot_inv(t_inv[j], a_pow[j]) for j in pairs]
    for lm in level_masks:
        x = [_dot_inv(jnp.where(lm, l_mat[j], 0.0), t_inv[j]) for j in pairs]
        t_inv = [t_inv[j] - _dot_inv(t_inv[j], x[j]) for j in pairs]
    uw = [_dot_inv(t_inv[j], jnp.concatenate([beta[j] * vh[j], (beta[j] * e_g[j]) * kn[j]], axis=1))
          for j in pairs]
    wq_b = [jnp.concatenate([uw[j][:, C_DIM:], qn[j] * e_g[j]], axis=0).astype(BF16) for j in pairs]
    qk_b = [(qk[j] * dec_incl[j]).astype(BF16) for j in pairs]
    g_last = [g_col[j][n - 1:n] for j in pairs]
    kd_t = [(kn[j] * jnp.exp(g_last[j] - g_col[j])).T.astype(BF16) for j in pairs]

    state = [s_ref[h] for h in heads]
    for ch in chunks:
        js = [ch * C_HEADS + h for h in heads]
        s_b = [state[h].astype(BF16) for h in heads]
        wq_s = [_dot(wq_b[js[h]], s_b[h]) for h in heads]
        delta_b = [(uw[js[h]][:, :C_DIM] - wq_s[h][:n]).astype(BF16) for h in heads]
        o = [wq_s[h][n:] + _dot(qk_b[js[h]], delta_b[h]) for h in heads]
        state = [jnp.exp(g_last[js[h]]) * state[h] + _dot(kd_t[js[h]], delta_b[h]) for h in heads]
        for h in heads:
            hs = slice(h * C_DIM, (h + 1) * C_DIM)
            y = o[h] * lax.rsqrt(jnp.mean(o[h] * o[h], axis=-1, keepdims=True) + NORM_EPS)
            o_ref[ch * n:(ch + 1) * n, hs] = (y * ng_ref[...] * _silu(tile(gate_ref, ch, h))).astype(o_ref.dtype)
    for h in heads:
        s_ref[h] = state[h]

    @pl.when(i == pl.num_programs(1) - 1)
    def _():
        s_out_ref[...] = s_ref[...]


def _gdn_prompt(proj, alog_row, dtb_row, norm_g):
    bsz, t, _ = proj.shape
    n = GDN_CHUNK * GDN_STEP_CHUNKS

    def col(cidx):
        return pl.BlockSpec((None, n, C_QK), lambda b, i: (b, i, cidx))

    one = pl.BlockSpec((1, LANES), lambda b, i: (0, 0))
    return pl.pallas_call(
        _gdn_kernel,
        grid=(bsz, t // n),
        in_specs=[
            col(0), col(1), col(2), col(3),
            pl.BlockSpec((None, n, LANES), lambda b, i: (b, i, (C_CONV_DIM + C_QK) // LANES)),
            one, one, one,
        ],
        out_specs=[
            pl.BlockSpec((None, n, C_QK), lambda b, i: (b, i, 0)),
            pl.BlockSpec((None, C_HEADS, C_DIM, C_DIM), lambda b, i: (b, 0, 0, 0)),
        ],
        out_shape=[
            jax.ShapeDtypeStruct((bsz, t, C_QK), BF16),
            jax.ShapeDtypeStruct((bsz, C_HEADS, C_DIM, C_DIM), F32),
        ],
        scratch_shapes=[pltpu.VMEM((C_HEADS, C_DIM, C_DIM), F32)],
        compiler_params=_params(("parallel", "arbitrary")),
        name="gdn_prompt",
    )(proj, proj, proj, proj, proj, alog_row, dtb_row, norm_g.reshape(1, C_DIM))


def _swa_dec_kernel(sink_ref, q_ref, kn_ref, vn_ref, ck_ref, cv_ref, o_ref, ok_ref, ov_ref):
    batch = range(q_ref.shape[0])
    row = lax.broadcasted_iota(jnp.int32, (WINDOW, LANES), 0)
    head = lax.broadcasted_iota(jnp.int32, (A_HEADS, WINDOW), 0)
    dist = (WINDOW - 1 - lax.broadcasted_iota(jnp.int32, (A_HEADS, WINDOW), 1)).astype(F32)
    scale = A_HEAD_DIM ** -0.5
    slope = jnp.zeros((A_HEADS, WINDOW), F32)
    sink = jnp.zeros((A_HEADS, 1), F32)
    for h in range(A_HEADS):
        slope = jnp.where(head == h, 2.0 ** (-8.0 * (h + 1) / A_HEADS), slope)
        sink = jnp.where(head[:, 0:1] == h, sink_ref[h], sink)
    bias = slope * dist
    first_kv = head[:, 0:A_HEAD_DIM] < A_GROUP
    keys = [jnp.where(row == WINDOW - 1, kn_ref[b], pltpu.roll(ck_ref[b], WINDOW - 1, 0)) for b in batch]
    vals = [jnp.where(row == WINDOW - 1, vn_ref[b], pltpu.roll(cv_ref[b], WINDOW - 1, 0)) for b in batch]
    for b in batch:
        ok_ref[b] = keys[b]
        ov_ref[b] = vals[b]
    q8 = [jnp.concatenate([jnp.where(first_kv, q_ref[b], 0.0), jnp.where(first_kv, 0.0, q_ref[b])],
                          axis=1).astype(BF16) for b in batch]
    s = [_dot_nt(q8[b], keys[b].astype(BF16)) * scale - bias for b in batch]
    m = [jnp.maximum(jnp.max(s[b], axis=-1, keepdims=True), sink) for b in batch]
    p = [jnp.exp(s[b] - m[b]) for b in batch]
    den = [jnp.sum(p[b], axis=-1, keepdims=True) + jnp.exp(sink - m[b]) for b in batch]
    o = [_dot(p[b].astype(BF16), vals[b].astype(BF16)) for b in batch]
    for b in batch:
        o_ref[b] = jnp.where(first_kv, o[b][:, :A_HEAD_DIM], o[b][:, A_HEAD_DIM:]) / den[b]


def _swa_decode(proj2, cache_k, cache_v, sinks):
    nbatch = proj2.shape[0]
    bb = 8
    q3 = proj2[:, :A_Q].reshape(nbatch, A_HEADS, A_HEAD_DIM)
    kn = proj2[:, A_Q:A_Q + A_KV].reshape(nbatch, 1, A_KV)
    vn = proj2[:, A_Q + A_KV:A_Q + 2 * A_KV].reshape(nbatch, 1, A_KV)
    ck = cache_k.reshape(nbatch, WINDOW, A_KV)
    cv = cache_v.reshape(nbatch, WINDOW, A_KV)
    qspec = pl.BlockSpec((bb, A_HEADS, A_HEAD_DIM), lambda j: (j, 0, 0))
    nspec = pl.BlockSpec((bb, 1, A_KV), lambda j: (j, 0, 0))
    cspec = pl.BlockSpec((bb, WINDOW, A_KV), lambda j: (j, 0, 0))
    o, nk, nv = pl.pallas_call(
        _swa_dec_kernel,
        grid=(nbatch // bb,),
        in_specs=[pl.BlockSpec(memory_space=pltpu.SMEM), qspec, nspec, nspec, cspec, cspec],
        out_specs=[qspec, cspec, cspec],
        out_shape=[
            jax.ShapeDtypeStruct((nbatch, A_HEADS, A_HEAD_DIM), F32),
            jax.ShapeDtypeStruct((nbatch, WINDOW, A_KV), F32),
            jax.ShapeDtypeStruct((nbatch, WINDOW, A_KV), F32),
        ],
        compiler_params=_params(("parallel",)),
        name="swa_decode",
    )(sinks, q3, kn, vn, ck, cv)
    shape5 = (nbatch, WINDOW, A_KV_HEADS, A_HEAD_DIM)
    return o.reshape(nbatch, A_Q), nk.reshape(shape5), nv.reshape(shape5)


def _cols(rows, heads):
    nbatch = rows.shape[0]
    x = rows.reshape(nbatch // DEC_BLOCK, DEC_BLOCK, heads, LANES)
    return x.transpose(2, 0, 3, 1)


def _hgrn_dec_kernel(qt_ref, ft_ref, lb_ref, v_ref, g_ref, ng_ref, s_ref, o_ref, so_ref, orow_ref):
    lb = lb_ref[...]
    qc = _silu(qt_ref[...])
    f = lb + (1.0 - lb) * jax.nn.sigmoid(ft_ref[...])
    kk = 1.0 - f
    v = v_ref[...]
    for bb in range(DEC_BLOCK):
        s_new = f[:, bb:bb + 1] * s_ref[bb] + kk[:, bb:bb + 1] * v[bb:bb + 1, :]
        so_ref[bb] = s_new
        orow_ref[bb:bb + 1, :] = jnp.sum(qc[:, bb:bb + 1] * s_new, axis=0, keepdims=True)
    o_ref[...] = _rms_gate(orow_ref[...], ng_ref[...], g_ref[...])


def _hgrn_decode(proj2, state, lower, norm_g):
    nbatch = proj2.shape[0]
    nh = B_HEADS
    base = A_Q + 2 * A_KV
    width = nh * B_DIM
    qt = _cols(proj2[:, base:base + width], nh)
    ft = _cols(proj2[:, base + width:base + 2 * width], nh)
    colspec = pl.BlockSpec((None, None, B_DIM, DEC_BLOCK), lambda h, j: (h, j, 0, 0))
    sspec = pl.BlockSpec((DEC_BLOCK, None, B_DIM, B_DIM), lambda h, j: (j, h, 0, 0))
    cb = base // LANES
    o, s_new = pl.pallas_call(
        _hgrn_dec_kernel,
        grid=(nh, nbatch // DEC_BLOCK),
        in_specs=[
            colspec, colspec,
            pl.BlockSpec((None, B_DIM, 1), lambda h, j: (h, 0, 0)),
            pl.BlockSpec((DEC_BLOCK, B_DIM), lambda h, j: (j, cb + 2 * nh + h)),
            pl.BlockSpec((DEC_BLOCK, B_DIM), lambda h, j: (j, cb + 3 * nh + h)),
            pl.BlockSpec((1, B_DIM), lambda h, j: (0, 0)),
            sspec,
        ],
        out_specs=[pl.BlockSpec((DEC_BLOCK, B_DIM), lambda h, j: (j, h)), sspec],
        out_shape=[
            jax.ShapeDtypeStruct((nbatch, width), F32),
            jax.ShapeDtypeStruct(state.shape, F32),
        ],
        scratch_shapes=[pltpu.VMEM((DEC_BLOCK, B_DIM), F32)],
        compiler_params=_params(("parallel", "parallel")),
        name="hgrn_decode",
    )(qt, ft, lower.reshape(nh, B_DIM, 1), proj2, proj2, norm_g.reshape(1, B_DIM), state)
    return o, s_new


def _gdn_prep_kernel(p_ref, h0_ref, h1_ref, h2_ref, cw_ref, alog_ref, dtb_ref, q_ref, k_ref, v_ref, ab_ref):
    acc = (h0_ref[...] * cw_ref[0:1, :] + h1_ref[...] * cw_ref[1:2, :] + h2_ref[...] * cw_ref[2:3, :]
           + p_ref[:, 0:C_CONV_DIM] * cw_ref[3:4, :])
    qkv = _silu(acc)
    for h in range(C_HEADS):
        hs = slice(h * C_DIM, (h + 1) * C_DIM)
        qh = qkv[:, h * C_DIM:(h + 1) * C_DIM]
        kh = qkv[:, C_QK + h * C_DIM:C_QK + (h + 1) * C_DIM]
        q_ref[:, hs] = qh * lax.rsqrt(jnp.sum(qh * qh, axis=-1, keepdims=True) + NORM_EPS) * (C_DIM ** -0.5)
        k_ref[:, hs] = kh * lax.rsqrt(jnp.sum(kh * kh, axis=-1, keepdims=True) + NORM_EPS)
    v_ref[...] = qkv[:, 2 * C_QK:3 * C_QK]
    ab = p_ref[:, C_CONV_DIM + C_QK:C_CONV_DIM + C_QK + LANES]
    alpha = jnp.exp(-jnp.exp(alog_ref[...]) * _softplus(ab + dtb_ref[...]))
    beta = jax.nn.sigmoid(ab)
    lane = lax.broadcasted_iota(jnp.int32, ab.shape, 1)
    ab_ref[...] = jnp.where(lane < C_HEADS, alpha, beta)


def _gdn_dec_kernel(qt_ref, kt_ref, v_ref, gate_ref, al_ref, be_ref, ng_ref, s_ref, o_ref, so_ref, orow_ref):
    qt = qt_ref[...]
    kt = kt_ref[...]
    v = v_ref[...]
    al = al_ref[...]
    be = be_ref[...]
    for bb in range(DEC_BLOCK):
        s = s_ref[bb]
        kc = kt[:, bb:bb + 1]
        a = al[bb:bb + 1, :]
        ks = jnp.sum(kc * s, axis=0, keepdims=True)
        delta = be[bb:bb + 1, :] * (v[bb:bb + 1, :] - a * ks)
        s_new = a * s + kc * delta
        so_ref[bb] = s_new
        orow_ref[bb:bb + 1, :] = jnp.sum(qt[:, bb:bb + 1] * s_new, axis=0, keepdims=True)
    o_ref[...] = _rms_gate(orow_ref[...], ng_ref[...], gate_ref[...])


def _gdn_decode(proj2, hist, state, conv_w, alog_row, dtb_row, norm_g):
    nbatch = proj2.shape[0]
    nh = C_HEADS
    full = lambda shape: pl.BlockSpec(shape, lambda: tuple(0 for _ in shape))
    hspec = full((nbatch, C_CONV_DIM))
    qn, kn, vc, ab = pl.pallas_call(
        _gdn_prep_kernel,
        grid=(),
        in_specs=[
            full((nbatch, ODD_IN_PAD)), hspec, hspec, hspec,
            full((C_CONV, C_CONV_DIM)), full((1, LANES)), full((1, LANES)),
        ],
        out_specs=[full((nbatch, C_QK)), full((nbatch, C_QK)), full((nbatch, C_QK)), full((nbatch, LANES))],
        out_shape=[
            jax.ShapeDtypeStruct((nbatch, C_QK), F32),
            jax.ShapeDtypeStruct((nbatch, C_QK), F32),
            jax.ShapeDtypeStruct((nbatch, C_QK), F32),
            jax.ShapeDtypeStruct((nbatch, LANES), F32),
        ],
        compiler_params=pltpu.CompilerParams(vmem_limit_bytes=VMEM_LIMIT),
        name="gdn_decode_prep",
    )(proj2, hist[:, 0], hist[:, 1], hist[:, 2], conv_w, alog_row, dtb_row)
    qt = _cols(qn, nh)
    kt = _cols(kn, nh)
    al = jnp.broadcast_to(ab[:, :nh].T[:, :, None], (nh, nbatch, LANES))
    be = jnp.broadcast_to(ab[:, nh:2 * nh].T[:, :, None], (nh, nbatch, LANES))
    colspec = pl.BlockSpec((None, None, C_DIM, DEC_BLOCK), lambda h, j: (h, j, 0, 0))
    sspec = pl.BlockSpec((DEC_BLOCK, None, C_DIM, C_DIM), lambda h, j: (j, h, 0, 0))
    rspec = pl.BlockSpec((None, DEC_BLOCK, LANES), lambda h, j: (h, j, 0))
    gcol = C_CONV_DIM // LANES
    o, s_new = pl.pallas_call(
        _gdn_dec_kernel,
        grid=(nh, nbatch // DEC_BLOCK),
        in_specs=[
            colspec, colspec,
            pl.BlockSpec((DEC_BLOCK, C_DIM), lambda h, j: (j, h)),
            pl.BlockSpec((DEC_BLOCK, C_DIM), lambda h, j: (j, gcol + h)),
            rspec, rspec,
            pl.BlockSpec((1, C_DIM), lambda h, j: (0, 0)),
            sspec,
        ],
        out_specs=[pl.BlockSpec((DEC_BLOCK, C_DIM), lambda h, j: (j, h)), sspec],
        out_shape=[
            jax.ShapeDtypeStruct((nbatch, C_QK), F32),
            jax.ShapeDtypeStruct(state.shape, F32),
        ],
        scratch_shapes=[pltpu.VMEM((DEC_BLOCK, C_DIM), F32)],
        compiler_params=_params(("parallel", "parallel")),
        name="gdn_decode",
    )(qt, kt, vc, proj2, al, be, norm_g.reshape(1, C_DIM), state)
    return o, s_new


def _trunk(x, mods, caches, p, tm):
    decode = caches is not None
    m = mods[0]
    ffn = functools.partial(_ffn, w_up=p["w_up"], w_down=p["w_down"], ln_g=p["ln_g"], ln_b=p["ln_b"], tm=tm)
    x = ffn(x, m, 0, layer=0, sub=0)
    if decode:
        proj = _inproj(x, m, 3, (p["even_in"],), tm)
        o_a, new_k, new_v = _swa_decode(proj[0], caches[0][0], caches[1][0], p["sinks"])
        o_b, s_hgrn = _hgrn_decode(proj[0], caches[2][0], p["lower"], p["hgrn_norm_g"])
        o_a, o_b = o_a[None], o_b[None]
    else:
        bsz = x.shape[0]
        proj = _inproj_even(x, m, 3, p["even_in"], p["lower"], tm)
        o_a = _swa_prompt(proj, p["sinks"])
        new_k = proj[:, -WINDOW:, A_Q:A_Q + A_KV].reshape(bsz, WINDOW, A_KV_HEADS, A_HEAD_DIM)
        new_v = proj[:, -WINDOW:, A_Q + A_KV:A_Q + 2 * A_KV].reshape(bsz, WINDOW, A_KV_HEADS, A_HEAD_DIM)
        o_b, s_hgrn = _hgrn_prompt(proj, p["hgrn_norm_g"])
    x = ffn(x, m, 6, layer=0, sub=1, mixer=(o_a, 0, o_b, 0, p["even_out"], 5))
    m = mods[1]
    x = ffn(x, m, 0, layer=1, sub=0)
    if decode:
        proj = _inproj(x, m, 3, p["odd_in"], tm)
        hist = caches[4][0]
        o_c, s_gdn = _gdn_decode(proj[0], hist, caches[3][0], p["conv_w"], p["alog_row"], p["dtb_row"],
                                 p["gdn_norm_g"])
        o_c = o_c[None]
        new_hist = jnp.concatenate([hist[:, 1:], proj[0][:, None, :C_CONV_DIM]], axis=1)
    else:
        proj, last_rows = _inproj_gdn(x, m, 3, p["odd_in"], p["conv_w"], tm)
        o_c, s_gdn = _gdn_prompt(proj, p["alog_row"], p["dtb_row"], p["gdn_norm_g"])
        new_hist = last_rows[:, -(C_CONV - 1):]
    x = ffn(x, m, 6, layer=1, sub=1, mixer=(o_c, 0, o_c, 1, p["odd_out"], 5))
    return x, new_k[None], new_v[None], s_hgrn[None], s_gdn[None], new_hist[None]


def kernel(x_prompt, x_sample, cache_swa_k, cache_swa_v, state_hgrn, state_gdn, state_gdn_conv, c_prompt, c_sample, ada_w, ada_b, ln_g, ln_b, ffn_w_up, ffn_w_down, even_w_in, even_w_out, swa_sinks, hgrn_norm_g, hgrn_lb_logits, odd_w_in, odd_w_out, gdn_conv_w, gdn_a_log, gdn_dt_bias, gdn_norm_g):
    n_prompt = c_prompt.shape[0]
    n_sample = c_sample.shape[0]
    assert n_sample % 8 == 0
    pad_rows = (-n_prompt) % 8
    c_all = jnp.concatenate([c_sample, c_prompt, jnp.zeros((pad_rows, D_MODEL), F32)], axis=0)
    mods_s, mods_p = _ada_mods(c_all, n_sample, ada_w, ada_b)
    mods_p = mods_p[:, :n_prompt].reshape(DEPTH, n_prompt, 1, N_MOD * D_MODEL)
    mods_s = mods_s.reshape(DEPTH, 1, n_sample, N_MOD * D_MODEL)

    probs = jax.nn.softmax(hgrn_lb_logits.astype(F32), axis=0)
    lower = (jnp.cumsum(probs, axis=0)[1:] - probs[0])[0]

    def lane_row(v):
        return jnp.pad(v.astype(F32), (0, LANES - v.shape[0])).reshape(1, LANES)

    p = dict(
        w_up=ffn_w_up.astype(BF16), w_down=ffn_w_down.astype(BF16),
        ln_g=ln_g, ln_b=ln_b,
        even_in=even_w_in[0].astype(BF16), even_out=even_w_out[0].astype(BF16),
        odd_in=(odd_w_in[0][:, :ODD_MAIN].astype(BF16),
                jnp.pad(odd_w_in[0][:, ODD_MAIN:], ((0, 0), (0, ODD_IN_PAD - ODD_IN))).astype(BF16)),
        odd_out=odd_w_out[0].astype(BF16),
        sinks=swa_sinks[0], lower=lower, hgrn_norm_g=hgrn_norm_g[0],
        conv_w=gdn_conv_w[0], alog_row=lane_row(gdn_a_log[0]), dtb_row=lane_row(gdn_dt_bias[0]),
        gdn_norm_g=gdn_norm_g[0],
    )
    y_p, p_k, p_v, p_hgrn, p_gdn, p_conv = _trunk(x_prompt, mods_p, None, p, 512)
    caches = (cache_swa_k, cache_swa_v, state_hgrn, state_gdn, state_gdn_conv)
    x_s = x_sample.reshape(1, n_sample, D_MODEL)
    y_s, s_k, s_v, s_hgrn, s_gdn, s_conv = _trunk(x_s, mods_s, caches, p, n_sample)
    y_s = y_s.reshape(n_sample, 1, D_MODEL)
    return (y_p, y_s, p_k, p_v, p_hgrn, p_gdn, p_conv, s_k, s_v, s_hgrn, s_gdn, s_conv)
```

```python
import functools

import jax
import jax.numpy as jnp
from jax import lax
from jax.experimental import pallas as pl
from jax.experimental.pallas import tpu as pltpu

F32 = jnp.float32
BF16 = jnp.bfloat16

D_MODEL = 1024
DEPTH = 2
WINDOW = 128
A_HEADS = 8
A_KV_HEADS = 2
A_GROUP = A_HEADS // A_KV_HEADS
A_HEAD_DIM = 64
A_Q = A_HEADS * A_HEAD_DIM
A_KV = A_KV_HEADS * A_HEAD_DIM
B_HEADS = 4
B_DIM = 128
EVEN_IN = A_Q + 2 * A_KV + 4 * B_HEADS * B_DIM
C_HEADS = 8
C_DIM = 128
C_QK = C_HEADS * C_DIM
C_CONV = 4
C_CONV_DIM = 3 * C_QK
ODD_IN = C_CONV_DIM + C_QK + 2 * C_HEADS
D_FF = 2816
N_MOD = 9
DN_ALPHA = (2 * DEPTH) ** 0.25
LN_EPS = 1e-5
NORM_EPS = 1e-6
LANES = 128

FF_CHUNK = 256
FFN_TM = 1024
FFN_VMEM_LIMIT = 56 << 20
PROJ_TILE = EVEN_IN // 2
ODD_IN_PAD = -(-ODD_IN // PROJ_TILE) * PROJ_TILE
SWA_STEP_BLOCKS = 2
HGRN_BLOCK = 256
HGRN_SUB = 8
GDN_CHUNK_LOG = 7
GDN_CHUNK = 1 << GDN_CHUNK_LOG
GDN_BASE_LOG = 3
GDN_STEP_CHUNKS = 4
DEC_BLOCK = 64
VMEM_LIMIT = 48 << 20


def _params(sem, vmem=VMEM_LIMIT):
    return pltpu.CompilerParams(dimension_semantics=sem, vmem_limit_bytes=vmem)


def _silu(x):
    return x * jax.nn.sigmoid(x)


def _softplus(x):
    return jnp.maximum(x, 0.0) + jnp.log(1.0 + jnp.exp(-jnp.abs(x)))


def _layer_norm(y, g, b):
    mu = jnp.mean(y, axis=-1, keepdims=True)
    d = y - mu
    var = jnp.mean(d * d, axis=-1, keepdims=True)
    return d * lax.rsqrt(var + LN_EPS) * g + b


def _rms_gate(o, norm_g, gate):
    y = o * lax.rsqrt(jnp.mean(o * o, axis=-1, keepdims=True) + NORM_EPS)
    return y * norm_g * _silu(gate)


def _dot(a, b, precision=None):
    return jnp.dot(a, b, preferred_element_type=F32, precision=precision)


def _dot_inv(a, b):
    return _dot(a.astype(BF16), b.astype(BF16))


def _dot_nt(a, b, precision=None):
    return lax.dot_general(a, b, (((1,), (1,)), ((), ())), preferred_element_type=F32, precision=precision)


def _ada_kernel(c_ref, w_ref, b_ref, os_ref, op_ref):
    cs = _silu(c_ref[...]).astype(BF16)
    res = _dot(cs, w_ref[...].astype(BF16)) + b_ref[...]
    n_sample = os_ref.shape[0]
    os_ref[...] = res[:n_sample]
    op_ref[...] = res[n_sample:]


def _ada_mods(c_all, n_sample, ada_w, ada_b):
    m = c_all.shape[0]
    n = N_MOD * D_MODEL
    tn = 1152
    return pl.pallas_call(
        _ada_kernel,
        grid=(DEPTH, n // tn),
        in_specs=[
            pl.BlockSpec((m, D_MODEL), lambda l, j: (0, 0)),
            pl.BlockSpec((None, D_MODEL, tn), lambda l, j: (l, 0, j)),
            pl.BlockSpec((None, 1, tn), lambda l, j: (l, 0, j)),
        ],
        out_specs=[
            pl.BlockSpec((None, n_sample, tn), lambda l, j: (l, 0, j)),
            pl.BlockSpec((None, m - n_sample, tn), lambda l, j: (l, 0, j)),
        ],
        out_shape=[
            jax.ShapeDtypeStruct((DEPTH, n_sample, n), F32),
            jax.ShapeDtypeStruct((DEPTH, m - n_sample, n), F32),
        ],
        compiler_params=_params(("parallel", "parallel")),
        name="ada_mods",
    )(c_all, ada_w, ada_b.reshape(DEPTH, 1, n))


def _mod_spec(mods, k, tm, grid_rank):
    per_token = mods.shape[1] != 1
    rows = tm if per_token else 1
    if grid_rank == 3:
        return pl.BlockSpec((None, rows, D_MODEL), lambda b, i, j: (b, i if per_token else 0, k))
    return pl.BlockSpec((None, rows, D_MODEL), lambda b, i: (b, i if per_token else 0, k))


def _ffn_body(x, sh_ref, sc_ref, g_ref, wu_ref, wd_ref, lg_ref, lb_ref, o_ref):
    h = (x * (1.0 + sc_ref[...]) + sh_ref[...]).astype(BF16)
    for c in range(D_FF // FF_CHUNK):
        lo = c * FF_CHUNK
        gate = _dot(h, wu_ref[:, lo:lo + FF_CHUNK])
        up = _dot(h, wu_ref[:, D_FF + lo:D_FF + lo + FF_CHUNK])
        act = (_silu(gate) * up).astype(BF16)
        part = _dot(act, wd_ref[lo:lo + FF_CHUNK, :])
        if c == 0:
            o_ref[...] = part
        else:
            o_ref[...] += part
    y = DN_ALPHA * x + (0.5 * g_ref[...]) * o_ref[...]
    o_ref[...] = _layer_norm(y, lg_ref[...], lb_ref[...])


def _ffn_kernel(x_ref, sh_ref, sc_ref, g_ref, wu_ref, wd_ref, lg_ref, lb_ref, o_ref):
    _ffn_body(x_ref[...], sh_ref, sc_ref, g_ref, wu_ref, wd_ref, lg_ref, lb_ref, o_ref)


def _mix_ffn_kernel(x_ref, o1_ref, o2_ref, gm_ref, w1_ref, w2_ref, lgm_ref, lbm_ref,
                    sh_ref, sc_ref, g_ref, wu_ref, wd_ref, lg_ref, lb_ref, o_ref):
    mix = _dot(o1_ref[...].astype(BF16), w1_ref[...]) + _dot(o2_ref[...].astype(BF16), w2_ref[...])
    x1 = _layer_norm(DN_ALPHA * x_ref[...] + gm_ref[...] * mix, lgm_ref[...], lbm_ref[...])
    _ffn_body(x1, sh_ref, sc_ref, g_ref, wu_ref, wd_ref, lg_ref, lb_ref, o_ref)


def _ffn(x, mods, k0, w_up, w_down, ln_g, ln_b, layer, sub, tm, mixer=None):
    bsz, t, _ = x.shape
    tm = min(t, max(tm, FFN_TM))
    half = D_MODEL // 2
    row = pl.BlockSpec((None, tm, D_MODEL), lambda b, i: (b, i, 0))
    vec = pl.BlockSpec((1, D_MODEL), lambda b, i: (0, 0))
    resident = pl.Buffered(1)
    ffn_specs = [
        _mod_spec(mods, k0, tm, 2), _mod_spec(mods, k0 + 1, tm, 2), _mod_spec(mods, k0 + 2, tm, 2),
        pl.BlockSpec((None, None, D_MODEL, 2 * D_FF), lambda b, i: (layer, sub, 0, 0), pipeline_mode=resident),
        pl.BlockSpec((None, None, D_FF, D_MODEL), lambda b, i: (layer, sub, 0, 0), pipeline_mode=resident),
        vec, vec,
    ]
    ffn_args = (mods, mods, mods, w_up, w_down,
                ln_g[layer, 2 * sub].reshape(1, D_MODEL), ln_b[layer, 2 * sub].reshape(1, D_MODEL))
    if mixer is None:
        body, specs, args = _ffn_kernel, [row] + ffn_specs, (x,) + ffn_args
    else:
        o1, c1, o2, c2, w_out, gate_k = mixer
        body = _mix_ffn_kernel
        specs = [
            row,
            pl.BlockSpec((None, tm, half), lambda b, i: (b, i, c1)),
            pl.BlockSpec((None, tm, half), lambda b, i: (b, i, c2)),
            _mod_spec(mods, gate_k, tm, 2),
            pl.BlockSpec((half, D_MODEL), lambda b, i: (0, 0), pipeline_mode=resident),
            pl.BlockSpec((half, D_MODEL), lambda b, i: (1, 0), pipeline_mode=resident),
            vec, vec,
        ] + ffn_specs
        args = (x, o1, o2, mods, w_out, w_out,
                ln_g[layer, 1].reshape(1, D_MODEL), ln_b[layer, 1].reshape(1, D_MODEL)) + ffn_args
    return pl.pallas_call(
        body,
        grid=(bsz, t // tm),
        in_specs=specs,
        out_specs=row,
        out_shape=jax.ShapeDtypeStruct(x.shape, F32),
        compiler_params=_params(("parallel", "parallel"), FFN_VMEM_LIMIT),
        name="ffn" if mixer is None else "mix_ffn",
    )(*args)


def _inproj_kernel(x_ref, sh_ref, sc_ref, w_ref, o_ref):
    h = (x_ref[...] * (1.0 + sc_ref[...]) + sh_ref[...]).astype(BF16)
    for lo in range(0, w_ref.shape[1], PROJ_TILE):
        o_ref[:, lo:lo + PROJ_TILE] = _dot(h, w_ref[:, lo:lo + PROJ_TILE])


def _inproj(x, mods, k0, w, tm):
    bsz, t, _ = x.shape
    n = w.shape[1]
    return pl.pallas_call(
        _inproj_kernel,
        grid=(bsz, t // tm),
        in_specs=[
            pl.BlockSpec((None, tm, D_MODEL), lambda b, i: (b, i, 0)),
            _mod_spec(mods, k0, tm, 2), _mod_spec(mods, k0 + 1, tm, 2),
            pl.BlockSpec((D_MODEL, n), lambda b, i: (0, 0), pipeline_mode=pl.Buffered(1)),
        ],
        out_specs=pl.BlockSpec((None, tm, n), lambda b, i: (b, i, 0)),
        out_shape=jax.ShapeDtypeStruct((bsz, t, n), F32),
        compiler_params=_params(("parallel", "parallel")),
        name="inproj",
    )(x, mods, mods, w)


def _inproj_even_kernel(x_ref, sh_ref, sc_ref, w_ref, lb_ref, o_ref):
    h = (x_ref[...] * (1.0 + sc_ref[...]) + sh_ref[...]).astype(BF16)
    attn = A_Q + 2 * A_KV
    width = B_HEADS * B_DIM

    def proj(group):
        return _dot(h, w_ref[:, attn + group * width:attn + (group + 1) * width])

    o_ref[:, 0:attn] = _dot(h, w_ref[:, 0:attn])
    o_ref[:, attn:attn + width] = _silu(proj(0))
    lb = lb_ref[...]
    f = lb + (1.0 - lb) * jax.nn.sigmoid(proj(1))
    o_ref[:, attn + width:attn + 2 * width] = jnp.log2(f)
    o_ref[:, attn + 2 * width:attn + 3 * width] = jnp.log2(1.0 - f)
    o_ref[:, attn + 3 * width:attn + 4 * width] = proj(2)
    o_ref[:, attn + 4 * width:attn + 5 * width] = _silu(proj(3))


def _inproj_even(x, mods, k0, w, lower, tm):
    bsz, t, _ = x.shape
    n_in = w.shape[1]
    width = B_HEADS * B_DIM
    n_out = n_in + width
    return pl.pallas_call(
        _inproj_even_kernel,
        grid=(bsz, t // tm),
        in_specs=[
            pl.BlockSpec((None, tm, D_MODEL), lambda b, i: (b, i, 0)),
            _mod_spec(mods, k0, tm, 2), _mod_spec(mods, k0 + 1, tm, 2),
            pl.BlockSpec((D_MODEL, n_in), lambda b, i: (0, 0), pipeline_mode=pl.Buffered(1)),
            pl.BlockSpec((1, width), lambda b, i: (0, 0)),
        ],
        out_specs=pl.BlockSpec((None, tm, n_out), lambda b, i: (b, i, 0)),
        out_shape=jax.ShapeDtypeStruct((bsz, t, n_out), F32),
        compiler_params=_params(("parallel", "parallel")),
        name="inproj_even",
    )(x, mods, mods, w, lower.reshape(1, width))


def _inproj_gdn_kernel(x_ref, sh_ref, sc_ref, w_ref, cw_ref, o_ref, hist_ref, carry_ref):
    assert C_CONV == 4
    i = pl.program_id(1)
    tm = x_ref.shape[0]

    @pl.when(i == 0)
    def _():
        carry_ref[...] = jnp.zeros_like(carry_ref)

    h = (x_ref[...] * (1.0 + sc_ref[...]) + sh_ref[...]).astype(BF16)
    for sec in range(3):
        cs = slice(sec * C_QK, (sec + 1) * C_QK)
        raw = _dot(h, w_ref[:, cs])
        ext = jnp.concatenate([carry_ref[:, cs], raw], axis=0)
        prev = pltpu.roll(ext, 1, 0)
        near = ext * cw_ref[3:4, cs] + prev * cw_ref[2:3, cs]
        far = ext * cw_ref[1:2, cs] + prev * cw_ref[0:1, cs]
        z = near + pltpu.roll(far, 2, 0)
        act = _silu(z[8:])
        if sec == 2:
            o_ref[:, cs] = act
        else:
            for hd in range(C_HEADS):
                a = act[:, hd * C_DIM:(hd + 1) * C_DIM]
                inv = lax.rsqrt(jnp.sum(a * a, axis=-1, keepdims=True) + NORM_EPS)
                if sec == 0:
                    inv = inv * (C_DIM ** -0.5)
                o_ref[:, sec * C_QK + hd * C_DIM:sec * C_QK + (hd + 1) * C_DIM] = a * inv
        carry_ref[:, cs] = raw[tm - 8:tm]
        hist_ref[:, cs] = raw[tm - 8:tm]
    o_ref[:, C_CONV_DIM:] = _dot(h, w_ref[:, C_CONV_DIM:])


def _inproj_gdn(x, mods, k0, w, conv_w, tm):
    bsz, t, _ = x.shape
    n = w.shape[1]
    return pl.pallas_call(
        _inproj_gdn_kernel,
        grid=(bsz, t // tm),
        in_specs=[
            pl.BlockSpec((None, tm, D_MODEL), lambda b, i: (b, i, 0)),
            _mod_spec(mods, k0, tm, 2), _mod_spec(mods, k0 + 1, tm, 2),
            pl.BlockSpec((D_MODEL, n), lambda b, i: (0, 0), pipeline_mode=pl.Buffered(1)),
            pl.BlockSpec((C_CONV, C_CONV_DIM), lambda b, i: (0, 0)),
        ],
        out_specs=[
            pl.BlockSpec((None, tm, n), lambda b, i: (b, i, 0)),
            pl.BlockSpec((None, 8, C_CONV_DIM), lambda b, i: (b, 0, 0)),
        ],
        out_shape=[
            jax.ShapeDtypeStruct((bsz, t, n), F32),
            jax.ShapeDtypeStruct((bsz, 8, C_CONV_DIM), F32),
        ],
        scratch_shapes=[pltpu.VMEM((8, C_CONV_DIM), F32)],
        compiler_params=_params(("parallel", "arbitrary")),
        name="inproj_gdn",
    )(x, mods, mods, w, conv_w)


def _swa_kernel(sink_ref, q_ref, kc_ref, kp_ref, vc_ref, vp_ref, o_ref):
    i = pl.program_id(1)
    k_all = jnp.concatenate([kp_ref[...], kc_ref[...]], axis=0).astype(BF16)
    v_all = jnp.concatenate([vp_ref[...], vc_ref[...]], axis=0).astype(BF16)
    r = lax.broadcasted_iota(jnp.int32, (WINDOW, 2 * WINDOW), 0)
    c = lax.broadcasted_iota(jnp.int32, (WINDOW, 2 * WINDOW), 1)
    dist = r + WINDOW - c
    in_window = (dist >= 0) & (dist < WINDOW)
    valid = [in_window & (c >= jnp.where(i > 0, 0, WINDOW)) if b == 0 else in_window
             for b in range(SWA_STEP_BLOCKS)]
    dist = dist.astype(F32)
    scale = A_HEAD_DIM ** -0.5
    ksl = [slice((h // A_GROUP) * A_HEAD_DIM, (h // A_GROUP + 1) * A_HEAD_DIM) for h in range(A_HEADS)]
    slope = [2.0 ** (-8.0 * (h + 1) / A_HEADS) for h in range(A_HEADS)]
    ones = jnp.ones((2 * WINDOW, A_KV), BF16)
    items = [(b, h) for b in range(SWA_STEP_BLOCKS) for h in range(A_HEADS)]
    pairs = range(len(items))
    qh = [q_ref[b * WINDOW:(b + 1) * WINDOW, h * A_HEAD_DIM:(h + 1) * A_HEAD_DIM].astype(BF16) for b, h in items]
    s = [_dot_nt(qh[j], k_all[b * WINDOW:(b + 2) * WINDOW, ksl[h]]) for j, (b, h) in enumerate(items)]
    s = [jnp.where(valid[b], s[j] * scale - slope[h] * dist, -jnp.inf) for j, (b, h) in enumerate(items)]
    m = [jnp.maximum(jnp.max(s[j], axis=-1, keepdims=True), sink_ref[h]) for j, (b, h) in enumerate(items)]
    p = [jnp.exp(s[j] - m[j]).astype(BF16) for j in pairs]
    o = [_dot(p[j], v_all[b * WINDOW:(b + 2) * WINDOW]) for j, (b, h) in enumerate(items)]
    den = [_dot(p[j], ones) + jnp.exp(sink_ref[h] - m[j]) for j, (b, h) in enumerate(items)]
    for b in range(SWA_STEP_BLOCKS):
        o_ref[b * WINDOW:(b + 1) * WINDOW, :] = jnp.concatenate(
            [(o[j] / den[j])[:, ksl[h]] for j, (bb, h) in enumerate(items) if bb == b], axis=1).astype(o_ref.dtype)


def _swa_prompt(proj, sinks):
    bsz, t, _ = proj.shape
    rows = WINDOW * SWA_STEP_BLOCKS
    kcol = A_Q // LANES
    vcol = kcol + 1

    def cur(col):
        return pl.BlockSpec((None, rows, LANES), lambda b, i: (b, i, col))

    def prev(col):
        return pl.BlockSpec((None, WINDOW, LANES), lambda b, i: (b, jnp.maximum(i * SWA_STEP_BLOCKS - 1, 0), col))

    return pl.pallas_call(
        _swa_kernel,
        grid=(bsz, t // rows),
        in_specs=[
            pl.BlockSpec(memory_space=pltpu.SMEM),
            pl.BlockSpec((None, rows, A_Q), lambda b, i: (b, i, 0)),
            cur(kcol), prev(kcol), cur(vcol), prev(vcol),
        ],
        out_specs=pl.BlockSpec((None, rows, A_Q), lambda b, i: (b, i, 0)),
        out_shape=jax.ShapeDtypeStruct((bsz, t, A_Q), BF16),
        compiler_params=_params(("parallel", "parallel")),
        name="swa_prompt",
    )(sinks, proj, proj, proj, proj, proj)


def _split3(x):
    hi = x.astype(BF16)
    r1 = x - hi.astype(F32)
    mid = r1.astype(BF16)
    lo = (r1 - mid.astype(F32)).astype(BF16)
    return hi, mid, lo


def _hgrn_kernel(q0_ref, q1_ref, lf0_ref, lf1_ref, k0_ref, k1_ref, v0_ref, v1_ref, g0_ref, g1_ref, ng_ref,
                 o_ref, s_ref, st_ref):
    i = pl.program_id(1)

    @pl.when(i == 0)
    def _():
        st_ref[...] = jnp.zeros_like(st_ref)

    n = HGRN_BLOCK
    nsub = n // HGRN_SUB
    heads = range(B_HEADS)
    qb = jnp.concatenate([q0_ref[...], q1_ref[...]], axis=1)
    lf = jnp.concatenate([lf0_ref[...], lf1_ref[...]], axis=1)
    lk = jnp.concatenate([k0_ref[...], k1_ref[...]], axis=1)
    v = jnp.concatenate([v0_ref[...], v1_ref[...]], axis=1)
    r = lax.broadcasted_iota(jnp.int32, (n, n), 0)
    c = lax.broadcasted_iota(jnp.int32, (n, n), 1)
    tri = jnp.where((c <= r) & (c >= r - (r & (HGRN_SUB - 1))), 1.0, 0.0).astype(BF16)
    lf_hi, lf_mid, lf_lo = _split3(lf)
    a_all = _dot(tri, lf_hi) + (_dot(tri, lf_mid) + _dot(tri, lf_lo))
    a3 = a_all.reshape(nsub, HGRN_SUB, B_HEADS * B_DIM)
    a_last3 = a3[:, HGRN_SUB - 1:HGRN_SUB, :]
    qb3 = qb.reshape(a3.shape)
    v3 = v.reshape(a3.shape)
    b3 = a3 - lk.reshape(a3.shape)
    qe3 = qb3 * jnp.exp2(a3)
    kd3 = jnp.exp2(a_last3 - b3)
    dec3 = jnp.exp2(a_last3)
    trow = lax.broadcasted_iota(jnp.int32, (1, HGRN_SUB, 1), 1)
    o_diag = []
    v_t = []
    for h in heads:
        hs = slice(h * B_DIM, (h + 1) * B_DIM)
        a_h, q_h, b_h, v_h = a3[:, :, hs], qb3[:, :, hs], b3[:, :, hs], v3[:, :, hs]
        acc = jnp.zeros((nsub, HGRN_SUB, B_DIM), F32)
        for s in range(HGRN_SUB):
            e = jnp.exp2(jnp.where(trow >= s, a_h - b_h[:, s:s + 1, :], -jnp.inf))
            col = jnp.sum(q_h * e, axis=-1, keepdims=True)
            acc = acc + col * v_h[:, s:s + 1, :]
        o_diag.append(acc.reshape(n, B_DIM))
        v_t.append(v[:, hs].T.astype(BF16))
    state = [st_ref[h] for h in heads]
    o_state = [[] for _ in heads]
    for ch in range(nsub):
        sl = slice(ch * HGRN_SUB, (ch + 1) * HGRN_SUB)
        for h in heads:
            hs = slice(h * B_DIM, (h + 1) * B_DIM)
            o_state[h].append(_dot_nt(qe3[ch, :, hs].astype(BF16), state[h].astype(BF16)))
            state[h] = state[h] * dec3[ch, :, hs] + _dot(v_t[h][:, sl], kd3[ch, :, hs].astype(BF16))
    g = jnp.concatenate([g0_ref[...], g1_ref[...]], axis=1)
    for h in heads:
        hs = slice(h * B_DIM, (h + 1) * B_DIM)
        st_ref[h] = state[h]
        o = jnp.concatenate(o_state[h], axis=0) + o_diag[h]
        y = o * lax.rsqrt(jnp.mean(o * o, axis=-1, keepdims=True) + NORM_EPS)
        o_ref[:, hs] = (y * ng_ref[...] * g[:, hs]).astype(o_ref.dtype)

    @pl.when(i == pl.num_programs(1) - 1)
    def _():
        for h in heads:
            s_ref[h] = state[h].T


def _hgrn_prompt(proj, norm_g):
    bsz, t, _ = proj.shape
    nh = B_HEADS
    width = nh * B_DIM
    half = width // 2
    base = (A_Q + 2 * A_KV) // half

    def col(off):
        return pl.BlockSpec((None, HGRN_BLOCK, half), lambda b, i: (b, i, base + off))

    return pl.pallas_call(
        _hgrn_kernel,
        grid=(bsz, t // HGRN_BLOCK),
        in_specs=[
            col(0), col(1), col(2), col(3), col(4), col(5), col(6), col(7), col(8), col(9),
            pl.BlockSpec((1, B_DIM), lambda b, i: (0, 0)),
        ],
        out_specs=[
            pl.BlockSpec((None, HGRN_BLOCK, width), lambda b, i: (b, i, 0)),
            pl.BlockSpec((None, nh, B_DIM, B_DIM), lambda b, i: (b, 0, 0, 0)),
        ],
        out_shape=[
            jax.ShapeDtypeStruct((bsz, t, width), BF16),
            jax.ShapeDtypeStruct((bsz, nh, B_DIM, B_DIM), F32),
        ],
        scratch_shapes=[pltpu.VMEM((nh, B_DIM, B_DIM), F32)],
        compiler_params=_params(("parallel", "arbitrary")),
        name="hgrn_prompt",
    )(*([proj] * 10), norm_g.reshape(1, B_DIM))


def _gdn_kernel(q_ref, k_ref, v_ref, gate_ref, ab_ref, alog_ref, dtb_ref, ng_ref, o_ref, s_out_ref, s_ref):
    i = pl.program_id(1)
    n = GDN_CHUNK

    @pl.when(i == 0)
    def _():
        s_ref[...] = jnp.zeros_like(s_ref)

    r = lax.broadcasted_iota(jnp.int32, (n, n), 0)
    c = lax.broadcasted_iota(jnp.int32, (n, n), 1)
    causal = c <= r
    strict = c < r
    tri = jnp.where(causal, 1.0, 0.0).astype(BF16)
    eye = jnp.where(r == c, 1.0, 0.0).astype(F32)
    base_mask = (r >> GDN_BASE_LOG) == (c >> GDN_BASE_LOG)
    level_masks = [((r >> (k + 1)) == (c >> (k + 1))) & ((r >> k) != (c >> k))
                   for k in range(GDN_BASE_LOG, GDN_CHUNK_LOG)]
    chunks = range(GDN_STEP_CHUNKS)
    heads = range(C_HEADS)
    g_cum, g_cum_t, beta_all = [], [], []
    for ch in chunks:
        ab = ab_ref[ch * n:(ch + 1) * n, :]
        log_alpha = -jnp.exp(alog_ref[...]) * _softplus(ab + dtb_ref[...])
        beta_all.append(jax.nn.sigmoid(ab))
        la_hi, la_mid, la_lo = _split3(log_alpha)
        g_cum.append(_dot(tri, la_hi) + (_dot(tri, la_mid) + _dot(tri, la_lo)))
        g_cum_t.append(g_cum[ch].T)

    items = [(ch, h) for ch in chunks for h in heads]
    pairs = range(len(items))

    def tile(ref, ch, h):
        return ref[ch * n:(ch + 1) * n, h * C_DIM:(h + 1) * C_DIM]

    qn = [tile(q_ref, ch, h) for ch, h in items]
    kn = [tile(k_ref, ch, h) for ch, h in items]
    vh = [tile(v_ref, ch, h) for ch, h in items]
    kn_b = [kn[j].astype(BF16) for j in pairs]
    g_col = [g_cum[ch][:, h:h + 1] for ch, h in items]
    beta = [beta_all[ch][:, C_HEADS + h:C_HEADS + h + 1] for ch, h in items]
    dec_incl = [jnp.exp(jnp.where(causal, g_col[j] - g_cum_t[ch][h:h + 1, :], -jnp.inf))
                for j, (ch, h) in enumerate(items)]
    e_g = [jnp.exp(g_col[j]) for j in pairs]
    kq_kt = [_dot_nt(jnp.concatenate([kn_b[j], qn[j].astype(BF16)], axis=0), kn_b[j]) for j in pairs]
    kkt = [kq_kt[j][:n] for j in pairs]
    qk = [kq_kt[j][n:] for j in pairs]
    l_mat = [beta[j] * kkt[j] * jnp.where(strict, dec_incl[j], 0.0) for j in pairs]
    a_pow = [jnp.where(base_mask, -l_mat[j], 0.0) for j in pairs]
    t_inv = [eye + a_pow[j] for j in pairs]
    for _ in range(GDN_BASE_LOG - 1):
        a_pow = [_dot_inv(a_pow[j], a_pow[j]) for j in pairs]
        t_inv = [t_inv[j] + _dot_inv(t_inv[j], a_pow[j]) for j in pairs]
    for lm in level_masks:
        x = [_dot_inv(jnp.where(lm, l_mat[j], 0.0), t_inv[j]) for j in pairs]
        t_inv = [t_inv[j] - _dot_inv(t_inv[j], x[j]) for j in pairs]
    uw = [_dot_inv(t_inv[j], jnp.concatenate([beta[j] * vh[j], (beta[j] * e_g[j]) * kn[j]], axis=1))
          for j in pairs]
    wq_b = [jnp.concatenate([uw[j][:, C_DIM:], qn[j] * e_g[j]], axis=0).astype(BF16) for j in pairs]
    qk_b = [(qk[j] * dec_incl[j]).astype(BF16) for j in pairs]
    g_last = [g_col[j][n - 1:n] for j in pairs]
    kd_t = [(kn[j] * jnp.exp(g_last[j] - g_col[j])).T.astype(BF16) for j in pairs]

    state = [s_ref[h] for h in heads]
    for ch in chunks:
        js = [ch * C_HEADS + h for h in heads]
        s_b = [state[h].astype(BF16) for h in heads]
        wq_s = [_dot(wq_b[js[h]], s_b[h]) for h in heads]
        delta_b = [(uw[js[h]][:, :C_DIM] - wq_s[h][:n]).astype(BF16) for h in heads]
        o = [wq_s[h][n:] + _dot(qk_b[js[h]], delta_b[h]) for h in heads]
        state = [jnp.exp(g_last[js[h]]) * state[h] + _dot(kd_t[js[h]], delta_b[h]) for h in heads]
        for h in heads:
            hs = slice(h * C_DIM, (h + 1) * C_DIM)
            y = o[h] * lax.rsqrt(jnp.mean(o[h] * o[h], axis=-1, keepdims=True) + NORM_EPS)
            o_ref[ch * n:(ch + 1) * n, hs] = (y * ng_ref[...] * _silu(tile(gate_ref, ch, h))).astype(o_ref.dtype)
    for h in heads:
        s_ref[h] = state[h]

    @pl.when(i == pl.num_programs(1) - 1)
    def _():
        s_out_ref[...] = s_ref[...]


def _gdn_prompt(proj, alog_row, dtb_row, norm_g):
    bsz, t, _ = proj.shape
    n = GDN_CHUNK * GDN_STEP_CHUNKS

    def col(cidx):
        return pl.BlockSpec((None, n, C_QK), lambda b, i: (b, i, cidx))

    one = pl.BlockSpec((1, LANES), lambda b, i: (0, 0))
    return pl.pallas_call(
        _gdn_kernel,
        grid=(bsz, t // n),
        in_specs=[
            col(0), col(1), col(2), col(3),
            pl.BlockSpec((None, n, LANES), lambda b, i: (b, i, (C_CONV_DIM + C_QK) // LANES)),
            one, one, one,
        ],
        out_specs=[
            pl.BlockSpec((None, n, C_QK), lambda b, i: (b, i, 0)),
            pl.BlockSpec((None, C_HEADS, C_DIM, C_DIM), lambda b, i: (b, 0, 0, 0)),
        ],
        out_shape=[
            jax.ShapeDtypeStruct((bsz, t, C_QK), BF16),
            jax.ShapeDtypeStruct((bsz, C_HEADS, C_DIM, C_DIM), F32),
        ],
        scratch_shapes=[pltpu.VMEM((C_HEADS, C_DIM, C_DIM), F32)],
        compiler_params=_params(("parallel", "arbitrary")),
        name="gdn_prompt",
    )(proj, proj, proj, proj, proj, alog_row, dtb_row, norm_g.reshape(1, C_DIM))


def _swa_dec_kernel(sink_ref, q_ref, kn_ref, vn_ref, ck_ref, cv_ref, o_ref, ok_ref, ov_ref):
    batch = range(q_ref.shape[0])
    row = lax.broadcasted_iota(jnp.int32, (WINDOW, LANES), 0)
    head = lax.broadcasted_iota(jnp.int32, (A_HEADS, WINDOW), 0)
    dist = (WINDOW - 1 - lax.broadcasted_iota(jnp.int32, (A_HEADS, WINDOW), 1)).astype(F32)
    scale = A_HEAD_DIM ** -0.5
    slope = jnp.zeros((A_HEADS, WINDOW), F32)
    sink = jnp.zeros((A_HEADS, 1), F32)
    for h in range(A_HEADS):
        slope = jnp.where(head == h, 2.0 ** (-8.0 * (h + 1) / A_HEADS), slope)
        sink = jnp.where(head[:, 0:1] == h, sink_ref[h], sink)
    bias = slope * dist
    first_kv = head[:, 0:A_HEAD_DIM] < A_GROUP
    keys = [jnp.where(row == WINDOW - 1, kn_ref[b], pltpu.roll(ck_ref[b], WINDOW - 1, 0)) for b in batch]
    vals = [jnp.where(row == WINDOW - 1, vn_ref[b], pltpu.roll(cv_ref[b], WINDOW - 1, 0)) for b in batch]
    for b in batch:
        ok_ref[b] = keys[b]
        ov_ref[b] = vals[b]
    q8 = [jnp.concatenate([jnp.where(first_kv, q_ref[b], 0.0), jnp.where(first_kv, 0.0, q_ref[b])],
                          axis=1).astype(BF16) for b in batch]
    s = [_dot_nt(q8[b], keys[b].astype(BF16)) * scale - bias for b in batch]
    m = [jnp.maximum(jnp.max(s[b], axis=-1, keepdims=True), sink) for b in batch]
    p = [jnp.exp(s[b] - m[b]) for b in batch]
    den = [jnp.sum(p[b], axis=-1, keepdims=True) + jnp.exp(sink - m[b]) for b in batch]
    o = [_dot(p[b].astype(BF16), vals[b].astype(BF16)) for b in batch]
    for b in batch:
        o_ref[b] = jnp.where(first_kv, o[b][:, :A_HEAD_DIM], o[b][:, A_HEAD_DIM:]) / den[b]


def _swa_decode(proj2, cache_k, cache_v, sinks):
    nbatch = proj2.shape[0]
    bb = 8
    q3 = proj2[:, :A_Q].reshape(nbatch, A_HEADS, A_HEAD_DIM)
    kn = proj2[:, A_Q:A_Q + A_KV].reshape(nbatch, 1, A_KV)
    vn = proj2[:, A_Q + A_KV:A_Q + 2 * A_KV].reshape(nbatch, 1, A_KV)
    ck = cache_k.reshape(nbatch, WINDOW, A_KV)
    cv = cache_v.reshape(nbatch, WINDOW, A_KV)
    qspec = pl.BlockSpec((bb, A_HEADS, A_HEAD_DIM), lambda j: (j, 0, 0))
    nspec = pl.BlockSpec((bb, 1, A_KV), lambda j: (j, 0, 0))
    cspec = pl.BlockSpec((bb, WINDOW, A_KV), lambda j: (j, 0, 0))
    o, nk, nv = pl.pallas_call(
        _swa_dec_kernel,
        grid=(nbatch // bb,),
        in_specs=[pl.BlockSpec(memory_space=pltpu.SMEM), qspec, nspec, nspec, cspec, cspec],
        out_specs=[qspec, cspec, cspec],
        out_shape=[
            jax.ShapeDtypeStruct((nbatch, A_HEADS, A_HEAD_DIM), F32),
            jax.ShapeDtypeStruct((nbatch, WINDOW, A_KV), F32),
            jax.ShapeDtypeStruct((nbatch, WINDOW, A_KV), F32),
        ],
        compiler_params=_params(("parallel",)),
        name="swa_decode",
    )(sinks, q3, kn, vn, ck, cv)
    shape5 = (nbatch, WINDOW, A_KV_HEADS, A_HEAD_DIM)
    return o.reshape(nbatch, A_Q), nk.reshape(shape5), nv.reshape(shape5)


def _cols(rows, heads):
    nbatch = rows.shape[0]
    x = rows.reshape(nbatch // DEC_BLOCK, DEC_BLOCK, heads, LANES)
    return x.transpose(2, 0, 3, 1)


def _hgrn_dec_kernel(qt_ref, ft_ref, lb_ref, v_ref, g_ref, ng_ref, s_ref, o_ref, so_ref, orow_ref):
    lb = lb_ref[...]
    qc = _silu(qt_ref[...])
    f = lb + (1.0 - lb) * jax.nn.sigmoid(ft_ref[...])
    kk = 1.0 - f
    v = v_ref[...]
    for bb in range(DEC_BLOCK):
        s_new = f[:, bb:bb + 1] * s_ref[bb] + kk[:, bb:bb + 1] * v[bb:bb + 1, :]
        so_ref[bb] = s_new
        orow_ref[bb:bb + 1, :] = jnp.sum(qc[:, bb:bb + 1] * s_new, axis=0, keepdims=True)
    o_ref[...] = _rms_gate(orow_ref[...], ng_ref[...], g_ref[...])


def _hgrn_decode(proj2, state, lower, norm_g):
    nbatch = proj2.shape[0]
    nh = B_HEADS
    base = A_Q + 2 * A_KV
    width = nh * B_DIM
    qt = _cols(proj2[:, base:base + width], nh)
    ft = _cols(proj2[:, base + width:base + 2 * width], nh)
    colspec = pl.BlockSpec((None, None, B_DIM, DEC_BLOCK), lambda h, j: (h, j, 0, 0))
    sspec = pl.BlockSpec((DEC_BLOCK, None, B_DIM, B_DIM), lambda h, j: (j, h, 0, 0))
    cb = base // LANES
    o, s_new = pl.pallas_call(
        _hgrn_dec_kernel,
        grid=(nh, nbatch // DEC_BLOCK),
        in_specs=[
            colspec, colspec,
            pl.BlockSpec((None, B_DIM, 1), lambda h, j: (h, 0, 0)),
            pl.BlockSpec((DEC_BLOCK, B_DIM), lambda h, j: (j, cb + 2 * nh + h)),
            pl.BlockSpec((DEC_BLOCK, B_DIM), lambda h, j: (j, cb + 3 * nh + h)),
            pl.BlockSpec((1, B_DIM), lambda h, j: (0, 0)),
            sspec,
        ],
        out_specs=[pl.BlockSpec((DEC_BLOCK, B_DIM), lambda h, j: (j, h)), sspec],
        out_shape=[
            jax.ShapeDtypeStruct((nbatch, width), F32),
            jax.ShapeDtypeStruct(state.shape, F32),
        ],
        scratch_shapes=[pltpu.VMEM((DEC_BLOCK, B_DIM), F32)],
        compiler_params=_params(("parallel", "parallel")),
        name="hgrn_decode",
    )(qt, ft, lower.reshape(nh, B_DIM, 1), proj2, proj2, norm_g.reshape(1, B_DIM), state)
    return o, s_new


def _gdn_prep_kernel(p_ref, h0_ref, h1_ref, h2_ref, cw_ref, alog_ref, dtb_ref, q_ref, k_ref, v_ref, ab_ref):
    acc = (h0_ref[...] * cw_ref[0:1, :] + h1_ref[...] * cw_ref[1:2, :] + h2_ref[...] * cw_ref[2:3, :]
           + p_ref[:, 0:C_CONV_DIM] * cw_ref[3:4, :])
    qkv = _silu(acc)
    for h in range(C_HEADS):
        hs = slice(h * C_DIM, (h + 1) * C_DIM)
        qh = qkv[:, h * C_DIM:(h + 1) * C_DIM]
        kh = qkv[:, C_QK + h * C_DIM:C_QK + (h + 1) * C_DIM]
        q_ref[:, hs] = qh * lax.rsqrt(jnp.sum(qh * qh, axis=-1, keepdims=True) + NORM_EPS) * (C_DIM ** -0.5)
        k_ref[:, hs] = kh * lax.rsqrt(jnp.sum(kh * kh, axis=-1, keepdims=True) + NORM_EPS)
    v_ref[...] = qkv[:, 2 * C_QK:3 * C_QK]
    ab = p_ref[:, C_CONV_DIM + C_QK:C_CONV_DIM + C_QK + LANES]
    alpha = jnp.exp(-jnp.exp(alog_ref[...]) * _softplus(ab + dtb_ref[...]))
    beta = jax.nn.sigmoid(ab)
    lane = lax.broadcasted_iota(jnp.int32, ab.shape, 1)
    ab_ref[...] = jnp.where(lane < C_HEADS, alpha, beta)


def _gdn_dec_kernel(qt_ref, kt_ref, v_ref, gate_ref, al_ref, be_ref, ng_ref, s_ref, o_ref, so_ref, orow_ref):
    qt = qt_ref[...]
    kt = kt_ref[...]
    v = v_ref[...]
    al = al_ref[...]
    be = be_ref[...]
    for bb in range(DEC_BLOCK):
        s = s_ref[bb]
        kc = kt[:, bb:bb + 1]
        a = al[bb:bb + 1, :]
        ks = jnp.sum(kc * s, axis=0, keepdims=True)
        delta = be[bb:bb + 1, :] * (v[bb:bb + 1, :] - a * ks)
        s_new = a * s + kc * delta
        so_ref[bb] = s_new
        orow_ref[bb:bb + 1, :] = jnp.sum(qt[:, bb:bb + 1] * s_new, axis=0, keepdims=True)
    o_ref[...] = _rms_gate(orow_ref[...], ng_ref[...], gate_ref[...])


def _gdn_decode(proj2, hist, state, conv_w, alog_row, dtb_row, norm_g):
    nbatch = proj2.shape[0]
    nh = C_HEADS
    full = lambda shape: pl.BlockSpec(shape, lambda: tuple(0 for _ in shape))
    hspec = full((nbatch, C_CONV_DIM))
    qn, kn, vc, ab = pl.pallas_call(
        _gdn_prep_kernel,
        grid=(),
        in_specs=[
            full((nbatch, ODD_IN_PAD)), hspec, hspec, hspec,
            full((C_CONV, C_CONV_DIM)), full((1, LANES)), full((1, LANES)),
        ],
        out_specs=[full((nbatch, C_QK)), full((nbatch, C_QK)), full((nbatch, C_QK)), full((nbatch, LANES))],
        out_shape=[
            jax.ShapeDtypeStruct((nbatch, C_QK), F32),
            jax.ShapeDtypeStruct((nbatch, C_QK), F32),
            jax.ShapeDtypeStruct((nbatch, C_QK), F32),
            jax.ShapeDtypeStruct((nbatch, LANES), F32),
        ],
        compiler_params=pltpu.CompilerParams(vmem_limit_bytes=VMEM_LIMIT),
        name="gdn_decode_prep",
    )(proj2, hist[:, 0], hist[:, 1], hist[:, 2], conv_w, alog_row, dtb_row)
    qt = _cols(qn, nh)
    kt = _cols(kn, nh)
    al = jnp.broadcast_to(ab[:, :nh].T[:, :, None], (nh, nbatch, LANES))
    be = jnp.broadcast_to(ab[:, nh:2 * nh].T[:, :, None], (nh, nbatch, LANES))
    colspec = pl.BlockSpec((None, None, C_DIM, DEC_BLOCK), lambda h, j: (h, j, 0, 0))
    sspec = pl.BlockSpec((DEC_BLOCK, None, C_DIM, C_DIM), lambda h, j: (j, h, 0, 0))
    rspec = pl.BlockSpec((None, DEC_BLOCK, LANES), lambda h, j: (h, j, 0))
    gcol = C_CONV_DIM // LANES
    o, s_new = pl.pallas_call(
        _gdn_dec_kernel,
        grid=(nh, nbatch // DEC_BLOCK),
        in_specs=[
            colspec, colspec,
            pl.BlockSpec((DEC_BLOCK, C_DIM), lambda h, j: (j, h)),
            pl.BlockSpec((DEC_BLOCK, C_DIM), lambda h, j: (j, gcol + h)),
            rspec, rspec,
            pl.BlockSpec((1, C_DIM), lambda h, j: (0, 0)),
            sspec,
        ],
        out_specs=[pl.BlockSpec((DEC_BLOCK, C_DIM), lambda h, j: (j, h)), sspec],
        out_shape=[
            jax.ShapeDtypeStruct((nbatch, C_QK), F32),
            jax.ShapeDtypeStruct(state.shape, F32),
        ],
        scratch_shapes=[pltpu.VMEM((DEC_BLOCK, C_DIM), F32)],
        compiler_params=_params(("parallel", "parallel")),
        name="gdn_decode",
    )(qt, kt, vc, proj2, al, be, norm_g.reshape(1, C_DIM), state)
    return o, s_new


def _trunk(x, mods, caches, p, tm):
    decode = caches is not None
    m = mods[0]
    ffn = functools.partial(_ffn, w_up=p["w_up"], w_down=p["w_down"], ln_g=p["ln_g"], ln_b=p["ln_b"], tm=tm)
    x = ffn(x, m, 0, layer=0, sub=0)
    if decode:
        proj = _inproj(x, m, 3, p["even_in"], tm)
        o_a, new_k, new_v = _swa_decode(proj[0], caches[0][0], caches[1][0], p["sinks"])
        o_b, s_hgrn = _hgrn_decode(proj[0], caches[2][0], p["lower"], p["hgrn_norm_g"])
        o_a, o_b = o_a[None], o_b[None]
    else:
        bsz = x.shape[0]
        proj = _inproj_even(x, m, 3, p["even_in"], p["lower"], tm)
        o_a = _swa_prompt(proj, p["sinks"])
        new_k = proj[:, -WINDOW:, A_Q:A_Q + A_KV].reshape(bsz, WINDOW, A_KV_HEADS, A_HEAD_DIM)
        new_v = proj[:, -WINDOW:, A_Q + A_KV:A_Q + 2 * A_KV].reshape(bsz, WINDOW, A_KV_HEADS, A_HEAD_DIM)
        o_b, s_hgrn = _hgrn_prompt(proj, p["hgrn_norm_g"])
    x = ffn(x, m, 6, layer=0, sub=1, mixer=(o_a, 0, o_b, 0, p["even_out"], 5))
    m = mods[1]
    x = ffn(x, m, 0, layer=1, sub=0)
    if decode:
        proj = _inproj(x, m, 3, p["odd_in"], tm)
        hist = caches[4][0]
        o_c, s_gdn = _gdn_decode(proj[0], hist, caches[3][0], p["conv_w"], p["alog_row"], p["dtb_row"],
                                 p["gdn_norm_g"])
        o_c = o_c[None]
        new_hist = jnp.concatenate([hist[:, 1:], proj[0][:, None, :C_CONV_DIM]], axis=1)
    else:
        proj, last_rows = _inproj_gdn(x, m, 3, p["odd_in"], p["conv_w"], tm)
        o_c, s_gdn = _gdn_prompt(proj, p["alog_row"], p["dtb_row"], p["gdn_norm_g"])
        new_hist = last_rows[:, -(C_CONV - 1):]
    x = ffn(x, m, 6, layer=1, sub=1, mixer=(o_c, 0, o_c, 1, p["odd_out"], 5))
    return x, new_k[None], new_v[None], s_hgrn[None], s_gdn[None], new_hist[None]


def kernel(x_prompt, x_sample, cache_swa_k, cache_swa_v, state_hgrn, state_gdn, state_gdn_conv, c_prompt, c_sample, ada_w, ada_b, ln_g, ln_b, ffn_w_up, ffn_w_down, even_w_in, even_w_out, swa_sinks, hgrn_norm_g, hgrn_lb_logits, odd_w_in, odd_w_out, gdn_conv_w, gdn_a_log, gdn_dt_bias, gdn_norm_g):
    n_prompt = c_prompt.shape[0]
    n_sample = c_sample.shape[0]
    assert n_sample % 8 == 0
    pad_rows = (-n_prompt) % 8
    c_all = jnp.concatenate([c_sample, c_prompt, jnp.zeros((pad_rows, D_MODEL), F32)], axis=0)
    mods_s, mods_p = _ada_mods(c_all, n_sample, ada_w, ada_b)
    mods_p = mods_p[:, :n_prompt].reshape(DEPTH, n_prompt, 1, N_MOD * D_MODEL)
    mods_s = mods_s.reshape(DEPTH, 1, n_sample, N_MOD * D_MODEL)

    probs = jax.nn.softmax(hgrn_lb_logits.astype(F32), axis=0)
    lower = (jnp.cumsum(probs, axis=0)[1:] - probs[0])[0]

    def lane_row(v):
        return jnp.pad(v.astype(F32), (0, LANES - v.shape[0])).reshape(1, LANES)

    p = dict(
        w_up=ffn_w_up.astype(BF16), w_down=ffn_w_down.astype(BF16),
        ln_g=ln_g, ln_b=ln_b,
        even_in=even_w_in[0].astype(BF16), even_out=even_w_out[0].astype(BF16),
        odd_in=jnp.concatenate([odd_w_in[0].astype(BF16), jnp.zeros((D_MODEL, ODD_IN_PAD - ODD_IN), BF16)], axis=1),
        odd_out=odd_w_out[0].astype(BF16),
        sinks=swa_sinks[0], lower=lower, hgrn_norm_g=hgrn_norm_g[0],
        conv_w=gdn_conv_w[0], alog_row=lane_row(gdn_a_log[0]), dtb_row=lane_row(gdn_dt_bias[0]),
        gdn_norm_g=gdn_norm_g[0],
    )
    y_p, p_k, p_v, p_hgrn, p_gdn, p_conv = _trunk(x_prompt, mods_p, None, p, 512)
    caches = (cache_swa_k, cache_swa_v, state_hgrn, state_gdn, state_gdn_conv)
    x_s = x_sample.reshape(1, n_sample, D_MODEL)
    y_s, s_k, s_v, s_hgrn, s_gdn, s_conv = _trunk(x_s, mods_s, caches, p, n_sample)
    y_s = y_s.reshape(n_sample, 1, D_MODEL)
    return (y_p, y_s, p_k, p_v, p_hgrn, p_gdn, p_conv, s_k, s_v, s_hgrn, s_gdn, s_conv)
```
